```python
import math
import jax
import jax.numpy as jnp
from jax import lax
import numpy as np

D_MODEL = 2048
BATCH = 1
SEQ = 8192
DEPTH = 1

N_META = 16
CHUNK = 128
PREFIX = CHUNK
PAD = PREFIX - N_META
Q_BLOCK = 128
GRID_W = 64
ROPE_THETA = 10000.0
NORM_EPS = 1e-6

MIX_WIDTH = D_MODEL
RET_WIDTH = MIX_WIDTH // 2
ATTN_WIDTH = MIX_WIDTH - RET_WIDTH
RET_V_DIM = 128
RET_HEADS = RET_WIDTH // RET_V_DIM
RET_QK_DIM = RET_V_DIM // 2
ATTN_HEAD_DIM = 128
ATTN_HEADS = ATTN_WIDTH // ATTN_HEAD_DIM
ATTN_KV_HEADS = 2

SPLIT_SIZES = (RET_HEADS * RET_QK_DIM, RET_HEADS * RET_QK_DIM, RET_WIDTH, RET_WIDTH,
               ATTN_WIDTH, ATTN_KV_HEADS * ATTN_HEAD_DIM, ATTN_KV_HEADS * ATTN_HEAD_DIM)
SPLIT_POINTS = tuple(int(s) for s in np.cumsum(SPLIT_SIZES)[:-1])
IN_WIDTH = sum(SPLIT_SIZES)

N_GROUPS = 8
EXPERTS_PER_GROUP = 8
N_EXPERTS = N_GROUPS * EXPERTS_PER_GROUP
TOP_K = 2
D_EXPERT = D_MODEL // 2
EXPERT_BLOCK = 128

kernel_name = "hybrid_retention_axial_gqa_hmoe_block"


def rmsnorm(x, g, eps=NORM_EPS):
    xf = x.astype(jnp.float32)
    y = xf * lax.rsqrt(jnp.mean(xf * xf, axis=-1, keepdims=True) + eps)
    return y.astype(x.dtype) * g


def rope_1d(x, ang):
    n = ang.shape[-1]
    cos = jnp.cos(ang)[None, :, None, :].astype(x.dtype)
    sin = jnp.sin(ang)[None, :, None, :].astype(x.dtype)
    x1, x2 = x[..., :n], x[..., n:]
    return jnp.concatenate([x1 * cos - x2 * sin, x2 * cos + x1 * sin], axis=-1)


def axial_rope(x, pos_row, pos_col):
    half = x.shape[-1] // 2
    freqs = ROPE_THETA ** (-jnp.arange(0, half, 2, dtype=jnp.float32) / half)
    xr = rope_1d(x[..., :half], pos_row[:, None] * freqs[None, :])
    xc = rope_1d(x[..., half:], pos_col[:, None] * freqs[None, :])
    return jnp.concatenate([xr, xc], axis=-1)


def retention_direction(q, k, v, log_gamma, strict):
    B, L, H, dk = q.shape
    dv = v.shape[-1]
    n = L // CHUNK
    q = q.reshape(B, n, CHUNK, H, dk)
    k = k.reshape(B, n, CHUNK, H, dk)
    v = v.reshape(B, n, CHUNK, H, dv)
    i = jnp.arange(CHUNK, dtype=jnp.float32)
    diff = i[:, None] - i[None, :]
    mask = (diff > 0) if strict else (diff >= 0)
    dec = jnp.where(mask[None], jnp.exp(log_gamma[:, None, None] * jnp.maximum(diff, 0.0)[None]), 0.0)
    scores = jnp.einsum('bnihd,bnjhd->bnhij', q, k) * dec
    intra = jnp.einsum('bnhij,bnjhe->bnihe', scores, v)
    k_dec = k * jnp.exp(log_gamma[None, :] * (CHUNK - 1 - i)[:, None])[None, None, :, :, None]
    upd = jnp.einsum('bnjhd,bnjhe->bnhde', k_dec, v)
    chunk_decay = jnp.exp(log_gamma * CHUNK)[None, :, None, None]

    def step(state, u):
        return chunk_decay * state + u, state

    _, s_prev = lax.scan(step, jnp.zeros((B, H, dk, dv), upd.dtype), jnp.moveaxis(upd, 1, 0))
    s_prev = jnp.moveaxis(s_prev, 0, 1)
    q_dec = q * jnp.exp(log_gamma[None, :] * (i + 1.0)[:, None])[None, None, :, :, None]
    cross = jnp.einsum('bnihd,bnhde->bnihe', q_dec, s_prev)
    return (intra + cross).reshape(B, L, H, dv)


def bidirectional_retention(q, k, v, logit_fwd, logit_bwd):
    lg_f = jax.nn.log_sigmoid(logit_fwd.astype(jnp.float32))
    lg_b = jax.nn.log_sigmoid(logit_bwd.astype(jnp.float32))
    fwd = retention_direction(q, k, v, lg_f, False)
    flip = lambda t: jnp.flip(t, axis=1)
    bwd = flip(retention_direction(flip(q), flip(k), flip(v), lg_b, True))
    return fwd + bwd


def block_gqa_attention(q, k, v):
    B, Lq, Hq, dh = q.shape
    Hkv = k.shape[2]
    G = Hq // Hkv
    nb = Lq // Q_BLOCK
    qb = jnp.moveaxis(q.reshape(B, nb, Q_BLOCK, Hkv, G, dh), 1, 0)
    scale = dh ** -0.5

    def one_block(qblk):
        s = jnp.einsum('bqkgd,bskd->bkgqs', qblk, k).astype(jnp.float32) * scale
        p = jax.nn.softmax(s, axis=-1).astype(v.dtype)
        return jnp.einsum('bkgqs,bskd->bqkgd', p, v)

    o = lax.map(one_block, qb)
    return jnp.moveaxis(o, 0, 1).reshape(B, Lq, Hq * dh)


def hierarchical_moe(xt, wg, bg, we, be, w_gate, w_up, w_down):
    T, D = xt.shape
    xf = xt.astype(jnp.float32)
    g_prob = jax.nn.softmax(xf @ wg.astype(jnp.float32) + bg.astype(jnp.float32), axis=-1)
    g_w, g_idx = lax.top_k(g_prob, 1)
    e_logits = (xf @ we.astype(jnp.float32) + be.astype(jnp.float32)).reshape(T, N_GROUPS, EXPERTS_PER_GROUP)
    e_in = jnp.take_along_axis(e_logits, g_idx[:, :, None], axis=1)[:, 0]
    top_v, top_i = lax.top_k(e_in, TOP_K)
    gate = g_w * jax.nn.softmax(top_v, axis=-1)
    expert = (g_idx * EXPERTS_PER_GROUP + top_i).reshape(-1)
    token = jnp.repeat(jnp.arange(T), TOP_K)
    weight = gate.reshape(-1)
    tk = T * TOP_K
    order = jnp.argsort(expert, stable=True)
    se, st, sw = expert[order], token[order], weight[order]
    counts = jax.ops.segment_sum(jnp.ones_like(expert), expert, num_segments=N_EXPERTS)
    starts = jnp.cumsum(counts) - counts
    padded = (counts + EXPERT_BLOCK - 1) // EXPERT_BLOCK * EXPERT_BLOCK
    pad_end = jnp.cumsum(padded)
    pad_start = pad_end - padded
    dest = pad_start[se] + jnp.arange(tk) - starts[se]
    n_blk = -(-tk // EXPERT_BLOCK) + N_EXPERTS
    rows = n_blk * EXPERT_BLOCK
    buf = jnp.zeros((rows, D), xt.dtype).at[dest].set(xt[st])
    blk_expert = jnp.minimum(
        jnp.searchsorted(pad_end, jnp.arange(n_blk) * EXPERT_BLOCK, side='right'), N_EXPERTS - 1)

    def run_block(args):
        xb, e = args
        hb = jax.nn.silu(xb @ w_gate[e]) * (xb @ w_up[e])
        return hb @ w_down[e]

    out = lax.map(run_block, (buf.reshape(n_blk, EXPERT_BLOCK, D), blk_expert)).reshape(rows, D)
    return jnp.zeros((T, D), xt.dtype).at[st].add(out[dest] * sw[:, None].astype(xt.dtype))


def setup_inputs(seed: int = 0) -> dict:
    key = jax.random.key(seed)
    ks = jax.random.split(key, 20)
    nrm = lambda k, shape, scale: jax.random.normal(k, shape, jnp.float32) * scale
    gain = lambda k, shape: 1.0 + 0.02 * jax.random.normal(k, shape, jnp.float32)
    a = np.arange(5, 5 + RET_HEADS, dtype=np.float64)
    base_logit = jnp.asarray(np.log(2.0 ** a - 1.0).astype(np.float32))
    return {
        "x": nrm(ks[0], (BATCH, SEQ, D_MODEL), 1.0),
        "meta_tokens": nrm(ks[1], (N_META, D_MODEL), 1.0),
        "norm_mix_g": gain(ks[2], (DEPTH, D_MODEL)),
        "w_in": nrm(ks[3], (DEPTH, D_MODEL, IN_WIDTH), D_MODEL ** -0.5),
        "ret_decay_logit_fwd": base_logit[None] + nrm(ks[4], (DEPTH, RET_HEADS), 0.05),
        "ret_decay_logit_bwd": base_logit[None] + nrm(ks[5], (DEPTH, RET_HEADS), 0.05),
        "ret_norm_g": gain(ks[6], (DEPTH, RET_WIDTH)),
        "attn_q_norm_g": gain(ks[7], (DEPTH, ATTN_HEAD_DIM)),
        "attn_k_norm_g": gain(ks[8], (DEPTH, ATTN_HEAD_DIM)),
        "attn_out_norm_g": gain(ks[9], (DEPTH, ATTN_WIDTH)),
        "w_out": nrm(ks[10], (DEPTH, MIX_WIDTH, D_MODEL), MIX_WIDTH ** -0.5),
        "norm_ffn_g": gain(ks[11], (DEPTH, D_MODEL)),
        "router_group_w": nrm(ks[12], (DEPTH, D_MODEL, N_GROUPS), D_MODEL ** -0.5),
        "router_group_b": nrm(ks[13], (DEPTH, N_GROUPS), 0.01),
        "router_expert_w": nrm(ks[14], (DEPTH, D_MODEL, N_EXPERTS), D_MODEL ** -0.5),
        "router_expert_b": nrm(ks[15], (DEPTH, N_EXPERTS), 0.01),
        "expert_w_gate": nrm(ks[16], (DEPTH, N_EXPERTS, D_MODEL, D_EXPERT), D_MODEL ** -0.5),
        "expert_w_up": nrm(ks[17], (DEPTH, N_EXPERTS, D_MODEL, D_EXPERT), D_MODEL ** -0.5),
        "expert_w_down": nrm(ks[18], (DEPTH, N_EXPERTS, D_EXPERT, D_MODEL), D_EXPERT ** -0.5),
        "norm_final_g": gain(ks[19], (D_MODEL,)),
    }


def reference(x, meta_tokens, norm_mix_g, w_in, ret_decay_logit_fwd, ret_decay_logit_bwd,
              ret_norm_g, attn_q_norm_g, attn_k_norm_g, attn_out_norm_g, w_out, norm_ffn_g,
              router_group_w, router_group_b, router_expert_w, router_expert_b,
              expert_w_gate, expert_w_up, expert_w_down, norm_final_g):
    B, n_tok, D = x.shape
    rows = n_tok // GRID_W
    Lp = n_tok + PREFIX
    h = jnp.concatenate([
        jnp.zeros((B, PAD, D), x.dtype),
        jnp.broadcast_to(meta_tokens.astype(x.dtype)[None], (B, N_META, D)),
        x], axis=1)
    row_ids = jnp.repeat(jnp.arange(rows), GRID_W).astype(jnp.float32)
    col_ids = jnp.tile(jnp.arange(GRID_W), rows).astype(jnp.float32)
    zeros_p = jnp.zeros((PREFIX,), jnp.float32)
    pos_row = jnp.concatenate([zeros_p, row_ids])
    pos_col = jnp.concatenate([zeros_p, col_ids])
    key_valid = (jnp.arange(Lp) >= PAD).astype(x.dtype)[None, :, None, None]

    for l in range(DEPTH):
        a = rmsnorm(h, norm_mix_g[l])
        proj = a @ w_in[l]
        rq, rk, rv, rg, aq, ak, av = jnp.split(proj, SPLIT_POINTS, axis=-1)

        rq = axial_rope(rq.reshape(B, Lp, RET_HEADS, RET_QK_DIM), pos_row, pos_col)
        rk = axial_rope(rk.reshape(B, Lp, RET_HEADS, RET_QK_DIM), pos_row, pos_col)
        rk = rk * (RET_QK_DIM ** -0.5) * key_valid
        rv = rv.reshape(B, Lp, RET_HEADS, RET_V_DIM)
        ro = bidirectional_retention(rq, rk, rv, ret_decay_logit_fwd[l], ret_decay_logit_bwd[l])
        ro = ro.astype(jnp.float32)
        mu = jnp.mean(ro, axis=-1, keepdims=True)
        var = jnp.mean(jnp.square(ro - mu), axis=-1, keepdims=True)
        ro = ((ro - mu) * lax.rsqrt(var + 1e-5)).reshape(B, Lp, RET_WIDTH).astype(x.dtype)
        ret_out = jax.nn.silu(rg) * (ro * ret_norm_g[l])

        aq = axial_rope(rmsnorm(aq.reshape(B, Lp, ATTN_HEADS, ATTN_HEAD_DIM), attn_q_norm_g[l]),
                        pos_row, pos_col)
        ak = axial_rope(rmsnorm(ak.reshape(B, Lp, ATTN_KV_HEADS, ATTN_HEAD_DIM), attn_k_norm_g[l]),
                        pos_row, pos_col)[:, PAD:]
        av = av.reshape(B, Lp, ATTN_KV_HEADS, ATTN_HEAD_DIM)[:, PAD:]
        attn_out = rmsnorm(block_gqa_attention(aq, ak, av), attn_out_norm_g[l])

        mix = jnp.concatenate([ret_out, attn_out], axis=-1) @ w_out[l]
        h = h + mix

        f = rmsnorm(h, norm_ffn_g[l]).reshape(B * Lp, D)
        h = h + hierarchical_moe(f, router_group_w[l], router_group_b[l], router_expert_w[l],
                                 router_expert_b[l], expert_w_gate[l], expert_w_up[l],
                                 expert_w_down[l]).reshape(B, Lp, D)

    return rmsnorm(h, norm_final_g)[:, PREFIX:]
```

```python
import functools

import jax
import jax.numpy as jnp
from jax import lax
from jax.experimental import pallas as pl
from jax.experimental.pallas import tpu as pltpu

F32 = jnp.float32
BF16 = jnp.bfloat16

N_META = 16
CHUNK = 128
PREFIX = CHUNK
PAD = PREFIX - N_META
GRID_W = 64
ROPE_THETA = 10000.0
NORM_EPS = 1e-6
GROUPNORM_EPS = 1e-5

RET_HEADS = 8
RET_QK_DIM = 64
RET_V_DIM = 128
ATTN_HEADS = 8
ATTN_KV_HEADS = 2
ATTN_HEAD_DIM = 128
ATTN_GROUP = ATTN_HEADS // ATTN_KV_HEADS

N_GROUPS = 8
EXPERTS_PER_GROUP = 8
N_EXPERTS = N_GROUPS * EXPERTS_PER_GROUP
TOP_K = 2
EXPERT_BLOCK = 128

LANES = 128
PROJ_TILE = 512
VMEM_LIMIT = 56 * 1024 * 1024
NEG_BIG = -1e30


def _params(sem, vmem=VMEM_LIMIT):
    return pltpu.CompilerParams(dimension_semantics=sem, vmem_limit_bytes=vmem)


def _swap_halves(x, hw):
    w = x.shape[-1]
    lane = lax.broadcasted_iota(jnp.int32, x.shape, 1)
    first = (lane & hw) == 0
    return jnp.where(first, pltpu.roll(x, w - hw, 1), pltpu.roll(x, hw, 1))


def _rope_tables(n_tok, head_dim):
    half = head_dim // 2
    freqs = ROPE_THETA ** (-jnp.arange(0, half, 2, dtype=F32) / half)
    rows = n_tok // GRID_W
    row_ids = jnp.repeat(jnp.arange(rows), GRID_W).astype(F32)
    col_ids = jnp.tile(jnp.arange(GRID_W), rows).astype(F32)
    zeros_p = jnp.zeros((PREFIX,), F32)
    pos_row = jnp.concatenate([zeros_p, row_ids])
    pos_col = jnp.concatenate([zeros_p, col_ids])
    ang_r = pos_row[:, None] * freqs[None, :]
    ang_c = pos_col[:, None] * freqs[None, :]
    cos = jnp.concatenate([jnp.cos(ang_r), jnp.cos(ang_r), jnp.cos(ang_c), jnp.cos(ang_c)], axis=-1)
    sin = jnp.concatenate([-jnp.sin(ang_r), jnp.sin(ang_r), -jnp.sin(ang_c), jnp.sin(ang_c)], axis=-1)
    reps = LANES // head_dim
    return jnp.tile(cos, (1, reps)), jnp.tile(sin, (1, reps))


def _in_proj_kernel(x_ref, g_ref, w_ref, cr_ref, sr_ref, ca_ref, sa_ref, qg_ref, kg_ref,
                    oqk_ref, orest_ref, a_scr, *, tm):
    i = pl.program_id(0)
    j = pl.program_id(1)

    @pl.when(j == 0)
    def _():
        x = x_ref[...]
        ms = jnp.mean(x * x, axis=-1, keepdims=True)
        a_scr[...] = ((x * lax.rsqrt(ms + NORM_EPS)) * g_ref[...]).astype(BF16)

    acc = jnp.dot(a_scr[...], w_ref[...], preferred_element_type=F32)
    n_lane_groups = PROJ_TILE // LANES

    def tiled(ref):
        return jnp.concatenate([ref[...]] * n_lane_groups, axis=1)

    @pl.when(j < 2)
    def _():
        y = acc * tiled(cr_ref) + _swap_halves(acc, RET_QK_DIM // 4) * tiled(sr_ref)
        row = i * tm + lax.broadcasted_iota(jnp.int32, (tm, 1), 0)
        k_scale = jnp.where(row >= PAD, RET_QK_DIM ** -0.5, 0.0).astype(F32)
        y = y * jnp.where(j == 1, k_scale, jnp.ones_like(k_scale))
        lo = lax.broadcasted_iota(jnp.int32, (tm, LANES), 1) < RET_QK_DIM
        pieces = []
        for c in range(n_lane_groups):
            xg = y[:, c * LANES:(c + 1) * LANES]
            xr = pltpu.roll(xg, RET_QK_DIM, 1)
            pieces.append(jnp.where(lo, xg, xr))
            pieces.append(jnp.where(lo, xr, xg))
        oqk_ref[...] = jnp.concatenate(pieces, axis=1).astype(BF16)

    @pl.when((j >= 2) & (j < 6))
    def _():
        orest_ref[...] = acc.astype(BF16)

    def qk_norm_rope(xg, gain_ref):
        ms = jnp.mean(xg * xg, axis=-1, keepdims=True)
        xn = (xg * lax.rsqrt(ms + NORM_EPS)) * gain_ref[...]
        return xn * ca_ref[...] + _swap_halves(xn, ATTN_HEAD_DIM // 4) * sa_ref[...]

    @pl.when((j == 6) | (j == 7))
    def _():
        pieces = [qk_norm_rope(acc[:, c * LANES:(c + 1) * LANES], qg_ref) * (ATTN_HEAD_DIM ** -0.5)
                  for c in range(n_lane_groups)]
        orest_ref[...] = jnp.concatenate(pieces, axis=1).astype(BF16)

    @pl.when(j == 8)
    def _():
        pieces = [qk_norm_rope(acc[:, c * LANES:(c + 1) * LANES], kg_ref) for c in range(ATTN_KV_HEADS)]
        pieces.append(acc[:, ATTN_KV_HEADS * LANES:])
        orest_ref[...] = jnp.concatenate(pieces, axis=1).astype(BF16)


def _in_proj(h, g, w_bf, cr, sr, ca, sa, qg, kg, *, tm):
    lp, d = h.shape
    n_out = w_bf.shape[1]
    nj = n_out // PROJ_TILE
    assert lp % tm == 0 and n_out % PROJ_TILE == 0 and nj == 9
    qk_w = 2 * 2 * RET_HEADS * RET_QK_DIM
    rest_w = n_out - 2 * RET_HEADS * RET_QK_DIM
    row_tab = pl.BlockSpec((tm, LANES), lambda i, j: (i, 0))
    vec = lambda n: pl.BlockSpec((1, n), lambda i, j: (0, 0))
    return pl.pallas_call(
        functools.partial(_in_proj_kernel, tm=tm),
        grid=(lp // tm, nj),
        in_specs=[
            pl.BlockSpec((tm, d), lambda i, j: (i, 0)),
            vec(d),
            pl.BlockSpec((d, PROJ_TILE), lambda i, j: (0, j)),
            row_tab, row_tab, row_tab, row_tab,
            vec(LANES), vec(LANES),
        ],
        out_specs=[
            pl.BlockSpec((tm, 2 * PROJ_TILE), lambda i, j: (i, jnp.minimum(j, 1))),
            pl.BlockSpec((tm, PROJ_TILE), lambda i, j: (i, jnp.maximum(j - 2, 0))),
        ],
        out_shape=[jax.ShapeDtypeStruct((lp, qk_w), BF16), jax.ShapeDtypeStruct((lp, rest_w), BF16)],
        scratch_shapes=[pltpu.VMEM((tm, d), BF16)],
        compiler_params=_params(("arbitrary", "arbitrary")),
        name="in_proj",
    )(h, g, w_bf, cr, sr, ca, sa, qg, kg)


def _retention_kernel(lg_ref, q_ref, k_ref, v_ref, g_ref, gn_ref, o_ref, st_ref, dk_ref, dq_ref, dm_ref,
                      *, n_chunks):
    h = pl.program_id(0)
    lgf = lg_ref[0, h]
    lgb = lg_ref[1, h]
    half = CHUNK // 2
    row = lax.broadcasted_iota(jnp.int32, (CHUNK, CHUNK), 0).astype(F32)
    lane = lax.broadcasted_iota(jnp.int32, (CHUNK, CHUNK), 1).astype(F32)
    lane_lo = lane < half
    dk_ref[...] = jnp.exp(jnp.where(lane_lo, lgf * (CHUNK - 1.0 - row), lgb * row))
    dq_ref[...] = jnp.exp(jnp.where(lane_lo, lgf * (row + 1.0), lgb * (CHUNK - row)))
    diff = row - lane
    dm_ref[...] = 0.5 * jnp.where(diff >= 0, jnp.exp(lgf * jnp.maximum(diff, 0.0)),
                                  jnp.exp(lgb * jnp.maximum(-diff, 0.0)))
    cf = jnp.exp(jnp.full((half, CHUNK), lgf * CHUNK, F32))
    cb = jnp.exp(jnp.full((half, CHUNK), lgb * CHUNK, F32))

    def rows(n):
        return pl.ds(pl.multiple_of(n * CHUNK, CHUNK), CHUNK)

    def updates(n, c):
        kd = k_ref[rows(n), :].astype(F32) * dk_ref[...]
        st_ref[n] = jnp.dot(kd.T.astype(BF16), v_ref[rows(n), :], preferred_element_type=F32)
        return c

    lax.fori_loop(0, n_chunks, updates, 0)

    def scan_fwd(n, s):
        u = st_ref[n, 0:half, :]
        st_ref[n, 0:half, :] = s
        return cf * s + u

    lax.fori_loop(0, n_chunks, scan_fwd, jnp.zeros((half, CHUNK), F32))

    def scan_bwd(t, s):
        n = n_chunks - 1 - t
        u = st_ref[n, half:CHUNK, :]
        st_ref[n, half:CHUNK, :] = s
        return cb * s + u

    lax.fori_loop(0, n_chunks, scan_bwd, jnp.zeros((half, CHUNK), F32))

    def outputs(n, c):
        q = q_ref[rows(n), :]
        k = k_ref[rows(n), :]
        v = v_ref[rows(n), :]
        s2 = lax.dot_general(q, k, (((1,), (1,)), ((), ())), preferred_element_type=F32)
        intra = jnp.dot((s2 * dm_ref[...]).astype(BF16), v, preferred_element_type=F32)
        qd = (q.astype(F32) * dq_ref[...]).astype(BF16)
        cross = jnp.dot(qd, st_ref[n].astype(BF16), preferred_element_type=F32)
        o = intra + cross
        mu = jnp.mean(o, axis=-1, keepdims=True)
        dlt = o - mu
        var = jnp.mean(dlt * dlt, axis=-1, keepdims=True)
        on = dlt * lax.rsqrt(var + GROUPNORM_EPS)
        gate = g_ref[rows(n), :].astype(F32)
        o_ref[rows(n), :] = ((gate * jax.nn.sigmoid(gate)) * (on * gn_ref[...])).astype(BF16)
        return c

    lax.fori_loop(0, n_chunks, outputs, 0)


def _retention(log_gamma, qk, rest, gn_g):
    lp = qk.shape[0]
    n_chunks = lp // CHUNK
    col = lambda off: pl.BlockSpec((lp, LANES), lambda h, lg: (0, h + off))
    return pl.pallas_call(
        functools.partial(_retention_kernel, n_chunks=n_chunks),
        grid_spec=pltpu.PrefetchScalarGridSpec(
            num_scalar_prefetch=1,
            grid=(RET_HEADS,),
            in_specs=[col(0), col(RET_HEADS), col(0), col(RET_HEADS),
                      pl.BlockSpec((1, LANES), lambda h, lg: (0, h))],
            out_specs=pl.BlockSpec((lp, LANES), lambda h, lg: (0, h)),
            scratch_shapes=[pltpu.VMEM((n_chunks, CHUNK, CHUNK), F32),
                            pltpu.VMEM((CHUNK, CHUNK), F32),
                            pltpu.VMEM((CHUNK, CHUNK), F32),
                            pltpu.VMEM((CHUNK, CHUNK), F32)],
        ),
        out_shape=jax.ShapeDtypeStruct((lp, RET_HEADS * RET_V_DIM), BF16),
        compiler_params=_params(("arbitrary",)),
        name="retention",
    )(log_gamma, qk, qk, rest, rest, gn_g)


def _attention_kernel(q_ref, k_ref, v_ref, o_ref, m_scr, l_scr, acc_scr, *, tq, tk, n_kc):
    q = jnp.concatenate([q_ref[:, g * LANES:(g + 1) * LANES] for g in range(ATTN_GROUP)], axis=0)

    def chunk(c, masked):
        ks = pl.ds(pl.multiple_of(c * tk, tk), tk)
        s = lax.dot_general(q, k_ref[ks, :], (((1,), (1,)), ((), ())), preferred_element_type=F32)
        if masked:
            key = lax.broadcasted_iota(jnp.int32, (1, tk), 1)
            s = jnp.where(key >= PAD, s, NEG_BIG)
        m_old = m_scr[...]
        m_new = jnp.maximum(m_old, jnp.max(s, axis=-1, keepdims=True))
        alpha = jnp.exp(m_old - m_new)
        p = jnp.exp(s - m_new)
        l_scr[...] = alpha * l_scr[...] + jnp.sum(p, axis=-1, keepdims=True)
        acc_scr[...] = alpha * acc_scr[...] + jnp.dot(p.astype(BF16), v_ref[ks, :],
                                                      preferred_element_type=F32)
        m_scr[...] = m_new

    m_scr[...] = jnp.full(m_scr.shape, NEG_BIG, F32)
    l_scr[...] = jnp.zeros(l_scr.shape, F32)
    acc_scr[...] = jnp.zeros(acc_scr.shape, F32)
    chunk(0, True)

    def body(c, carry):
        chunk(c, False)
        return carry

    lax.fori_loop(1, n_kc, body, 0)
    o = acc_scr[...] / l_scr[...]
    for g in range(ATTN_GROUP):
        o_ref[:, g * LANES:(g + 1) * LANES] = o[g * tq:(g + 1) * tq, :]


def _attention(rest, *, tq, tk):
    lp = rest.shape[0]
    assert lp % tq == 0 and lp % tk == 0 and tk >= PAD + 1
    gw = ATTN_GROUP * ATTN_HEAD_DIM
    q_off = (2 * RET_HEADS * RET_V_DIM) // gw
    k_off = (2 * RET_HEADS * RET_V_DIM + ATTN_HEADS * ATTN_HEAD_DIM) // LANES
    v_off = k_off + ATTN_KV_HEADS
    return pl.pallas_call(
        functools.partial(_attention_kernel, tq=tq, tk=tk, n_kc=lp // tk),
        grid=(ATTN_KV_HEADS, lp // tq),
        in_specs=[pl.BlockSpec((tq, gw), lambda kv, i: (i, q_off + kv)),
                  pl.BlockSpec((lp, LANES), lambda kv, i: (0, k_off + kv)),
                  pl.BlockSpec((lp, LANES), lambda kv, i: (0, v_off + kv))],
        out_specs=pl.BlockSpec((tq, gw), lambda kv, i: (i, kv)),
        out_shape=jax.ShapeDtypeStruct((lp, ATTN_HEADS * ATTN_HEAD_DIM), F32),
        scratch_shapes=[pltpu.VMEM((ATTN_GROUP * tq, 1), F32),
                        pltpu.VMEM((ATTN_GROUP * tq, 1), F32),
                        pltpu.VMEM((ATTN_GROUP * tq, LANES), F32)],
        compiler_params=_params(("arbitrary", "arbitrary")),
        name="attention",
    )(rest, rest, rest)


def _mix_router_kernel(ret_ref, att_ref, ang_ref, wo_ref, h_ref, fg_ref, wr_ref, br_ref,
                       h1_ref, f_ref, route_ref, cnt_ref, carry_scr, *, tm):
    i = pl.program_id(0)

    @pl.when(i == 0)
    def _():
        carry_scr[...] = jnp.zeros(carry_scr.shape, F32)

    att = att_ref[...]
    ms = jnp.mean(att * att, axis=-1, keepdims=True)
    att_n = ((att * lax.rsqrt(ms + NORM_EPS)) * ang_ref[...]).astype(BF16)
    mix_in = jnp.concatenate([ret_ref[...], att_n], axis=1)
    h1 = h_ref[...] + jnp.dot(mix_in, wo_ref[...], preferred_element_type=F32)
    h1_ref[...] = h1
    ms1 = jnp.mean(h1 * h1, axis=-1, keepdims=True)
    f = (h1 * lax.rsqrt(ms1 + NORM_EPS)) * fg_ref[...]
    f_ref[...] = f

    logits = jnp.dot(f, wr_ref[...], preferred_element_type=F32,
                     precision=lax.Precision.HIGHEST) + br_ref[...]
    lane = lax.broadcasted_iota(jnp.int32, (tm, LANES), 1).astype(F32)
    far = float(LANES)
    g_logit = jnp.where(lane < N_GROUPS, logits, NEG_BIG)
    g_max = jnp.max(g_logit, axis=-1, keepdims=True)
    g_w = 1.0 / jnp.sum(jnp.exp(g_logit - g_max), axis=-1, keepdims=True)
    g_idx = jnp.min(jnp.where(g_logit == g_max, lane, far), axis=-1, keepdims=True)
    first = N_GROUPS + g_idx * EXPERTS_PER_GROUP
    e_logit = jnp.where((lane >= first) & (lane < first + EXPERTS_PER_GROUP), logits, NEG_BIG)
    v1 = jnp.max(e_logit, axis=-1, keepdims=True)
    i1 = jnp.min(jnp.where(e_logit == v1, lane, far), axis=-1, keepdims=True)
    e_rest = jnp.where(lane == i1, NEG_BIG, e_logit)
    v2 = jnp.max(e_rest, axis=-1, keepdims=True)
    i2 = jnp.min(jnp.where(e_rest == v2, lane, far), axis=-1, keepdims=True)
    e2 = jnp.exp(v2 - v1)
    w1 = g_w / (1.0 + e2)
    w2 = g_w * e2 / (1.0 + e2)

    oh1 = (lane == i1).astype(F32)
    oh2 = (lane == i2).astype(F32)
    oh = oh1 + oh2
    r_i = lax.broadcasted_iota(jnp.int32, (tm, tm), 0)
    c_i = lax.broadcasted_iota(jnp.int32, (tm, tm), 1)
    lower = (c_i < r_i).astype(BF16)
    before = jnp.dot(lower, oh.astype(BF16), preferred_element_type=F32) + carry_scr[...]
    rank1 = jnp.sum(before * oh1, axis=-1, keepdims=True)
    rank2 = jnp.sum(before * oh2, axis=-1, keepdims=True)
    carry_scr[...] = carry_scr[...] + jnp.sum(oh, axis=0, keepdims=True)
    cnt_ref[...] = carry_scr[...]

    route = jnp.where(lane == 0, i1 - N_GROUPS, 0.0)
    route = jnp.where(lane == 1, i2 - N_GROUPS, route)
    route = jnp.where(lane == 2, w1, route)
    route = jnp.where(lane == 3, w2, route)
    route = jnp.where(lane == 4, rank1, route)
    route = jnp.where(lane == 5, rank2, route)
    route_ref[...] = route


def _mix_router(ret, att, ang, wo_bf, h, fg, wr, br, *, tm):
    lp, d = h.shape
    assert lp % tm == 0
    rowblk = lambda n: pl.BlockSpec((tm, n), lambda i: (i, 0))
    vec = lambda n: pl.BlockSpec((1, n), lambda i: (0, 0))
    full = lambda a: pl.BlockSpec(a.shape, lambda i: (0, 0))
    return pl.pallas_call(
        functools.partial(_mix_router_kernel, tm=tm),
        grid=(lp // tm,),
        in_specs=[rowblk(ret.shape[1]), rowblk(att.shape[1]), vec(att.shape[1]), full(wo_bf),
                  rowblk(d), vec(d), full(wr), vec(LANES)],
        out_specs=[rowblk(d), rowblk(d), rowblk(LANES), vec(LANES)],
        out_shape=[jax.ShapeDtypeStruct((lp, d), F32), jax.ShapeDtypeStruct((lp, d), F32),
                   jax.ShapeDtypeStruct((lp, LANES), F32), jax.ShapeDtypeStruct((1, LANES), F32)],
        scratch_shapes=[pltpu.VMEM((1, LANES), F32)],
        compiler_params=_params(("arbitrary",)),
        name="mix_router",
    )(ret, att, ang, wo_bf, h, fg, wr, br)


def _row_copy(src_ref, src_row, dst_ref, dst_row, sem):
    return pltpu.make_async_copy(src_ref.at[pl.ds(src_row, 1), :], dst_ref.at[pl.ds(dst_row, 1), :], sem)


def _dispatch_kernel(dest_ref, f_ref, buf_in_ref, buf_ref, sem, *, tm):
    del buf_in_ref
    base = pl.program_id(0) * tm

    def issue(t, c):
        for k in range(TOP_K):
            _row_copy(f_ref, t, buf_ref, dest_ref[TOP_K * (base + t) + k], sem).start()
        return c

    lax.fori_loop(0, tm, issue, 0)

    def drain(t, c):
        for k in range(TOP_K):
            _row_copy(f_ref, t, buf_ref, dest_ref[TOP_K * (base + t) + k], sem).wait()
        return c

    lax.fori_loop(0, tm, drain, 0)


def _dispatch(dest, f, buf_zero, *, tm):
    lp, d = f.shape
    return pl.pallas_call(
        functools.partial(_dispatch_kernel, tm=tm),
        grid_spec=pltpu.PrefetchScalarGridSpec(
            num_scalar_prefetch=1,
            grid=(lp // tm,),
            in_specs=[pl.BlockSpec((tm, d), lambda i, dst: (i, 0)),
                      pl.BlockSpec(memory_space=pl.ANY)],
            out_specs=pl.BlockSpec(memory_space=pl.ANY),
            scratch_shapes=[pltpu.SemaphoreType.DMA(())],
        ),
        out_shape=jax.ShapeDtypeStruct(buf_zero.shape, buf_zero.dtype),
        input_output_aliases={2: 0},
        compiler_params=_params(("arbitrary",)),
        name="dispatch",
    )(dest, f, buf_zero)


def _expert_changed(be_ref, b):
    return (b == 0) | (be_ref[b] != be_ref[jnp.maximum(b - 1, 0)])


def _ffn_up_kernel(be_ref, nu_ref, x_ref, wg_ref, wu_ref, h_ref, wg_bf, wu_bf):
    b = pl.program_id(0)

    @pl.when(_expert_changed(be_ref, b))
    def _():
        wg_bf[...] = wg_ref[0].astype(BF16)
        wu_bf[...] = wu_ref[0].astype(BF16)

    @pl.when(b < nu_ref[0])
    def _():
        x = x_ref[...].astype(BF16)
        gate = jnp.dot(x, wg_bf[...], preferred_element_type=F32)
        up = jnp.dot(x, wu_bf[...], preferred_element_type=F32)
        h_ref[...] = ((gate * jax.nn.sigmoid(gate)) * up).astype(BF16)

    @pl.when(b >= nu_ref[0])
    def _():
        h_ref[...] = jnp.zeros(h_ref.shape, BF16)


def _ffn_down_kernel(be_ref, nu_ref, h_ref, wd_ref, y_ref, wd_bf):
    b = pl.program_id(0)

    @pl.when(_expert_changed(be_ref, b))
    def _():
        wd_bf[...] = wd_ref[0].astype(BF16)

    @pl.when(b < nu_ref[0])
    def _():
        y_ref[...] = jnp.dot(h_ref[...], wd_bf[...], preferred_element_type=F32)

    @pl.when(b >= nu_ref[0])
    def _():
        y_ref[...] = jnp.zeros(y_ref.shape, F32)


def _expert_ffn(blk_expert, n_used, buf, w_gate, w_up, w_down):
    rows, d = buf.shape
    n_blk = rows // EXPERT_BLOCK
    d_e = w_gate.shape[2]
    w_in_spec = pl.BlockSpec((1, d, d_e), lambda b, be, nu: (be[b], 0, 0))
    hidden = pl.pallas_call(
        _ffn_up_kernel,
        grid_spec=pltpu.PrefetchScalarGridSpec(
            num_scalar_prefetch=2,
            grid=(n_blk,),
            in_specs=[pl.BlockSpec((EXPERT_BLOCK, d), lambda b, be, nu: (b, 0)), w_in_spec, w_in_spec],
            out_specs=pl.BlockSpec((EXPERT_BLOCK, d_e), lambda b, be, nu: (b, 0)),
            scratch_shapes=[pltpu.VMEM((d, d_e), BF16), pltpu.VMEM((d, d_e), BF16)],
        ),
        out_shape=jax.ShapeDtypeStruct((rows, d_e), BF16),
        compiler_params=_params(("arbitrary",)),
        name="ffn_up",
    )(blk_expert, n_used, buf, w_gate, w_up)
    return pl.pallas_call(
        _ffn_down_kernel,
        grid_spec=pltpu.PrefetchScalarGridSpec(
            num_scalar_prefetch=2,
            grid=(n_blk,),
            in_specs=[pl.BlockSpec((EXPERT_BLOCK, d_e), lambda b, be, nu: (b, 0)),
                      pl.BlockSpec((1, d_e, d), lambda b, be, nu: (be[b], 0, 0))],
            out_specs=pl.BlockSpec((EXPERT_BLOCK, d), lambda b, be, nu: (b, 0)),
            scratch_shapes=[pltpu.VMEM((d_e, d), BF16)],
        ),
        out_shape=jax.ShapeDtypeStruct((rows, d), F32),
        compiler_params=_params(("arbitrary",)),
        name="ffn_down",
    )(blk_expert, n_used, hidden, w_down)


def _combine_kernel(dest_ref, h1_ref, route_ref, y_ref, g_ref, o_ref, y_scr, sem, *, tm):
    base = PREFIX + pl.program_id(0) * tm

    def issue(t, c):
        for k in range(TOP_K):
            _row_copy(y_ref, dest_ref[TOP_K * (base + t) + k], y_scr.at[k], t, sem).start()
        return c

    lax.fori_loop(0, tm, issue, 0)

    def drain(t, c):
        for k in range(TOP_K):
            _row_copy(y_ref, dest_ref[TOP_K * (base + t) + k], y_scr.at[k], t, sem).wait()
        return c

    lax.fori_loop(0, tm, drain, 0)
    route = route_ref[...]
    h2 = h1_ref[...] + (route[:, 2:3] * y_scr[0] + route[:, 3:4] * y_scr[1])
    ms = jnp.mean(h2 * h2, axis=-1, keepdims=True)
    o_ref[...] = (h2 * lax.rsqrt(ms + NORM_EPS)) * g_ref[...]


def _combine(dest, h1, route, y, g, *, tm):
    lp, d = h1.shape
    assert PREFIX % tm == 0
    skip = PREFIX // tm
    n_tok = lp - PREFIX
    return pl.pallas_call(
        functools.partial(_combine_kernel, tm=tm),
        grid_spec=pltpu.PrefetchScalarGridSpec(
            num_scalar_prefetch=1,
            grid=(n_tok // tm,),
            in_specs=[pl.BlockSpec((tm, d), lambda i, dst: (i + skip, 0)),
                      pl.BlockSpec((tm, LANES), lambda i, dst: (i + skip, 0)),
                      pl.BlockSpec(memory_space=pl.ANY),
                      pl.BlockSpec((1, d), lambda i, dst: (0, 0))],
            out_specs=pl.BlockSpec((tm, d), lambda i, dst: (i, 0)),
            scratch_shapes=[pltpu.VMEM((TOP_K, tm, d), F32), pltpu.SemaphoreType.DMA(())],
        ),
        out_shape=jax.ShapeDtypeStruct((n_tok, d), F32),
        compiler_params=_params(("arbitrary",)),
        name="combine",
    )(dest, h1, route, y, g)


def _row_tile(lp, candidates):
    for t in candidates:
        if lp % t == 0:
            return t
    raise ValueError(f"no row tile for {lp}")


def kernel(x, meta_tokens, norm_mix_g, w_in, ret_decay_logit_fwd, ret_decay_logit_bwd, ret_norm_g,
           attn_q_norm_g, attn_k_norm_g, attn_out_norm_g, w_out, norm_ffn_g, router_group_w,
           router_group_b, router_expert_w, router_expert_b, expert_w_gate, expert_w_up, expert_w_down,
           norm_final_g):
    batch, n_tok, d = x.shape
    assert batch == 1 and norm_mix_g.shape[0] == 1
    lp = n_tok + PREFIX
    h = jnp.concatenate([jnp.zeros((PAD, d), x.dtype), meta_tokens.astype(x.dtype), x[0]], axis=0)

    cr, sr = _rope_tables(n_tok, RET_QK_DIM)
    ca, sa = _rope_tables(n_tok, ATTN_HEAD_DIM)
    qk, rest = _in_proj(h, norm_mix_g, w_in[0].astype(BF16), cr, sr, ca, sa,
                        attn_q_norm_g, attn_k_norm_g, tm=_row_tile(lp, (640, 128)))

    log_gamma = jnp.stack([jax.nn.log_sigmoid(ret_decay_logit_fwd[0].astype(F32)),
                           jax.nn.log_sigmoid(ret_decay_logit_bwd[0].astype(F32))])
    ret = _retention(log_gamma, qk, rest, ret_norm_g)
    att = _attention(rest, tq=128, tk=_row_tile(lp, (640, 128)))

    n_route = N_GROUPS + N_EXPERTS
    wr = jnp.pad(jnp.concatenate([router_group_w[0], router_expert_w[0]], axis=1).astype(F32),
                 ((0, 0), (0, LANES - n_route)))
    br = jnp.pad(jnp.concatenate([router_group_b[0], router_expert_b[0]]).astype(F32),
                 (0, LANES - n_route))[None]
    h1, f, route, cnt = _mix_router(ret, att, attn_out_norm_g, w_out[0].astype(BF16), h, norm_ffn_g,
                                    wr, br, tm=_row_tile(lp, (320, 128)))

    expert = route[:, 0:TOP_K].astype(jnp.int32)
    rank = route[:, 4:4 + TOP_K].astype(jnp.int32)
    counts = cnt[0, N_GROUPS:n_route].astype(jnp.int32)
    padded = (counts + EXPERT_BLOCK - 1) // EXPERT_BLOCK * EXPERT_BLOCK
    pad_end = jnp.cumsum(padded)
    pad_start = pad_end - padded
    dest = (pad_start[expert] + rank).reshape(-1).astype(jnp.int32)
    n_blk = -(-(lp * TOP_K) // EXPERT_BLOCK) + N_EXPERTS
    blk_expert = jnp.minimum(
        jnp.searchsorted(pad_end, jnp.arange(n_blk, dtype=jnp.int32) * EXPERT_BLOCK, side='right'),
        N_EXPERTS - 1).astype(jnp.int32)
    n_used = (pad_end[-1:] // EXPERT_BLOCK).astype(jnp.int32)

    buf = _dispatch(dest, f, jnp.zeros((n_blk * EXPERT_BLOCK, d), F32), tm=_row_tile(lp, (320, 128)))
    y = _expert_ffn(blk_expert, n_used, buf, expert_w_gate[0], expert_w_up[0], expert_w_down[0])
    out = _combine(dest, h1, route, y, norm_final_g[None], tm=PREFIX)
    return out[None]
```

```python
import functools

import jax
import jax.numpy as jnp
from jax import lax
from jax.experimental import pallas as pl
from jax.experimental.pallas import tpu as pltpu

F32 = jnp.float32
BF16 = jnp.bfloat16

N_META = 16
CHUNK = 128
PREFIX = CHUNK
PAD = PREFIX - N_META
GRID_W = 64
ROPE_THETA = 10000.0
NORM_EPS = 1e-6
GROUPNORM_EPS = 1e-5

RET_HEADS = 8
RET_QK_DIM = 64
RET_V_DIM = 128
ATTN_HEADS = 8
ATTN_KV_HEADS = 2
ATTN_HEAD_DIM = 128
ATTN_GROUP = ATTN_HEADS // ATTN_KV_HEADS
ATTN_STRIP = 256
ATTN_ONES_ROWS = 16
LOG2_E = 1.4426950408889634

N_GROUPS = 8
EXPERTS_PER_GROUP = 8
N_EXPERTS = N_GROUPS * EXPERTS_PER_GROUP
TOP_K = 2
EXPERT_BLOCK = 128

LANES = 128
PROJ_TILE = 512
VMEM_LIMIT = 56 * 1024 * 1024
NEG_BIG = -1e30


def _params(sem, vmem=VMEM_LIMIT):
    return pltpu.CompilerParams(dimension_semantics=sem, vmem_limit_bytes=vmem)


def _swap_halves(x, hw):
    w = x.shape[-1]
    lane = lax.broadcasted_iota(jnp.int32, x.shape, 1)
    first = (lane & hw) == 0
    return jnp.where(first, pltpu.roll(x, w - hw, 1), pltpu.roll(x, hw, 1))


def _rope_tables(n_tok, head_dim):
    half = head_dim // 2
    freqs = ROPE_THETA ** (-jnp.arange(0, half, 2, dtype=F32) / half)
    rows = n_tok // GRID_W
    row_ids = jnp.repeat(jnp.arange(rows), GRID_W).astype(F32)
    col_ids = jnp.tile(jnp.arange(GRID_W), rows).astype(F32)
    zeros_p = jnp.zeros((PREFIX,), F32)
    pos_row = jnp.concatenate([zeros_p, row_ids])
    pos_col = jnp.concatenate([zeros_p, col_ids])
    ang_r = pos_row[:, None] * freqs[None, :]
    ang_c = pos_col[:, None] * freqs[None, :]
    cos = jnp.concatenate([jnp.cos(ang_r), jnp.cos(ang_r), jnp.cos(ang_c), jnp.cos(ang_c)], axis=-1)
    sin = jnp.concatenate([-jnp.sin(ang_r), jnp.sin(ang_r), -jnp.sin(ang_c), jnp.sin(ang_c)], axis=-1)
    reps = LANES // head_dim
    return jnp.tile(cos, (1, reps)), jnp.tile(sin, (1, reps))


def _in_proj_kernel(x_ref, g_ref, w_ref, cr_ref, sr_ref, ca_ref, sa_ref, qg_ref, kg_ref,
                    oqk_ref, orest_ref, a_scr, *, tm):
    i = pl.program_id(0)
    j = pl.program_id(1)

    @pl.when(j == 0)
    def _():
        x = x_ref[...]
        ms = jnp.mean(x * x, axis=-1, keepdims=True)
        a_scr[...] = ((x * lax.rsqrt(ms + NORM_EPS)) * g_ref[...]).astype(BF16)

    acc = jnp.dot(a_scr[...], w_ref[...], preferred_element_type=F32)
    n_lane_groups = PROJ_TILE // LANES

    def tiled(ref):
        return jnp.concatenate([ref[...]] * n_lane_groups, axis=1)

    @pl.when(j < 2)
    def _():
        y = acc * tiled(cr_ref) + _swap_halves(acc, RET_QK_DIM // 4) * tiled(sr_ref)
        row = i * tm + lax.broadcasted_iota(jnp.int32, (tm, 1), 0)
        k_scale = jnp.where(row >= PAD, RET_QK_DIM ** -0.5, 0.0).astype(F32)
        y = y * jnp.where(j == 1, k_scale, jnp.ones_like(k_scale))
        lo = lax.broadcasted_iota(jnp.int32, (tm, LANES), 1) < RET_QK_DIM
        pieces = []
        for c in range(n_lane_groups):
            xg = y[:, c * LANES:(c + 1) * LANES]
            xr = pltpu.roll(xg, RET_QK_DIM, 1)
            pieces.append(jnp.where(lo, xg, xr))
            pieces.append(jnp.where(lo, xr, xg))
        oqk_ref[...] = jnp.concatenate(pieces, axis=1).astype(BF16)

    @pl.when((j >= 2) & (j < 6))
    def _():
        orest_ref[...] = acc.astype(BF16)

    def qk_norm_rope(xg, gain_ref):
        ms = jnp.mean(xg * xg, axis=-1, keepdims=True)
        xn = (xg * lax.rsqrt(ms + NORM_EPS)) * gain_ref[...]
        return xn * ca_ref[...] + _swap_halves(xn, ATTN_HEAD_DIM // 4) * sa_ref[...]

    @pl.when((j == 6) | (j == 7))
    def _():
        pieces = [qk_norm_rope(acc[:, c * LANES:(c + 1) * LANES], qg_ref) * (LOG2_E * ATTN_HEAD_DIM ** -0.5)
                  for c in range(n_lane_groups)]
        orest_ref[...] = jnp.concatenate(pieces, axis=1).astype(BF16)

    @pl.when(j == 8)
    def _():
        pieces = [qk_norm_rope(acc[:, c * LANES:(c + 1) * LANES], kg_ref) for c in range(ATTN_KV_HEADS)]
        pieces.append(acc[:, ATTN_KV_HEADS * LANES:])
        orest_ref[...] = jnp.concatenate(pieces, axis=1).astype(BF16)


def _in_proj(h, g, w_bf, cr, sr, ca, sa, qg, kg, *, tm):
    lp, d = h.shape
    n_out = w_bf.shape[1]
    nj = n_out // PROJ_TILE
    assert lp % tm == 0 and n_out % PROJ_TILE == 0 and nj == 9
    qk_w = 2 * 2 * RET_HEADS * RET_QK_DIM
    rest_w = n_out - 2 * RET_HEADS * RET_QK_DIM
    row_tab = pl.BlockSpec((tm, LANES), lambda i, j: (i, 0))
    vec = lambda n: pl.BlockSpec((1, n), lambda i, j: (0, 0))
    return pl.pallas_call(
        functools.partial(_in_proj_kernel, tm=tm),
        grid=(lp // tm, nj),
        in_specs=[
            pl.BlockSpec((tm, d), lambda i, j: (i, 0)),
            vec(d),
            pl.BlockSpec((d, PROJ_TILE), lambda i, j: (0, j)),
            row_tab, row_tab, row_tab, row_tab,
            vec(LANES), vec(LANES),
        ],
        out_specs=[
            pl.BlockSpec((tm, 2 * PROJ_TILE), lambda i, j: (i, jnp.minimum(j, 1))),
            pl.BlockSpec((tm, PROJ_TILE), lambda i, j: (i, jnp.maximum(j - 2, 0))),
        ],
        out_shape=[jax.ShapeDtypeStruct((lp, qk_w), BF16), jax.ShapeDtypeStruct((lp, rest_w), BF16)],
        scratch_shapes=[pltpu.VMEM((tm, d), BF16)],
        compiler_params=_params(("arbitrary", "arbitrary")),
        name="in_proj",
    )(h, g, w_bf, cr, sr, ca, sa, qg, kg)


def _retention_kernel(lg_ref, q_ref, k_ref, v_ref, g_ref, gn_ref, o_ref, st_ref, dk_ref, dq_ref, dm_ref,
                      *, n_chunks):
    h = pl.program_id(0)
    lgf = lg_ref[0, h]
    lgb = lg_ref[1, h]
    half = CHUNK // 2
    row = lax.broadcasted_iota(jnp.int32, (CHUNK, CHUNK), 0).astype(F32)
    lane = lax.broadcasted_iota(jnp.int32, (CHUNK, CHUNK), 1).astype(F32)
    lane_lo = lane < half
    dk_ref[...] = jnp.exp(jnp.where(lane_lo, lgf * (CHUNK - 1.0 - row), lgb * row))
    dq_ref[...] = jnp.exp(jnp.where(lane_lo, lgf * (row + 1.0), lgb * (CHUNK - row)))
    diff = row - lane
    dm_ref[...] = 0.5 * jnp.where(diff >= 0, jnp.exp(lgf * jnp.maximum(diff, 0.0)),
                                  jnp.exp(lgb * jnp.maximum(-diff, 0.0)))
    cf = jnp.exp(jnp.full((half, CHUNK), lgf * CHUNK, F32))
    cb = jnp.exp(jnp.full((half, CHUNK), lgb * CHUNK, F32))

    def rows(n):
        return pl.ds(pl.multiple_of(n * CHUNK, CHUNK), CHUNK)

    def updates(n, c):
        kd = k_ref[rows(n), :].astype(F32) * dk_ref[...]
        st_ref[n] = jnp.dot(kd.T.astype(BF16), v_ref[rows(n), :], preferred_element_type=F32)
        return c

    lax.fori_loop(0, n_chunks, updates, 0)

    def scan_fwd(n, s):
        u = st_ref[n, 0:half, :]
        st_ref[n, 0:half, :] = s
        return cf * s + u

    lax.fori_loop(0, n_chunks, scan_fwd, jnp.zeros((half, CHUNK), F32))

    def scan_bwd(t, s):
        n = n_chunks - 1 - t
        u = st_ref[n, half:CHUNK, :]
        st_ref[n, half:CHUNK, :] = s
        return cb * s + u

    lax.fori_loop(0, n_chunks, scan_bwd, jnp.zeros((half, CHUNK), F32))

    def outputs(n, c):
        q = q_ref[rows(n), :]
        k = k_ref[rows(n), :]
        v = v_ref[rows(n), :]
        s2 = lax.dot_general(q, k, (((1,), (1,)), ((), ())), preferred_element_type=F32)
        intra = jnp.dot((s2 * dm_ref[...]).astype(BF16), v, preferred_element_type=F32)
        qd = (q.astype(F32) * dq_ref[...]).astype(BF16)
        cross = jnp.dot(qd, st_ref[n].astype(BF16), preferred_element_type=F32)
        o = intra + cross
        mu = jnp.mean(o, axis=-1, keepdims=True)
        dlt = o - mu
        var = jnp.mean(dlt * dlt, axis=-1, keepdims=True)
        on = dlt * lax.rsqrt(var + GROUPNORM_EPS)
        gate = g_ref[rows(n), :].astype(F32)
        o_ref[rows(n), :] = ((gate * jax.nn.sigmoid(gate)) * (on * gn_ref[...])).astype(BF16)
        return c

    lax.fori_loop(0, n_chunks, outputs, 0)


def _retention(log_gamma, qk, rest, gn_g):
    lp = qk.shape[0]
    n_chunks = lp // CHUNK
    col = lambda off: pl.BlockSpec((lp, LANES), lambda h, lg: (0, h + off))
    return pl.pallas_call(
        functools.partial(_retention_kernel, n_chunks=n_chunks),
        grid_spec=pltpu.PrefetchScalarGridSpec(
            num_scalar_prefetch=1,
            grid=(RET_HEADS,),
            in_specs=[col(0), col(RET_HEADS), col(0), col(RET_HEADS),
                      pl.BlockSpec((1, LANES), lambda h, lg: (0, h))],
            out_specs=pl.BlockSpec((lp, LANES), lambda h, lg: (0, h)),
            scratch_shapes=[pltpu.VMEM((n_chunks, CHUNK, CHUNK), F32),
                            pltpu.VMEM((CHUNK, CHUNK), F32),
                            pltpu.VMEM((CHUNK, CHUNK), F32),
                            pltpu.VMEM((CHUNK, CHUNK), F32)],
        ),
        out_shape=jax.ShapeDtypeStruct((lp, RET_HEADS * RET_V_DIM), BF16),
        compiler_params=_params(("arbitrary",)),
        name="retention",
    )(log_gamma, qk, qk, rest, rest, gn_g)


def _attention_kernel(q_ref, k_ref, vt_ref, km_ref, vtm_ref, o_ref, s_scr, mx_scr, m_scr, acc_scr,
                      *, tq, tk, n_kc):
    q_all = jnp.concatenate([q_ref[:, g * LANES:(g + 1) * LANES] for g in range(ATTN_GROUP)], axis=0)
    n_strips = (ATTN_GROUP * tq) // ATTN_STRIP
    q_strips = [q_all[s * ATTN_STRIP:(s + 1) * ATTN_STRIP, :] for s in range(n_strips)]

    def scores(k, s):
        return lax.dot_general(k, q_strips[s], (((1,), (1,)), ((), ())), preferred_element_type=F32)

    def key_rows(c):
        return pl.ds(pl.multiple_of(PREFIX + c * tk, LANES), tk)

    def score_stage(c, slot):
        k = k_ref[key_rows(c), :]
        for s in range(n_strips):
            st = scores(k, s)
            s_scr[slot, s] = st
            mx_scr[slot, s] = jnp.max(st, axis=0, keepdims=True)

    def value_stage(c, slot):
        vt = vt_ref[0, :, key_rows(c)]
        for s in range(n_strips):
            m_old = m_scr[s]
            m_new = jnp.maximum(m_old, mx_scr[slot, s])
            p = jnp.exp2(s_scr[slot, s] - m_new).astype(BF16)
            acc_scr[s] = jnp.exp2(m_old - m_new) * acc_scr[s] + jnp.dot(vt, p, preferred_element_type=F32)
            m_scr[s] = m_new

    for s in range(n_strips):
        st = scores(km_ref[0], s)
        m0 = jnp.max(st, axis=0, keepdims=True)
        m_scr[s] = m0
        acc_scr[s] = jnp.dot(vtm_ref[0], jnp.exp2(st - m0).astype(BF16), preferred_element_type=F32)

    score_stage(0, 0)

    def body(i, carry):
        c = 2 * i
        score_stage(c + 1, 1)
        value_stage(c, 0)
        score_stage(c + 2, 0)
        value_stage(c + 1, 1)
        return carry

    lax.fori_loop(0, n_kc // 2 - 1, body, 0)
    score_stage(n_kc - 1, 1)
    value_stage(n_kc - 2, 0)
    value_stage(n_kc - 1, 1)
    outs = []
    for s in range(n_strips):
        acc = acc_scr[s]
        outs.append((acc[0:ATTN_HEAD_DIM, :] / acc[ATTN_HEAD_DIM:ATTN_HEAD_DIM + 1, :]).T)
    o = jnp.concatenate(outs, axis=0)
    for g in range(ATTN_GROUP):
        o_ref[:, g * LANES:(g + 1) * LANES] = o[g * tq:(g + 1) * tq, :]


def _attention(rest, *, tq, tk):
    lp = rest.shape[0]
    n_tok = lp - PREFIX
    assert lp % tq == 0 and n_tok % (2 * tk) == 0 and (ATTN_GROUP * tq) % ATTN_STRIP == 0
    gw = ATTN_GROUP * ATTN_HEAD_DIM
    q_off = (2 * RET_HEADS * RET_V_DIM) // gw
    k_col = 2 * RET_HEADS * RET_V_DIM + ATTN_HEADS * ATTN_HEAD_DIM
    k_off = k_col // LANES
    kv_w = ATTN_KV_HEADS * ATTN_HEAD_DIM
    vt = rest[:, k_col + kv_w:k_col + 2 * kv_w].T.reshape(ATTN_KV_HEADS, ATTN_HEAD_DIM, lp)
    vt = jnp.concatenate([vt, jnp.ones((ATTN_KV_HEADS, ATTN_ONES_ROWS, lp), BF16)], axis=1)
    vt_meta = vt[:, :, PAD:PREFIX]
    k_meta = rest[PAD:PREFIX, k_col:k_col + kv_w].reshape(N_META, ATTN_KV_HEADS, ATTN_HEAD_DIM)
    k_meta = jnp.swapaxes(k_meta, 0, 1)
    n_strips = (ATTN_GROUP * tq) // ATTN_STRIP
    vt_rows = ATTN_HEAD_DIM + ATTN_ONES_ROWS
    return pl.pallas_call(
        functools.partial(_attention_kernel, tq=tq, tk=tk, n_kc=n_tok // tk),
        grid=(ATTN_KV_HEADS, lp // tq),
        in_specs=[pl.BlockSpec((tq, gw), lambda kv, i: (i, q_off + kv)),
                  pl.BlockSpec((lp, LANES), lambda kv, i: (0, k_off + kv)),
                  pl.BlockSpec((1, vt_rows, lp), lambda kv, i: (kv, 0, 0)),
                  pl.BlockSpec((1, N_META, ATTN_HEAD_DIM), lambda kv, i: (kv, 0, 0)),
                  pl.BlockSpec((1, vt_rows, N_META), lambda kv, i: (kv, 0, 0))],
        out_specs=pl.BlockSpec((tq, gw), lambda kv, i: (i, kv)),
        out_shape=jax.ShapeDtypeStruct((lp, ATTN_HEADS * ATTN_HEAD_DIM), F32),
        scratch_shapes=[pltpu.VMEM((2, n_strips, tk, ATTN_STRIP), F32),
                        pltpu.VMEM((2, n_strips, 1, ATTN_STRIP), F32),
                        pltpu.VMEM((n_strips, 1, ATTN_STRIP), F32),
                        pltpu.VMEM((n_strips, vt_rows, ATTN_STRIP), F32)],
        compiler_params=_params(("arbitrary", "arbitrary")),
        name="attention",
    )(rest, rest, vt, k_meta, vt_meta)


def _mix_router_kernel(ret_ref, att_ref, ang_ref, wo_ref, h_ref, fg_ref, wr_ref, br_ref,
                       h1_ref, f_ref, route_ref, cnt_ref, carry_scr, *, tm):
    i = pl.program_id(0)

    @pl.when(i == 0)
    def _():
        carry_scr[...] = jnp.zeros(carry_scr.shape, F32)

    att = att_ref[...]
    ms = jnp.mean(att * att, axis=-1, keepdims=True)
    att_n = ((att * lax.rsqrt(ms + NORM_EPS)) * ang_ref[...]).astype(BF16)
    mix_in = jnp.concatenate([ret_ref[...], att_n], axis=1)
    h1 = h_ref[...] + jnp.dot(mix_in, wo_ref[...], preferred_element_type=F32)
    h1_ref[...] = h1
    ms1 = jnp.mean(h1 * h1, axis=-1, keepdims=True)
    f = (h1 * lax.rsqrt(ms1 + NORM_EPS)) * fg_ref[...]
    f_ref[...] = f

    logits = jnp.dot(f, wr_ref[...], preferred_element_type=F32,
                     precision=lax.Precision.HIGHEST) + br_ref[...]
    lane = lax.broadcasted_iota(jnp.int32, (tm, LANES), 1).astype(F32)
    far = float(LANES)
    g_logit = jnp.where(lane < N_GROUPS, logits, NEG_BIG)
    g_max = jnp.max(g_logit, axis=-1, keepdims=True)
    g_w = 1.0 / jnp.sum(jnp.exp(g_logit - g_max), axis=-1, keepdims=True)
    g_idx = jnp.min(jnp.where(g_logit == g_max, lane, far), axis=-1, keepdims=True)
    first = N_GROUPS + g_idx * EXPERTS_PER_GROUP
    e_logit = jnp.where((lane >= first) & (lane < first + EXPERTS_PER_GROUP), logits, NEG_BIG)
    v1 = jnp.max(e_logit, axis=-1, keepdims=True)
    i1 = jnp.min(jnp.where(e_logit == v1, lane, far), axis=-1, keepdims=True)
    e_rest = jnp.where(lane == i1, NEG_BIG, e_logit)
    v2 = jnp.max(e_rest, axis=-1, keepdims=True)
    i2 = jnp.min(jnp.where(e_rest == v2, lane, far), axis=-1, keepdims=True)
    e2 = jnp.exp(v2 - v1)
    w1 = g_w / (1.0 + e2)
    w2 = g_w * e2 / (1.0 + e2)

    oh1 = (lane == i1).astype(F32)
    oh2 = (lane == i2).astype(F32)
    oh = oh1 + oh2
    r_i = lax.broadcasted_iota(jnp.int32, (tm, tm), 0)
    c_i = lax.broadcasted_iota(jnp.int32, (tm, tm), 1)
    lower = (c_i < r_i).astype(BF16)
    before = jnp.dot(lower, oh.astype(BF16), preferred_element_type=F32) + carry_scr[...]
    rank1 = jnp.sum(before * oh1, axis=-1, keepdims=True)
    rank2 = jnp.sum(before * oh2, axis=-1, keepdims=True)
    carry_scr[...] = carry_scr[...] + jnp.sum(oh, axis=0, keepdims=True)
    cnt_ref[...] = carry_scr[...]

    route = jnp.where(lane == 0, i1 - N_GROUPS, 0.0)
    route = jnp.where(lane == 1, i2 - N_GROUPS, route)
    route = jnp.where(lane == 2, w1, route)
    route = jnp.where(lane == 3, w2, route)
    route = jnp.where(lane == 4, rank1, route)
    route = jnp.where(lane == 5, rank2, route)
    route_ref[...] = route


def _mix_router(ret, att, ang, wo_bf, h, fg, wr, br, *, tm):
    lp, d = h.shape
    assert lp % tm == 0
    rowblk = lambda n: pl.BlockSpec((tm, n), lambda i: (i, 0))
    vec = lambda n: pl.BlockSpec((1, n), lambda i: (0, 0))
    full = lambda a: pl.BlockSpec(a.shape, lambda i: (0, 0))
    return pl.pallas_call(
        functools.partial(_mix_router_kernel, tm=tm),
        grid=(lp // tm,),
        in_specs=[rowblk(ret.shape[1]), rowblk(att.shape[1]), vec(att.shape[1]), full(wo_bf),
                  rowblk(d), vec(d), full(wr), vec(LANES)],
        out_specs=[rowblk(d), rowblk(d), rowblk(LANES), vec(LANES)],
        out_shape=[jax.ShapeDtypeStruct((lp, d), F32), jax.ShapeDtypeStruct((lp, d), F32),
                   jax.ShapeDtypeStruct((lp, LANES), F32), jax.ShapeDtypeStruct((1, LANES), F32)],
        scratch_shapes=[pltpu.VMEM((1, LANES), F32)],
        compiler_params=_params(("arbitrary",)),
        name="mix_router",
    )(ret, att, ang, wo_bf, h, fg, wr, br)


def _row_copy(src_ref, src_row, dst_ref, dst_row, sem):
    return pltpu.make_async_copy(src_ref.at[pl.ds(src_row, 1), :], dst_ref.at[pl.ds(dst_row, 1), :], sem)


def _dispatch_kernel(dest_ref, f_ref, buf_in_ref, buf_ref, sem, *, tm):
    del buf_in_ref
    base = pl.program_id(0) * tm

    def issue(t, c):
        for k in range(TOP_K):
            _row_copy(f_ref, t, buf_ref, dest_ref[TOP_K * (base + t) + k], sem).start()
        return c

    lax.fori_loop(0, tm, issue, 0)

    def drain(t, c):
        for k in range(TOP_K):
            _row_copy(f_ref, t, buf_ref, dest_ref[TOP_K * (base + t) + k], sem).wait()
        return c

    lax.fori_loop(0, tm, drain, 0)


def _dispatch(dest, f, buf_zero, *, tm):
    lp, d = f.shape
    return pl.pallas_call(
        functools.partial(_dispatch_kernel, tm=tm),
        grid_spec=pltpu.PrefetchScalarGridSpec(
            num_scalar_prefetch=1,
            grid=(lp // tm,),
            in_specs=[pl.BlockSpec((tm, d), lambda i, dst: (i, 0)),
                      pl.BlockSpec(memory_space=pl.ANY)],
            out_specs=pl.BlockSpec(memory_space=pl.ANY),
            scratch_shapes=[pltpu.SemaphoreType.DMA(())],
        ),
        out_shape=jax.ShapeDtypeStruct(buf_zero.shape, buf_zero.dtype),
        input_output_aliases={2: 0},
        compiler_params=_params(("arbitrary",)),
        name="dispatch",
    )(dest, f, buf_zero)


def _expert_changed(be_ref, b):
    return (b == 0) | (be_ref[b] != be_ref[jnp.maximum(b - 1, 0)])


def _ffn_up_kernel(be_ref, nu_ref, x_ref, wg_ref, wu_ref, h_ref, wg_bf, wu_bf):
    b = pl.program_id(0)

    @pl.when(_expert_changed(be_ref, b))
    def _():
        wg_bf[...] = wg_ref[0].astype(BF16)
        wu_bf[...] = wu_ref[0].astype(BF16)

    @pl.when(b < nu_ref[0])
    def _():
        x = x_ref[...].astype(BF16)
        gate = jnp.dot(x, wg_bf[...], preferred_element_type=F32)
        up = jnp.dot(x, wu_bf[...], preferred_element_type=F32)
        h_ref[...] = ((gate * jax.nn.sigmoid(gate)) * up).astype(BF16)

    @pl.when(b >= nu_ref[0])
    def _():
        h_ref[...] = jnp.zeros(h_ref.shape, BF16)


def _ffn_down_kernel(be_ref, nu_ref, h_ref, wd_ref, y_ref, wd_bf):
    b = pl.program_id(0)

    @pl.when(_expert_changed(be_ref, b))
    def _():
        wd_bf[...] = wd_ref[0].astype(BF16)

    @pl.when(b < nu_ref[0])
    def _():
        y_ref[...] = jnp.dot(h_ref[...], wd_bf[...], preferred_element_type=F32)

    @pl.when(b >= nu_ref[0])
    def _():
        y_ref[...] = jnp.zeros(y_ref.shape, F32)


def _expert_ffn(blk_expert, n_used, buf, w_gate, w_up, w_down):
    rows, d = buf.shape
    n_blk = rows // EXPERT_BLOCK
    d_e = w_gate.shape[2]
    w_in_spec = pl.BlockSpec((1, d, d_e), lambda b, be, nu: (be[b], 0, 0))
    hidden = pl.pallas_call(
        _ffn_up_kernel,
        grid_spec=pltpu.PrefetchScalarGridSpec(
            num_scalar_prefetch=2,
            grid=(n_blk,),
            in_specs=[pl.BlockSpec((EXPERT_BLOCK, d), lambda b, be, nu: (b, 0)), w_in_spec, w_in_spec],
            out_specs=pl.BlockSpec((EXPERT_BLOCK, d_e), lambda b, be, nu: (b, 0)),
            scratch_shapes=[pltpu.VMEM((d, d_e), BF16), pltpu.VMEM((d, d_e), BF16)],
        ),
        out_shape=jax.ShapeDtypeStruct((rows, d_e), BF16),
        compiler_params=_params(("arbitrary",)),
        name="ffn_up",
    )(blk_expert, n_used, buf, w_gate, w_up)
    return pl.pallas_call(
        _ffn_down_kernel,
        grid_spec=pltpu.PrefetchScalarGridSpec(
            num_scalar_prefetch=2,
            grid=(n_blk,),
            in_specs=[pl.BlockSpec((EXPERT_BLOCK, d_e), lambda b, be, nu: (b, 0)),
                      pl.BlockSpec((1, d_e, d), lambda b, be, nu: (be[b], 0, 0))],
            out_specs=pl.BlockSpec((EXPERT_BLOCK, d), lambda b, be, nu: (b, 0)),
            scratch_shapes=[pltpu.VMEM((d_e, d), BF16)],
        ),
        out_shape=jax.ShapeDtypeStruct((rows, d), F32),
        compiler_params=_params(("arbitrary",)),
        name="ffn_down",
    )(blk_expert, n_used, hidden, w_down)


def _combine_kernel(dest_ref, h1_ref, route_ref, y_ref, g_ref, o_ref, y_scr, sem, *, tm):
    base = PREFIX + pl.program_id(0) * tm

    def issue(t, c):
        for k in range(TOP_K):
            _row_copy(y_ref, dest_ref[TOP_K * (base + t) + k], y_scr.at[k], t, sem).start()
        return c

    lax.fori_loop(0, tm, issue, 0)

    def drain(t, c):
        for k in range(TOP_K):
            _row_copy(y_ref, dest_ref[TOP_K * (base + t) + k], y_scr.at[k], t, sem).wait()
        return c

    lax.fori_loop(0, tm, drain, 0)
    route = route_ref[...]
    h2 = h1_ref[...] + (route[:, 2:3] * y_scr[0] + route[:, 3:4] * y_scr[1])
    ms = jnp.mean(h2 * h2, axis=-1, keepdims=True)
    o_ref[...] = (h2 * lax.rsqrt(ms + NORM_EPS)) * g_ref[...]


def _combine(dest, h1, route, y, g, *, tm):
    lp, d = h1.shape
    assert PREFIX % tm == 0
    skip = PREFIX // tm
    n_tok = lp - PREFIX
    return pl.pallas_call(
        functools.partial(_combine_kernel, tm=tm),
        grid_spec=pltpu.PrefetchScalarGridSpec(
            num_scalar_prefetch=1,
            grid=(n_tok // tm,),
            in_specs=[pl.BlockSpec((tm, d), lambda i, dst: (i + skip, 0)),
                      pl.BlockSpec((tm, LANES), lambda i, dst: (i + skip, 0)),
                      pl.BlockSpec(memory_space=pl.ANY),
                      pl.BlockSpec((1, d), lambda i, dst: (0, 0))],
            out_specs=pl.BlockSpec((tm, d), lambda i, dst: (i, 0)),
            scratch_shapes=[pltpu.VMEM((TOP_K, tm, d), F32), pltpu.SemaphoreType.DMA(())],
        ),
        out_shape=jax.ShapeDtypeStruct((n_tok, d), F32),
        compiler_params=_params(("arbitrary",)),
        name="combine",
    )(dest, h1, route, y, g)


def _row_tile(lp, candidates):
    for t in candidates:
        if lp % t == 0:
            return t
    raise ValueError(f"no row tile for {lp}")


def kernel(x, meta_tokens, norm_mix_g, w_in, ret_decay_logit_fwd, ret_decay_logit_bwd, ret_norm_g,
           attn_q_norm_g, attn_k_norm_g, attn_out_norm_g, w_out, norm_ffn_g, router_group_w,
           router_group_b, router_expert_w, router_expert_b, expert_w_gate, expert_w_up, expert_w_down,
           norm_final_g):
    batch, n_tok, d = x.shape
    assert batch == 1 and norm_mix_g.shape[0] == 1
    lp = n_tok + PREFIX
    h = jnp.concatenate([jnp.zeros((PAD, d), x.dtype), meta_tokens.astype(x.dtype), x[0]], axis=0)

    cr, sr = _rope_tables(n_tok, RET_QK_DIM)
    ca, sa = _rope_tables(n_tok, ATTN_HEAD_DIM)
    qk, rest = _in_proj(h, norm_mix_g, w_in[0].astype(BF16), cr, sr, ca, sa,
                        attn_q_norm_g, attn_k_norm_g, tm=_row_tile(lp, (640, 128)))

    log_gamma = jnp.stack([jax.nn.log_sigmoid(ret_decay_logit_fwd[0].astype(F32)),
                           jax.nn.log_sigmoid(ret_decay_logit_bwd[0].astype(F32))])
    ret = _retention(log_gamma, qk, rest, ret_norm_g)
    att = _attention(rest, tq=_row_tile(lp, (320, 128, 64)), tk=_row_tile(n_tok, (512, 256, 128)))

    n_route = N_GROUPS + N_EXPERTS
    wr = jnp.pad(jnp.concatenate([router_group_w[0], router_expert_w[0]], axis=1).astype(F32),
                 ((0, 0), (0, LANES - n_route)))
    br = jnp.pad(jnp.concatenate([router_group_b[0], router_expert_b[0]]).astype(F32),
                 (0, LANES - n_route))[None]
    h1, f, route, cnt = _mix_router(ret, att, attn_out_norm_g, w_out[0].astype(BF16), h, norm_ffn_g,
                                    wr, br, tm=_row_tile(lp, (320, 128)))

    expert = route[:, 0:TOP_K].astype(jnp.int32)
    rank = route[:, 4:4 + TOP_K].astype(jnp.int32)
    counts = cnt[0, N_GROUPS:n_route].astype(jnp.int32)
    padded = (counts + EXPERT_BLOCK - 1) // EXPERT_BLOCK * EXPERT_BLOCK
    pad_end = jnp.cumsum(padded)
    pad_start = pad_end - padded
    dest = (pad_start[expert] + rank).reshape(-1).astype(jnp.int32)
    n_blk = -(-(lp * TOP_K) // EXPERT_BLOCK) + N_EXPERTS
    blk_start = jnp.arange(n_blk, dtype=jnp.int32) * EXPERT_BLOCK
    blk_expert = jnp.minimum(jnp.sum((pad_end[None, :] <= blk_start[:, None]).astype(jnp.int32), axis=1),
                             N_EXPERTS - 1)
    n_used = (pad_end[-1:] // EXPERT_BLOCK).astype(jnp.int32)

    buf = _dispatch(dest, f, jnp.zeros((n_blk * EXPERT_BLOCK, d), F32), tm=_row_tile(lp, (320, 128)))
    y = _expert_ffn(blk_expert, n_used, buf, expert_w_gate[0], expert_w_up[0], expert_w_down[0])
    out = _combine(dest, h1, route, y, norm_final_g[None], tm=PREFIX)
    return out[None]
```

```python
import functools

import jax
import jax.numpy as jnp
from jax import lax
from jax.experimental import pallas as pl
from jax.experimental.pallas import tpu as pltpu

F32 = jnp.float32
BF16 = jnp.bfloat16

N_META = 16
CHUNK = 128
PREFIX = CHUNK
PAD = PREFIX - N_META
GRID_W = 64
ROPE_THETA = 10000.0
NORM_EPS = 1e-6
GROUPNORM_EPS = 1e-5

RET_HEADS = 8
RET_QK_DIM = 64
RET_V_DIM = 128
ATTN_HEADS = 8
ATTN_KV_HEADS = 2
ATTN_HEAD_DIM = 128
ATTN_GROUP = ATTN_HEADS // ATTN_KV_HEADS
ATTN_STRIP = 256
ATTN_ONES_ROWS = 16
LOG2_E = 1.4426950408889634

N_GROUPS = 8
EXPERTS_PER_GROUP = 8
N_EXPERTS = N_GROUPS * EXPERTS_PER_GROUP
TOP_K = 2
EXPERT_BLOCK = 128
MOE_ITEM_ROWS = 512
MOE_CHUNK = 512
DMA_GROUP = 8

LANES = 128
PROJ_TILE = 512
VMEM_LIMIT = 56 * 1024 * 1024
NEG_BIG = -1e30


def _params(sem, vmem=VMEM_LIMIT):
    return pltpu.CompilerParams(dimension_semantics=sem, vmem_limit_bytes=vmem)


def _swap_halves(x, hw):
    w = x.shape[-1]
    lane = lax.broadcasted_iota(jnp.int32, x.shape, 1)
    first = (lane & hw) == 0
    return jnp.where(first, pltpu.roll(x, w - hw, 1), pltpu.roll(x, hw, 1))


def _rope_tables(n_tok, head_dim):
    half = head_dim // 2
    freqs = ROPE_THETA ** (-jnp.arange(0, half, 2, dtype=F32) / half)
    rows = n_tok // GRID_W
    row_ids = jnp.repeat(jnp.arange(rows), GRID_W).astype(F32)
    col_ids = jnp.tile(jnp.arange(GRID_W), rows).astype(F32)
    zeros_p = jnp.zeros((PREFIX,), F32)
    pos_row = jnp.concatenate([zeros_p, row_ids])
    pos_col = jnp.concatenate([zeros_p, col_ids])
    ang_r = pos_row[:, None] * freqs[None, :]
    ang_c = pos_col[:, None] * freqs[None, :]
    cos = jnp.concatenate([jnp.cos(ang_r), jnp.cos(ang_r), jnp.cos(ang_c), jnp.cos(ang_c)], axis=-1)
    sin = jnp.concatenate([-jnp.sin(ang_r), jnp.sin(ang_r), -jnp.sin(ang_c), jnp.sin(ang_c)], axis=-1)
    reps = LANES // head_dim
    return jnp.tile(cos, (1, reps)), jnp.tile(sin, (1, reps))


def _in_proj_kernel(x_ref, g_ref, w_ref, cr_ref, sr_ref, ca_ref, sa_ref, qg_ref, kg_ref,
                    oqk_ref, orest_ref, a_scr, *, tm):
    i = pl.program_id(0)
    j = pl.program_id(1)

    @pl.when(j == 0)
    def _():
        x = x_ref[...]
        ms = jnp.mean(x * x, axis=-1, keepdims=True)
        a_scr[...] = ((x * lax.rsqrt(ms + NORM_EPS)) * g_ref[...]).astype(BF16)

    acc = jnp.dot(a_scr[...], w_ref[...], preferred_element_type=F32)
    n_lane_groups = PROJ_TILE // LANES

    def tiled(ref):
        return jnp.concatenate([ref[...]] * n_lane_groups, axis=1)

    @pl.when(j < 2)
    def _():
        y = acc * tiled(cr_ref) + _swap_halves(acc, RET_QK_DIM // 4) * tiled(sr_ref)
        row = i * tm + lax.broadcasted_iota(jnp.int32, (tm, 1), 0)
        k_scale = jnp.where(row >= PAD, RET_QK_DIM ** -0.5, 0.0).astype(F32)
        y = y * jnp.where(j == 1, k_scale, jnp.ones_like(k_scale))
        lo = lax.broadcasted_iota(jnp.int32, (tm, LANES), 1) < RET_QK_DIM
        pieces = []
        for c in range(n_lane_groups):
            xg = y[:, c * LANES:(c + 1) * LANES]
            xr = pltpu.roll(xg, RET_QK_DIM, 1)
            pieces.append(jnp.where(lo, xg, xr))
            pieces.append(jnp.where(lo, xr, xg))
        oqk_ref[...] = jnp.concatenate(pieces, axis=1).astype(BF16)

    @pl.when((j >= 2) & (j < 6))
    def _():
        orest_ref[...] = acc.astype(BF16)

    def qk_norm_rope(xg, gain_ref):
        ms = jnp.mean(xg * xg, axis=-1, keepdims=True)
        xn = (xg * lax.rsqrt(ms + NORM_EPS)) * gain_ref[...]
        return xn * ca_ref[...] + _swap_halves(xn, ATTN_HEAD_DIM // 4) * sa_ref[...]

    @pl.when((j == 6) | (j == 7))
    def _():
        pieces = [qk_norm_rope(acc[:, c * LANES:(c + 1) * LANES], qg_ref) * (LOG2_E * ATTN_HEAD_DIM ** -0.5)
                  for c in range(n_lane_groups)]
        orest_ref[...] = jnp.concatenate(pieces, axis=1).astype(BF16)

    @pl.when(j == 8)
    def _():
        pieces = [qk_norm_rope(acc[:, c * LANES:(c + 1) * LANES], kg_ref) for c in range(ATTN_KV_HEADS)]
        pieces.append(acc[:, ATTN_KV_HEADS * LANES:])
        orest_ref[...] = jnp.concatenate(pieces, axis=1).astype(BF16)


def _in_proj(h, g, w_bf, cr, sr, ca, sa, qg, kg, *, tm):
    lp, d = h.shape
    n_out = w_bf.shape[1]
    nj = n_out // PROJ_TILE
    assert lp % tm == 0 and n_out % PROJ_TILE == 0 and nj == 9
    qk_w = 2 * 2 * RET_HEADS * RET_QK_DIM
    rest_w = n_out - 2 * RET_HEADS * RET_QK_DIM
    row_tab = pl.BlockSpec((tm, LANES), lambda i, j: (i, 0))
    vec = lambda n: pl.BlockSpec((1, n), lambda i, j: (0, 0))
    return pl.pallas_call(
        functools.partial(_in_proj_kernel, tm=tm),
        grid=(lp // tm, nj),
        in_specs=[
            pl.BlockSpec((tm, d), lambda i, j: (i, 0)),
            vec(d),
            pl.BlockSpec((d, PROJ_TILE), lambda i, j: (0, j)),
            row_tab, row_tab, row_tab, row_tab,
            vec(LANES), vec(LANES),
        ],
        out_specs=[
            pl.BlockSpec((tm, 2 * PROJ_TILE), lambda i, j: (i, jnp.minimum(j, 1))),
            pl.BlockSpec((tm, PROJ_TILE), lambda i, j: (i, jnp.maximum(j - 2, 0))),
        ],
        out_shape=[jax.ShapeDtypeStruct((lp, qk_w), BF16), jax.ShapeDtypeStruct((lp, rest_w), BF16)],
        scratch_shapes=[pltpu.VMEM((tm, d), BF16)],
        compiler_params=_params(("arbitrary", "arbitrary")),
        name="in_proj",
    )(h, g, w_bf, cr, sr, ca, sa, qg, kg)


def _retention_kernel(lg_ref, q_ref, k_ref, v_ref, g_ref, gn_ref, o_ref, st_ref, dk_ref, dq_ref, dm_ref,
                      *, n_chunks):
    h = pl.program_id(0)
    lgf = lg_ref[0, h]
    lgb = lg_ref[1, h]
    half = CHUNK // 2
    row = lax.broadcasted_iota(jnp.int32, (CHUNK, CHUNK), 0).astype(F32)
    lane = lax.broadcasted_iota(jnp.int32, (CHUNK, CHUNK), 1).astype(F32)
    lane_lo = lane < half
    dk_ref[...] = jnp.exp(jnp.where(lane_lo, lgf * (CHUNK - 1.0 - row), lgb * row))
    dq_ref[...] = jnp.exp(jnp.where(lane_lo, lgf * (row + 1.0), lgb * (CHUNK - row)))
    diff = row - lane
    dm_ref[...] = 0.5 * jnp.where(diff >= 0, jnp.exp(lgf * jnp.maximum(diff, 0.0)),
                                  jnp.exp(lgb * jnp.maximum(-diff, 0.0)))
    cf = jnp.exp(jnp.full((half, CHUNK), lgf * CHUNK, F32))
    cb = jnp.exp(jnp.full((half, CHUNK), lgb * CHUNK, F32))

    def rows(n):
        return pl.ds(pl.multiple_of(n * CHUNK, CHUNK), CHUNK)

    def updates(n, c):
        kd = k_ref[rows(n), :].astype(F32) * dk_ref[...]
        st_ref[n] = jnp.dot(kd.T.astype(BF16), v_ref[rows(n), :], preferred_element_type=F32)
        return c

    lax.fori_loop(0, n_chunks, updates, 0)

    def scan_fwd(n, s):
        u = st_ref[n, 0:half, :]
        st_ref[n, 0:half, :] = s
        return cf * s + u

    lax.fori_loop(0, n_chunks, scan_fwd, jnp.zeros((half, CHUNK), F32))

    def scan_bwd(t, s):
        n = n_chunks - 1 - t
        u = st_ref[n, half:CHUNK, :]
        st_ref[n, half:CHUNK, :] = s
        return cb * s + u

    lax.fori_loop(0, n_chunks, scan_bwd, jnp.zeros((half, CHUNK), F32))

    def outputs(n, c):
        q = q_ref[rows(n), :]
        k = k_ref[rows(n), :]
        v = v_ref[rows(n), :]
        s2 = lax.dot_general(q, k, (((1,), (1,)), ((), ())), preferred_element_type=F32)
        intra = jnp.dot((s2 * dm_ref[...]).astype(BF16), v, preferred_element_type=F32)
        qd = (q.astype(F32) * dq_ref[...]).astype(BF16)
        cross = jnp.dot(qd, st_ref[n].astype(BF16), preferred_element_type=F32)
        o = intra + cross
        mu = jnp.mean(o, axis=-1, keepdims=True)
        dlt = o - mu
        var = jnp.mean(dlt * dlt, axis=-1, keepdims=True)
        on = dlt * lax.rsqrt(var + GROUPNORM_EPS)
        gate = g_ref[rows(n), :].astype(F32)
        o_ref[rows(n), :] = ((gate * jax.nn.sigmoid(gate)) * (on * gn_ref[...])).astype(BF16)
        return c

    lax.fori_loop(0, n_chunks, outputs, 0)


def _retention(log_gamma, qk, rest, gn_g):
    lp = qk.shape[0]
    n_chunks = lp // CHUNK
    col = lambda off: pl.BlockSpec((lp, LANES), lambda h, lg: (0, h + off))
    return pl.pallas_call(
        functools.partial(_retention_kernel, n_chunks=n_chunks),
        grid_spec=pltpu.PrefetchScalarGridSpec(
            num_scalar_prefetch=1,
            grid=(RET_HEADS,),
            in_specs=[col(0), col(RET_HEADS), col(0), col(RET_HEADS),
                      pl.BlockSpec((1, LANES), lambda h, lg: (0, h))],
            out_specs=pl.BlockSpec((lp, LANES), lambda h, lg: (0, h)),
            scratch_shapes=[pltpu.VMEM((n_chunks, CHUNK, CHUNK), F32),
                            pltpu.VMEM((CHUNK, CHUNK), F32),
                            pltpu.VMEM((CHUNK, CHUNK), F32),
                            pltpu.VMEM((CHUNK, CHUNK), F32)],
        ),
        out_shape=jax.ShapeDtypeStruct((lp, RET_HEADS * RET_V_DIM), BF16),
        compiler_params=_params(("arbitrary",)),
        name="retention",
    )(log_gamma, qk, qk, rest, rest, gn_g)


def _attention_kernel(q_ref, k_ref, vt_ref, km_ref, vtm_ref, o_ref, s_scr, mx_scr, m_scr, acc_scr,
                      *, tq, tk, n_kc):
    q_all = jnp.concatenate([q_ref[:, g * LANES:(g + 1) * LANES] for g in range(ATTN_GROUP)], axis=0)
    n_strips = (ATTN_GROUP * tq) // ATTN_STRIP
    q_strips = [q_all[s * ATTN_STRIP:(s + 1) * ATTN_STRIP, :] for s in range(n_strips)]

    def scores(k, s):
        return lax.dot_general(k, q_strips[s], (((1,), (1,)), ((), ())), preferred_element_type=F32)

    def key_rows(c):
        return pl.ds(pl.multiple_of(PREFIX + c * tk, LANES), tk)

    def score_stage(c, slot):
        k = k_ref[key_rows(c), :]
        for s in range(n_strips):
            st = scores(k, s)
            s_scr[slot, s] = st
            mx_scr[slot, s] = jnp.max(st, axis=0, keepdims=True)

    def value_stage(c, slot):
        vt = vt_ref[0, :, key_rows(c)]
        for s in range(n_strips):
            m_old = m_scr[s]
            m_new = jnp.maximum(m_old, mx_scr[slot, s])
            p = jnp.exp2(s_scr[slot, s] - m_new).astype(BF16)
            acc_scr[s] = jnp.exp2(m_old - m_new) * acc_scr[s] + jnp.dot(vt, p, preferred_element_type=F32)
            m_scr[s] = m_new

    for s in range(n_strips):
        st = scores(km_ref[0], s)
        m0 = jnp.max(st, axis=0, keepdims=True)
        m_scr[s] = m0
        acc_scr[s] = jnp.dot(vtm_ref[0], jnp.exp2(st - m0).astype(BF16), preferred_element_type=F32)

    score_stage(0, 0)

    def body(i, carry):
        c = 2 * i
        score_stage(c + 1, 1)
        value_stage(c, 0)
        score_stage(c + 2, 0)
        value_stage(c + 1, 1)
        return carry

    lax.fori_loop(0, n_kc // 2 - 1, body, 0)
    score_stage(n_kc - 1, 1)
    value_stage(n_kc - 2, 0)
    value_stage(n_kc - 1, 1)
    outs = []
    for s in range(n_strips):
        acc = acc_scr[s]
        outs.append((acc[0:ATTN_HEAD_DIM, :] / acc[ATTN_HEAD_DIM:ATTN_HEAD_DIM + 1, :]).T)
    o = jnp.concatenate(outs, axis=0)
    for g in range(ATTN_GROUP):
        o_ref[:, g * LANES:(g + 1) * LANES] = o[g * tq:(g + 1) * tq, :]


def _attention(rest, *, tq, tk):
    lp = rest.shape[0]
    n_tok = lp - PREFIX
    assert lp % tq == 0 and n_tok % (2 * tk) == 0 and (ATTN_GROUP * tq) % ATTN_STRIP == 0
    gw = ATTN_GROUP * ATTN_HEAD_DIM
    q_off = (2 * RET_HEADS * RET_V_DIM) // gw
    k_col = 2 * RET_HEADS * RET_V_DIM + ATTN_HEADS * ATTN_HEAD_DIM
    k_off = k_col // LANES
    kv_w = ATTN_KV_HEADS * ATTN_HEAD_DIM
    vt = rest[:, k_col + kv_w:k_col + 2 * kv_w].T.reshape(ATTN_KV_HEADS, ATTN_HEAD_DIM, lp)
    vt = jnp.concatenate([vt, jnp.ones((ATTN_KV_HEADS, ATTN_ONES_ROWS, lp), BF16)], axis=1)
    vt_meta = vt[:, :, PAD:PREFIX]
    k_meta = rest[PAD:PREFIX, k_col:k_col + kv_w].reshape(N_META, ATTN_KV_HEADS, ATTN_HEAD_DIM)
    k_meta = jnp.swapaxes(k_meta, 0, 1)
    n_strips = (ATTN_GROUP * tq) // ATTN_STRIP
    vt_rows = ATTN_HEAD_DIM + ATTN_ONES_ROWS
    return pl.pallas_call(
        functools.partial(_attention_kernel, tq=tq, tk=tk, n_kc=n_tok // tk),
        grid=(ATTN_KV_HEADS, lp // tq),
        in_specs=[pl.BlockSpec((tq, gw), lambda kv, i: (i, q_off + kv)),
                  pl.BlockSpec((lp, LANES), lambda kv, i: (0, k_off + kv)),
                  pl.BlockSpec((1, vt_rows, lp), lambda kv, i: (kv, 0, 0)),
                  pl.BlockSpec((1, N_META, ATTN_HEAD_DIM), lambda kv, i: (kv, 0, 0)),
                  pl.BlockSpec((1, vt_rows, N_META), lambda kv, i: (kv, 0, 0))],
        out_specs=pl.BlockSpec((tq, gw), lambda kv, i: (i, kv)),
        out_shape=jax.ShapeDtypeStruct((lp, ATTN_HEADS * ATTN_HEAD_DIM), F32),
        scratch_shapes=[pltpu.VMEM((2, n_strips, tk, ATTN_STRIP), F32),
                        pltpu.VMEM((2, n_strips, 1, ATTN_STRIP), F32),
                        pltpu.VMEM((n_strips, 1, ATTN_STRIP), F32),
                        pltpu.VMEM((n_strips, vt_rows, ATTN_STRIP), F32)],
        compiler_params=_params(("arbitrary", "arbitrary")),
        name="attention",
    )(rest, rest, vt, k_meta, vt_meta)


def _mix_router_kernel(ret_ref, att_ref, ang_ref, wo_ref, h_ref, fg_ref, wr_ref, br_ref,
                       h1_ref, f_ref, route_ref, cnt_ref, carry_scr, *, tm):
    i = pl.program_id(0)

    @pl.when(i == 0)
    def _():
        carry_scr[...] = jnp.zeros(carry_scr.shape, F32)

    att = att_ref[...]
    ms = jnp.mean(att * att, axis=-1, keepdims=True)
    att_n = ((att * lax.rsqrt(ms + NORM_EPS)) * ang_ref[...]).astype(BF16)
    mix_in = jnp.concatenate([ret_ref[...], att_n], axis=1)
    h1 = h_ref[...] + jnp.dot(mix_in, wo_ref[...], preferred_element_type=F32)
    h1_ref[...] = h1
    ms1 = jnp.mean(h1 * h1, axis=-1, keepdims=True)
    f = (h1 * lax.rsqrt(ms1 + NORM_EPS)) * fg_ref[...]
    f_ref[...] = f

    logits = jnp.dot(f, wr_ref[...], preferred_element_type=F32,
                     precision=lax.Precision.HIGHEST) + br_ref[...]
    lane = lax.broadcasted_iota(jnp.int32, (tm, LANES), 1).astype(F32)
    far = float(LANES)
    g_logit = jnp.where(lane < N_GROUPS, logits, NEG_BIG)
    g_max = jnp.max(g_logit, axis=-1, keepdims=True)
    g_w = 1.0 / jnp.sum(jnp.exp(g_logit - g_max), axis=-1, keepdims=True)
    g_idx = jnp.min(jnp.where(g_logit == g_max, lane, far), axis=-1, keepdims=True)
    first = N_GROUPS + g_idx * EXPERTS_PER_GROUP
    e_logit = jnp.where((lane >= first) & (lane < first + EXPERTS_PER_GROUP), logits, NEG_BIG)
    v1 = jnp.max(e_logit, axis=-1, keepdims=True)
    i1 = jnp.min(jnp.where(e_logit == v1, lane, far), axis=-1, keepdims=True)
    e_rest = jnp.where(lane == i1, NEG_BIG, e_logit)
    v2 = jnp.max(e_rest, axis=-1, keepdims=True)
    i2 = jnp.min(jnp.where(e_rest == v2, lane, far), axis=-1, keepdims=True)
    e2 = jnp.exp(v2 - v1)
    w1 = g_w / (1.0 + e2)
    w2 = g_w * e2 / (1.0 + e2)

    oh1 = (lane == i1).astype(F32)
    oh2 = (lane == i2).astype(F32)
    oh = oh1 + oh2
    r_i = lax.broadcasted_iota(jnp.int32, (tm, tm), 0)
    c_i = lax.broadcasted_iota(jnp.int32, (tm, tm), 1)
    lower = (c_i < r_i).astype(BF16)
    before = jnp.dot(lower, oh.astype(BF16), preferred_element_type=F32) + carry_scr[...]
    rank1 = jnp.sum(before * oh1, axis=-1, keepdims=True)
    rank2 = jnp.sum(before * oh2, axis=-1, keepdims=True)
    carry_scr[...] = carry_scr[...] + jnp.sum(oh, axis=0, keepdims=True)
    cnt_ref[...] = carry_scr[...]

    route = jnp.where(lane == 0, i1 - N_GROUPS, 0.0)
    route = jnp.where(lane == 1, i2 - N_GROUPS, route)
    route = jnp.where(lane == 2, w1, route)
    route = jnp.where(lane == 3, w2, route)
    route = jnp.where(lane == 4, rank1, route)
    route = jnp.where(lane == 5, rank2, route)
    route_ref[...] = route


def _mix_router(ret, att, ang, wo_bf, h, fg, wr, br, *, tm):
    lp, d = h.shape
    assert lp % tm == 0
    rowblk = lambda n: pl.BlockSpec((tm, n), lambda i: (i, 0))
    vec = lambda n: pl.BlockSpec((1, n), lambda i: (0, 0))
    full = lambda a: pl.BlockSpec(a.shape, lambda i: (0, 0))
    return pl.pallas_call(
        functools.partial(_mix_router_kernel, tm=tm),
        grid=(lp // tm,),
        in_specs=[rowblk(ret.shape[1]), rowblk(att.shape[1]), vec(att.shape[1]), full(wo_bf),
                  rowblk(d), vec(d), full(wr), vec(LANES)],
        out_specs=[rowblk(d), rowblk(d), rowblk(LANES), vec(LANES)],
        out_shape=[jax.ShapeDtypeStruct((lp, d), F32), jax.ShapeDtypeStruct((lp, d), F32),
                   jax.ShapeDtypeStruct((lp, LANES), F32), jax.ShapeDtypeStruct((1, LANES), F32)],
        scratch_shapes=[pltpu.VMEM((1, LANES), F32)],
        compiler_params=_params(("arbitrary",)),
        name="mix_router",
    )(ret, att, ang, wo_bf, h, fg, wr, br)


def _row_copy(src_ref, src_row, dst_ref, dst_row, sem):
    return pltpu.make_async_copy(src_ref.at[pl.ds(src_row, 1), :], dst_ref.at[pl.ds(dst_row, 1), :], sem)


def _moe_kernel(ie_ref, irow_ref, irows_ref, sidx_ref, f_ref, wg_ref, wu_ref, wd_ref, o_ref,
                x_scr, y_scr, gsem, ssem, *, n_items, n_chunks, lp):
    del ie_ref
    i = pl.program_id(0)
    c = pl.program_id(1)
    n_rows = irows_ref[i]
    slot = i % 2

    def group_wait(src_ref, dst_ref, sem):
        pltpu.make_async_copy(src_ref.at[pl.ds(0, DMA_GROUP), :], dst_ref.at[pl.ds(0, DMA_GROUP), :],
                              sem).wait()

    def gather(item, dst_slot, start):
        row0 = irow_ref[item]
        dst = x_scr.at[dst_slot]
        sem = gsem.at[dst_slot]

        def group(g, carry):
            if start:
                for u in range(DMA_GROUP):
                    r = g * DMA_GROUP + u
                    a = sidx_ref[row0 + r]
                    tok = jnp.where(a >= lp, a - lp, jnp.maximum(a, 0))
                    _row_copy(f_ref, tok, dst, r, sem).start()
            else:
                group_wait(f_ref, dst, sem)
            return carry

        lax.fori_loop(0, (irows_ref[item] + DMA_GROUP - 1) // DMA_GROUP, group, 0)

    def scatter(item, start):
        row0 = irow_ref[item]
        n = irows_ref[item]
        n_groups = n // DMA_GROUP

        def group(g, carry):
            if start:
                for u in range(DMA_GROUP):
                    r = g * DMA_GROUP + u
                    _row_copy(y_scr, r, o_ref, sidx_ref[row0 + r], ssem).start()
            else:
                group_wait(y_scr, o_ref, ssem)
            return carry

        lax.fori_loop(0, n_groups, group, 0)

        def single(r, carry):
            cp = _row_copy(y_scr, r, o_ref, sidx_ref[row0 + r], ssem)
            cp.start() if start else cp.wait()
            return carry

        lax.fori_loop(n_groups * DMA_GROUP, n, single, 0)

    @pl.when(c == 0)
    def _():
        @pl.when(i == 0)
        def _():
            x_scr[...] = jnp.zeros(x_scr.shape, F32)
            y_scr[...] = jnp.zeros(y_scr.shape, F32)
            gather(0, 0, True)

        @pl.when(i + 1 < n_items)
        def _():
            gather(i + 1, 1 - slot, True)

        gather(i, slot, False)

        @pl.when(i > 0)
        def _():
            scatter(i - 1, False)

    n_blocks = (n_rows + EXPERT_BLOCK - 1) // EXPERT_BLOCK
    for v in range(1, MOE_ITEM_ROWS // EXPERT_BLOCK + 1):
        @pl.when(n_blocks == v)
        def _(m=v * EXPERT_BLOCK):
            x = x_scr[slot, 0:m, :].astype(BF16)
            gate = jnp.dot(x, wg_ref[0].astype(BF16), preferred_element_type=F32)
            up = jnp.dot(x, wu_ref[0].astype(BF16), preferred_element_type=F32)
            hid = ((gate * jax.nn.sigmoid(gate)) * up).astype(BF16)
            y = jnp.dot(hid, wd_ref[0].astype(BF16), preferred_element_type=F32)
            y_scr[0:m, :] = jnp.where(c > 0, y_scr[0:m, :], 0.0) + y

    @pl.when(c == n_chunks - 1)
    def _():
        scatter(i, True)

        @pl.when(i == n_items - 1)
        def _():
            scatter(i, False)


def _moe_experts(item_e, item_row, item_rows, sidx, f, w_gate, w_up, w_down, *, chunk):
    lp, d = f.shape
    d_e = w_gate.shape[2]
    n_items = item_e.shape[0]
    n_chunks = d_e // chunk
    return pl.pallas_call(
        functools.partial(_moe_kernel, n_items=n_items, n_chunks=n_chunks, lp=lp),
        grid_spec=pltpu.PrefetchScalarGridSpec(
            num_scalar_prefetch=4,
            grid=(n_items, n_chunks),
            in_specs=[pl.BlockSpec(memory_space=pl.ANY),
                      pl.BlockSpec((1, d, chunk), lambda i, c, ie, ir, irs, sx: (ie[i], 0, c)),
                      pl.BlockSpec((1, d, chunk), lambda i, c, ie, ir, irs, sx: (ie[i], 0, c)),
                      pl.BlockSpec((1, chunk, d), lambda i, c, ie, ir, irs, sx: (ie[i], c, 0))],
            out_specs=pl.BlockSpec(memory_space=pl.ANY),
            scratch_shapes=[pltpu.VMEM((2, MOE_ITEM_ROWS, d), F32),
                            pltpu.VMEM((MOE_ITEM_ROWS, d), F32),
                            pltpu.SemaphoreType.DMA((2,)),
                            pltpu.SemaphoreType.DMA(())],
        ),
        out_shape=jax.ShapeDtypeStruct((TOP_K * lp, d), F32),
        compiler_params=_params(("arbitrary", "arbitrary")),
        name="moe_experts",
    )(item_e, item_row, item_rows, sidx, f, w_gate, w_up, w_down)


def _combine_kernel(h1_ref, route_ref, y0_ref, y1_ref, g_ref, o_ref):
    route = route_ref[...]
    h2 = h1_ref[...] + (route[:, 2:3] * y0_ref[0] + route[:, 3:4] * y1_ref[0])
    ms = jnp.mean(h2 * h2, axis=-1, keepdims=True)
    o_ref[...] = (h2 * lax.rsqrt(ms + NORM_EPS)) * g_ref[...]


def _combine(h1, route, y, g, *, tm):
    lp, d = h1.shape
    assert PREFIX % tm == 0
    skip = PREFIX // tm
    n_tok = lp - PREFIX
    y3 = y.reshape(TOP_K, lp, d)
    return pl.pallas_call(
        _combine_kernel,
        grid=(n_tok // tm,),
        in_specs=[pl.BlockSpec((tm, d), lambda i: (i + skip, 0)),
                  pl.BlockSpec((tm, LANES), lambda i: (i + skip, 0)),
                  pl.BlockSpec((1, tm, d), lambda i: (0, i + skip, 0)),
                  pl.BlockSpec((1, tm, d), lambda i: (1, i + skip, 0)),
                  pl.BlockSpec((1, d), lambda i: (0, 0))],
        out_specs=pl.BlockSpec((tm, d), lambda i: (i, 0)),
        out_shape=jax.ShapeDtypeStruct((n_tok, d), F32),
        compiler_params=_params(("arbitrary",)),
        name="combine",
    )(h1, route, y3, y3, g)


def _row_tile(lp, candidates):
    for t in candidates:
        if lp % t == 0:
            return t
    raise ValueError(f"no row tile for {lp}")


def kernel(x, meta_tokens, norm_mix_g, w_in, ret_decay_logit_fwd, ret_decay_logit_bwd, ret_norm_g,
           attn_q_norm_g, attn_k_norm_g, attn_out_norm_g, w_out, norm_ffn_g, router_group_w,
           router_group_b, router_expert_w, router_expert_b, expert_w_gate, expert_w_up, expert_w_down,
           norm_final_g):
    batch, n_tok, d = x.shape
    assert batch == 1 and norm_mix_g.shape[0] == 1
    lp = n_tok + PREFIX
    h = jnp.concatenate([jnp.zeros((PAD, d), x.dtype), meta_tokens.astype(x.dtype), x[0]], axis=0)

    cr, sr = _rope_tables(n_tok, RET_QK_DIM)
    ca, sa = _rope_tables(n_tok, ATTN_HEAD_DIM)
    qk, rest = _in_proj(h, norm_mix_g, w_in[0].astype(BF16), cr, sr, ca, sa,
                        attn_q_norm_g, attn_k_norm_g, tm=_row_tile(lp, (640, 128)))

    log_gamma = jnp.stack([jax.nn.log_sigmoid(ret_decay_logit_fwd[0].astype(F32)),
                           jax.nn.log_sigmoid(ret_decay_logit_bwd[0].astype(F32))])
    ret = _retention(log_gamma, qk, rest, ret_norm_g)
    att = _attention(rest, tq=_row_tile(lp, (320, 128, 64)), tk=_row_tile(n_tok, (512, 256, 128)))

    n_route = N_GROUPS + N_EXPERTS
    wr = jnp.pad(jnp.concatenate([router_group_w[0], router_expert_w[0]], axis=1).astype(F32),
                 ((0, 0), (0, LANES - n_route)))
    br = jnp.pad(jnp.concatenate([router_group_b[0], router_expert_b[0]]).astype(F32),
                 (0, LANES - n_route))[None]
    h1, f, route, cnt = _mix_router(ret, att, attn_out_norm_g, w_out[0].astype(BF16), h, norm_ffn_g,
                                    wr, br, tm=_row_tile(lp, (320, 128)))

    expert = route[:, 0:TOP_K].astype(jnp.int32)
    rank = route[:, 4:4 + TOP_K].astype(jnp.int32)
    counts = cnt[0, N_GROUPS:n_route].astype(jnp.int32)
    padded = (counts + EXPERT_BLOCK - 1) // EXPERT_BLOCK * EXPERT_BLOCK
    pad_end = jnp.cumsum(padded)
    pad_start = pad_end - padded
    dest = (pad_start[expert] + rank).T.reshape(-1)
    n_rows = (-(-(lp * TOP_K) // EXPERT_BLOCK) + N_EXPERTS) * EXPERT_BLOCK
    sidx = jnp.full((n_rows,), -1, jnp.int32).at[dest].set(jnp.arange(TOP_K * lp, dtype=jnp.int32))
    groups = (counts + MOE_ITEM_ROWS - 1) // MOE_ITEM_ROWS
    g_end = jnp.cumsum(groups)
    g_start = g_end - groups
    n_items = N_EXPERTS + -(-(lp * TOP_K) // MOE_ITEM_ROWS)
    ii = jnp.arange(n_items, dtype=jnp.int32)
    valid = ii < g_end[-1]
    item_e = jnp.sum((g_end[None, :] <= jnp.minimum(ii, g_end[-1] - 1)[:, None]).astype(jnp.int32), axis=1)
    g_in = ii - g_start[item_e]
    item_row = jnp.where(valid, pad_start[item_e] + g_in * MOE_ITEM_ROWS, 0).astype(jnp.int32)
    item_rows = jnp.where(valid, jnp.minimum(MOE_ITEM_ROWS, counts[item_e] - g_in * MOE_ITEM_ROWS),
                          0).astype(jnp.int32)

    y = _moe_experts(item_e, item_row, item_rows, sidx, f, expert_w_gate[0], expert_w_up[0],
                     expert_w_down[0], chunk=MOE_CHUNK)
    out = _combine(h1, route, y, norm_final_g[None], tm=PREFIX)
    return out[None]
```

```python
import functools

import jax
import jax.numpy as jnp
from jax import lax
from jax.experimental import pallas as pl
from jax.experimental.pallas import tpu as pltpu

F32 = jnp.float32
BF16 = jnp.bfloat16

N_META = 16
CHUNK = 128
PREFIX = CHUNK
PAD = PREFIX - N_META
GRID_W = 64
ROPE_THETA = 10000.0
NORM_EPS = 1e-6
GROUPNORM_EPS = 1e-5

RET_HEADS = 8
RET_QK_DIM = 64
RET_V_DIM = 128
RET_UNROLL = 5
ATTN_HEADS = 8
ATTN_KV_HEADS = 2
ATTN_HEAD_DIM = 128
ATTN_GROUP = ATTN_HEADS // ATTN_KV_HEADS
ATTN_STRIP = 256
ATTN_ONES_ROWS = 16
LOG2_E = 1.4426950408889634

N_GROUPS = 8
EXPERTS_PER_GROUP = 8
N_EXPERTS = N_GROUPS * EXPERTS_PER_GROUP
TOP_K = 2
EXPERT_BLOCK = 128
MOE_ITEM_ROWS = 512
MOE_CHUNK = 512
DMA_GROUP = 8

LANES = 128
PROJ_TILE = 512
VMEM_LIMIT = 56 * 1024 * 1024
NEG_BIG = -1e30


def _params(sem, vmem=VMEM_LIMIT):
    return pltpu.CompilerParams(dimension_semantics=sem, vmem_limit_bytes=vmem)


def _swap_halves(x, hw):
    w = x.shape[-1]
    lane = lax.broadcasted_iota(jnp.int32, x.shape, 1)
    first = (lane & hw) == 0
    return jnp.where(first, pltpu.roll(x, w - hw, 1), pltpu.roll(x, hw, 1))


def _rope_tables(n_tok, head_dim):
    half = head_dim // 2
    freqs = ROPE_THETA ** (-jnp.arange(0, half, 2, dtype=F32) / half)
    rows = n_tok // GRID_W
    row_ids = jnp.repeat(jnp.arange(rows), GRID_W).astype(F32)
    col_ids = jnp.tile(jnp.arange(GRID_W), rows).astype(F32)
    zeros_p = jnp.zeros((PREFIX,), F32)
    pos_row = jnp.concatenate([zeros_p, row_ids])
    pos_col = jnp.concatenate([zeros_p, col_ids])
    ang_r = pos_row[:, None] * freqs[None, :]
    ang_c = pos_col[:, None] * freqs[None, :]
    cos = jnp.concatenate([jnp.cos(ang_r), jnp.cos(ang_r), jnp.cos(ang_c), jnp.cos(ang_c)], axis=-1)
    sin = jnp.concatenate([-jnp.sin(ang_r), jnp.sin(ang_r), -jnp.sin(ang_c), jnp.sin(ang_c)], axis=-1)
    reps = LANES // head_dim
    return jnp.tile(cos, (1, reps)), jnp.tile(sin, (1, reps))


def _in_proj_kernel(x_ref, g_ref, w_ref, cr_ref, sr_ref, ca_ref, sa_ref, qg_ref, kg_ref,
                    oqk_ref, orest_ref, a_scr, *, tm):
    i = pl.program_id(0)
    j = pl.program_id(1)

    @pl.when(j == 0)
    def _():
        x = x_ref[...]
        ms = jnp.mean(x * x, axis=-1, keepdims=True)
        a_scr[...] = ((x * lax.rsqrt(ms + NORM_EPS)) * g_ref[...]).astype(BF16)

    acc = jnp.dot(a_scr[...], w_ref[...], preferred_element_type=F32)
    n_lane_groups = PROJ_TILE // LANES

    def tiled(ref):
        return jnp.concatenate([ref[...]] * n_lane_groups, axis=1)

    @pl.when(j < 2)
    def _():
        y = acc * tiled(cr_ref) + _swap_halves(acc, RET_QK_DIM // 4) * tiled(sr_ref)
        row = i * tm + lax.broadcasted_iota(jnp.int32, (tm, 1), 0)
        k_scale = jnp.where(row >= PAD, RET_QK_DIM ** -0.5, 0.0).astype(F32)
        y = y * jnp.where(j == 1, k_scale, jnp.ones_like(k_scale))
        lo = lax.broadcasted_iota(jnp.int32, (tm, LANES), 1) < RET_QK_DIM
        pieces = []
        for c in range(n_lane_groups):
            xg = y[:, c * LANES:(c + 1) * LANES]
            xr = pltpu.roll(xg, RET_QK_DIM, 1)
            pieces.append(jnp.where(lo, xg, xr))
            pieces.append(jnp.where(lo, xr, xg))
        oqk_ref[...] = jnp.concatenate(pieces, axis=1).astype(BF16)

    @pl.when((j >= 2) & (j < 6))
    def _():
        orest_ref[...] = acc.astype(BF16)

    def qk_norm_rope(xg, gain_ref):
        ms = jnp.mean(xg * xg, axis=-1, keepdims=True)
        xn = (xg * lax.rsqrt(ms + NORM_EPS)) * gain_ref[...]
        return xn * ca_ref[...] + _swap_halves(xn, ATTN_HEAD_DIM // 4) * sa_ref[...]

    @pl.when((j == 6) | (j == 7))
    def _():
        pieces = [qk_norm_rope(acc[:, c * LANES:(c + 1) * LANES], qg_ref) * (LOG2_E * ATTN_HEAD_DIM ** -0.5)
                  for c in range(n_lane_groups)]
        orest_ref[...] = jnp.concatenate(pieces, axis=1).astype(BF16)

    @pl.when(j == 8)
    def _():
        pieces = [qk_norm_rope(acc[:, c * LANES:(c + 1) * LANES], kg_ref) for c in range(ATTN_KV_HEADS)]
        pieces.append(acc[:, ATTN_KV_HEADS * LANES:])
        orest_ref[...] = jnp.concatenate(pieces, axis=1).astype(BF16)


def _in_proj(h, g, w_bf, cr, sr, ca, sa, qg, kg, *, tm):
    lp, d = h.shape
    n_out = w_bf.shape[1]
    nj = n_out // PROJ_TILE
    assert lp % tm == 0 and n_out % PROJ_TILE == 0 and nj == 9
    qk_w = 2 * 2 * RET_HEADS * RET_QK_DIM
    rest_w = n_out - 2 * RET_HEADS * RET_QK_DIM
    row_tab = pl.BlockSpec((tm, LANES), lambda i, j: (i, 0))
    vec = lambda n: pl.BlockSpec((1, n), lambda i, j: (0, 0))
    return pl.pallas_call(
        functools.partial(_in_proj_kernel, tm=tm),
        grid=(lp // tm, nj),
        in_specs=[
            pl.BlockSpec((tm, d), lambda i, j: (i, 0)),
            vec(d),
            pl.BlockSpec((d, PROJ_TILE), lambda i, j: (0, j)),
            row_tab, row_tab, row_tab, row_tab,
            vec(LANES), vec(LANES),
        ],
        out_specs=[
            pl.BlockSpec((tm, 2 * PROJ_TILE), lambda i, j: (i, jnp.minimum(j, 1))),
            pl.BlockSpec((tm, PROJ_TILE), lambda i, j: (i, jnp.maximum(j - 2, 0))),
        ],
        out_shape=[jax.ShapeDtypeStruct((lp, qk_w), BF16), jax.ShapeDtypeStruct((lp, rest_w), BF16)],
        scratch_shapes=[pltpu.VMEM((tm, d), BF16)],
        compiler_params=_params(("arbitrary", "arbitrary")),
        name="in_proj",
    )(h, g, w_bf, cr, sr, ca, sa, qg, kg)


def _retention_kernel(lg_ref, q_ref, k_ref, v_ref, g_ref, gn_ref, o_ref, st_ref, dk_ref, dq_ref, dm_ref,
                      *, n_chunks):
    h = pl.program_id(0)
    lgf = lg_ref[0, h]
    lgb = lg_ref[1, h]
    half = CHUNK // 2
    row = lax.broadcasted_iota(jnp.int32, (CHUNK, CHUNK), 0).astype(F32)
    lane = lax.broadcasted_iota(jnp.int32, (CHUNK, CHUNK), 1).astype(F32)
    lane_lo = lane < half
    dk_ref[...] = jnp.exp(jnp.where(lane_lo, lgf * (CHUNK - 1.0 - row), lgb * row))
    dq_ref[...] = jnp.exp(jnp.where(lane_lo, lgf * (row + 1.0), lgb * (CHUNK - row)))
    diff = row - lane
    dm_ref[...] = 0.5 * jnp.where(diff >= 0, jnp.exp(lgf * jnp.maximum(diff, 0.0)),
                                  jnp.exp(lgb * jnp.maximum(-diff, 0.0)))
    cf = jnp.exp(jnp.full((half, CHUNK), lgf * CHUNK, F32))
    cb = jnp.exp(jnp.full((half, CHUNK), lgb * CHUNK, F32))

    def rows(n):
        return pl.ds(pl.multiple_of(n * CHUNK, CHUNK), CHUNK)

    def updates(n, c):
        kd = k_ref[rows(n), :].astype(F32) * dk_ref[...]
        st_ref[n] = jnp.dot(kd.T.astype(BF16), v_ref[rows(n), :], preferred_element_type=F32)
        return c

    unroll = max(u for u in range(1, RET_UNROLL + 1) if n_chunks % u == 0)

    def unrolled(fn):
        def body(t, c):
            for u in range(unroll):
                fn(t * unroll + u, c)
            return c
        return body

    lax.fori_loop(0, n_chunks // unroll, unrolled(updates), 0)

    def scan_fwd(n, s):
        u = st_ref[n, 0:half, :]
        st_ref[n, 0:half, :] = s
        return cf * s + u

    lax.fori_loop(0, n_chunks, scan_fwd, jnp.zeros((half, CHUNK), F32))

    def scan_bwd(t, s):
        n = n_chunks - 1 - t
        u = st_ref[n, half:CHUNK, :]
        st_ref[n, half:CHUNK, :] = s
        return cb * s + u

    lax.fori_loop(0, n_chunks, scan_bwd, jnp.zeros((half, CHUNK), F32))

    def outputs(n, c):
        q = q_ref[rows(n), :]
        k = k_ref[rows(n), :]
        v = v_ref[rows(n), :]
        s2 = lax.dot_general(q, k, (((1,), (1,)), ((), ())), preferred_element_type=F32)
        intra = jnp.dot((s2 * dm_ref[...]).astype(BF16), v, preferred_element_type=F32)
        qd = (q.astype(F32) * dq_ref[...]).astype(BF16)
        cross = jnp.dot(qd, st_ref[n].astype(BF16), preferred_element_type=F32)
        o = intra + cross
        mu = jnp.mean(o, axis=-1, keepdims=True)
        dlt = o - mu
        var = jnp.mean(dlt * dlt, axis=-1, keepdims=True)
        on = dlt * lax.rsqrt(var + GROUPNORM_EPS)
        gate = g_ref[rows(n), :].astype(F32)
        o_ref[rows(n), :] = ((gate * jax.nn.sigmoid(gate)) * (on * gn_ref[...])).astype(BF16)
        return c

    lax.fori_loop(0, n_chunks // unroll, unrolled(outputs), 0)


def _retention(log_gamma, qk, rest, gn_g):
    lp = qk.shape[0]
    n_chunks = lp // CHUNK
    col = lambda off: pl.BlockSpec((lp, LANES), lambda h, lg: (0, h + off))
    return pl.pallas_call(
        functools.partial(_retention_kernel, n_chunks=n_chunks),
        grid_spec=pltpu.PrefetchScalarGridSpec(
            num_scalar_prefetch=1,
            grid=(RET_HEADS,),
            in_specs=[col(0), col(RET_HEADS), col(0), col(RET_HEADS),
                      pl.BlockSpec((1, LANES), lambda h, lg: (0, h))],
            out_specs=pl.BlockSpec((lp, LANES), lambda h, lg: (0, h)),
            scratch_shapes=[pltpu.VMEM((n_chunks, CHUNK, CHUNK), F32),
                            pltpu.VMEM((CHUNK, CHUNK), F32),
                            pltpu.VMEM((CHUNK, CHUNK), F32),
                            pltpu.VMEM((CHUNK, CHUNK), F32)],
        ),
        out_shape=jax.ShapeDtypeStruct((lp, RET_HEADS * RET_V_DIM), BF16),
        compiler_params=_params(("arbitrary",)),
        name="retention",
    )(log_gamma, qk, qk, rest, rest, gn_g)


def _attention_kernel(q_ref, k_ref, vt_ref, km_ref, vtm_ref, o_ref, s_scr, mx_scr, m_scr, acc_scr,
                      *, tq, tk, n_kc):
    q_all = jnp.concatenate([q_ref[:, g * LANES:(g + 1) * LANES] for g in range(ATTN_GROUP)], axis=0)
    n_strips = (ATTN_GROUP * tq) // ATTN_STRIP
    q_strips = [q_all[s * ATTN_STRIP:(s + 1) * ATTN_STRIP, :] for s in range(n_strips)]

    def scores(k, s):
        return lax.dot_general(k, q_strips[s], (((1,), (1,)), ((), ())), preferred_element_type=F32)

    def key_rows(c):
        return pl.ds(pl.multiple_of(PREFIX + c * tk, LANES), tk)

    def score_stage(c, slot):
        k = k_ref[key_rows(c), :]
        for s in range(n_strips):
            st = scores(k, s)
            s_scr[slot, s] = st
            mx_scr[slot, s] = jnp.max(st, axis=0, keepdims=True)

    def value_stage(c, slot):
        vt = vt_ref[0, :, key_rows(c)]
        for s in range(n_strips):
            m_old = m_scr[s]
            m_new = jnp.maximum(m_old, mx_scr[slot, s])
            p = jnp.exp2(s_scr[slot, s] - m_new).astype(BF16)
            acc_scr[s] = jnp.exp2(m_old - m_new) * acc_scr[s] + jnp.dot(vt, p, preferred_element_type=F32)
            m_scr[s] = m_new

    for s in range(n_strips):
        st = scores(km_ref[0], s)
        m0 = jnp.max(st, axis=0, keepdims=True)
        m_scr[s] = m0
        acc_scr[s] = jnp.dot(vtm_ref[0], jnp.exp2(st - m0).astype(BF16), preferred_element_type=F32)

    score_stage(0, 0)

    def body(i, carry):
        c = 2 * i
        score_stage(c + 1, 1)
        value_stage(c, 0)
        score_stage(c + 2, 0)
        value_stage(c + 1, 1)
        return carry

    lax.fori_loop(0, n_kc // 2 - 1, body, 0)
    score_stage(n_kc - 1, 1)
    value_stage(n_kc - 2, 0)
    value_stage(n_kc - 1, 1)
    outs = []
    for s in range(n_strips):
        acc = acc_scr[s]
        outs.append((acc[0:ATTN_HEAD_DIM, :] / acc[ATTN_HEAD_DIM:ATTN_HEAD_DIM + 1, :]).T)
    o = jnp.concatenate(outs, axis=0)
    for g in range(ATTN_GROUP):
        o_ref[:, g * LANES:(g + 1) * LANES] = o[g * tq:(g + 1) * tq, :]


def _attention(rest, *, tq, tk):
    lp = rest.shape[0]
    n_tok = lp - PREFIX
    assert lp % tq == 0 and n_tok % (2 * tk) == 0 and (ATTN_GROUP * tq) % ATTN_STRIP == 0
    gw = ATTN_GROUP * ATTN_HEAD_DIM
    q_off = (2 * RET_HEADS * RET_V_DIM) // gw
    k_col = 2 * RET_HEADS * RET_V_DIM + ATTN_HEADS * ATTN_HEAD_DIM
    k_off = k_col // LANES
    kv_w = ATTN_KV_HEADS * ATTN_HEAD_DIM
    vt = rest[:, k_col + kv_w:k_col + 2 * kv_w].T.reshape(ATTN_KV_HEADS, ATTN_HEAD_DIM, lp)
    vt = jnp.concatenate([vt, jnp.ones((ATTN_KV_HEADS, ATTN_ONES_ROWS, lp), BF16)], axis=1)
    vt_meta = vt[:, :, PAD:PREFIX]
    k_meta = rest[PAD:PREFIX, k_col:k_col + kv_w].reshape(N_META, ATTN_KV_HEADS, ATTN_HEAD_DIM)
    k_meta = jnp.swapaxes(k_meta, 0, 1)
    n_strips = (ATTN_GROUP * tq) // ATTN_STRIP
    vt_rows = ATTN_HEAD_DIM + ATTN_ONES_ROWS
    return pl.pallas_call(
        functools.partial(_attention_kernel, tq=tq, tk=tk, n_kc=n_tok // tk),
        grid=(ATTN_KV_HEADS, lp // tq),
        in_specs=[pl.BlockSpec((tq, gw), lambda kv, i: (i, q_off + kv)),
                  pl.BlockSpec((lp, LANES), lambda kv, i: (0, k_off + kv)),
                  pl.BlockSpec((1, vt_rows, lp), lambda kv, i: (kv, 0, 0)),
                  pl.BlockSpec((1, N_META, ATTN_HEAD_DIM), lambda kv, i: (kv, 0, 0)),
                  pl.BlockSpec((1, vt_rows, N_META), lambda kv, i: (kv, 0, 0))],
        out_specs=pl.BlockSpec((tq, gw), lambda kv, i: (i, kv)),
        out_shape=jax.ShapeDtypeStruct((lp, ATTN_HEADS * ATTN_HEAD_DIM), F32),
        scratch_shapes=[pltpu.VMEM((2, n_strips, tk, ATTN_STRIP), F32),
                        pltpu.VMEM((2, n_strips, 1, ATTN_STRIP), F32),
                        pltpu.VMEM((n_strips, 1, ATTN_STRIP), F32),
                        pltpu.VMEM((n_strips, vt_rows, ATTN_STRIP), F32)],
        compiler_params=_params(("arbitrary", "arbitrary")),
        name="attention",
    )(rest, rest, vt, k_meta, vt_meta)


def _mix_router_kernel(ret_ref, att_ref, ang_ref, wo_ref, h_ref, fg_ref, wr_ref, br_ref,
                       h1_ref, f_ref, route_ref, cnt_ref, carry_scr, *, tm):
    i = pl.program_id(0)

    @pl.when(i == 0)
    def _():
        carry_scr[...] = jnp.zeros(carry_scr.shape, F32)

    att = att_ref[...]
    ms = jnp.mean(att * att, axis=-1, keepdims=True)
    att_n = ((att * lax.rsqrt(ms + NORM_EPS)) * ang_ref[...]).astype(BF16)
    mix_in = jnp.concatenate([ret_ref[...], att_n], axis=1)
    h1 = h_ref[...] + jnp.dot(mix_in, wo_ref[...], preferred_element_type=F32)
    h1_ref[...] = h1
    ms1 = jnp.mean(h1 * h1, axis=-1, keepdims=True)
    f = (h1 * lax.rsqrt(ms1 + NORM_EPS)) * fg_ref[...]
    f_ref[...] = f

    f_hi = f.astype(BF16)
    f_lo = (f - f_hi.astype(F32)).astype(BF16)
    logits = jnp.dot(jnp.concatenate([f_hi, f_lo, f_hi], axis=1), wr_ref[...],
                     preferred_element_type=F32) + br_ref[...]
    lane = lax.broadcasted_iota(jnp.int32, (tm, LANES), 1).astype(F32)
    far = float(LANES)
    g_logit = jnp.where(lane < N_GROUPS, logits, NEG_BIG)
    g_max = jnp.max(g_logit, axis=-1, keepdims=True)
    g_w = 1.0 / jnp.sum(jnp.exp(g_logit - g_max), axis=-1, keepdims=True)
    g_idx = jnp.min(jnp.where(g_logit == g_max, lane, far), axis=-1, keepdims=True)
    first = N_GROUPS + g_idx * EXPERTS_PER_GROUP
    e_logit = jnp.where((lane >= first) & (lane < first + EXPERTS_PER_GROUP), logits, NEG_BIG)
    v1 = jnp.max(e_logit, axis=-1, keepdims=True)
    i1 = jnp.min(jnp.where(e_logit == v1, lane, far), axis=-1, keepdims=True)
    e_rest = jnp.where(lane == i1, NEG_BIG, e_logit)
    v2 = jnp.max(e_rest, axis=-1, keepdims=True)
    i2 = jnp.min(jnp.where(e_rest == v2, lane, far), axis=-1, keepdims=True)
    e2 = jnp.exp(v2 - v1)
    w1 = g_w / (1.0 + e2)
    w2 = g_w * e2 / (1.0 + e2)

    oh1 = (lane == i1).astype(F32)
    oh2 = (lane == i2).astype(F32)
    oh = oh1 + oh2
    r_i = lax.broadcasted_iota(jnp.int32, (tm, tm), 0)
    c_i = lax.broadcasted_iota(jnp.int32, (tm, tm), 1)
    lower = (c_i < r_i).astype(BF16)
    before = jnp.dot(lower, oh.astype(BF16), preferred_element_type=F32) + carry_scr[...]
    rank1 = jnp.sum(before * oh1, axis=-1, keepdims=True)
    rank2 = jnp.sum(before * oh2, axis=-1, keepdims=True)
    carry_scr[...] = carry_scr[...] + jnp.sum(oh, axis=0, keepdims=True)
    cnt_ref[...] = carry_scr[...]

    route = jnp.where(lane == 0, i1 - N_GROUPS, 0.0)
    route = jnp.where(lane == 1, i2 - N_GROUPS, route)
    route = jnp.where(lane == 2, w1, route)
    route = jnp.where(lane == 3, w2, route)
    route = jnp.where(lane == 4, rank1, route)
    route = jnp.where(lane == 5, rank2, route)
    route_ref[...] = route


def _mix_router(ret, att, ang, wo_bf, h, fg, wr, br, *, tm):
    lp, d = h.shape
    assert lp % tm == 0
    rowblk = lambda n: pl.BlockSpec((tm, n), lambda i: (i, 0))
    vec = lambda n: pl.BlockSpec((1, n), lambda i: (0, 0))
    full = lambda a: pl.BlockSpec(a.shape, lambda i: (0, 0))
    return pl.pallas_call(
        functools.partial(_mix_router_kernel, tm=tm),
        grid=(lp // tm,),
        in_specs=[rowblk(ret.shape[1]), rowblk(att.shape[1]), vec(att.shape[1]), full(wo_bf),
                  rowblk(d), vec(d), full(wr), vec(LANES)],
        out_specs=[rowblk(d), rowblk(d), rowblk(LANES), vec(LANES)],
        out_shape=[jax.ShapeDtypeStruct((lp, d), F32), jax.ShapeDtypeStruct((lp, d), F32),
                   jax.ShapeDtypeStruct((lp, LANES), F32), jax.ShapeDtypeStruct((1, LANES), F32)],
        scratch_shapes=[pltpu.VMEM((1, LANES), F32)],
        compiler_params=_params(("arbitrary",)),
        name="mix_router",
    )(ret, att, ang, wo_bf, h, fg, wr, br)


def _row_copy(src_ref, src_row, dst_ref, dst_row, sem):
    return pltpu.make_async_copy(src_ref.at[pl.ds(src_row, 1), :], dst_ref.at[pl.ds(dst_row, 1), :], sem)


def _moe_kernel(ie_ref, irow_ref, irows_ref, sidx_ref, f_ref, wg_hbm, wu_hbm, wd_hbm, o_ref,
                x_scr, y_scr, wg_buf, wu_buf, wd_buf, gsem, ssem, wsem, *, n_items, n_chunks, lp, chunk):
    i = pl.program_id(0)
    c = pl.program_id(1)
    n_rows = irows_ref[i]
    slot = i % 2
    w_slot = (i * n_chunks + c) % 2

    def weights(item, cc, ws, start):
        e = ie_ref[item]
        half = chunk // 2
        col0 = pl.multiple_of(cc * chunk, chunk)
        copies = (
            (wg_hbm.at[e, :, pl.ds(col0, chunk)], wg_buf.at[ws]),
            (wu_hbm.at[e, :, pl.ds(col0, chunk)], wu_buf.at[ws]),
            (wd_hbm.at[e, pl.ds(col0, half), :], wd_buf.at[ws, 0:half, :]),
            (wd_hbm.at[e, pl.ds(col0 + half, half), :], wd_buf.at[ws, half:chunk, :]),
        )

        @pl.when(irows_ref[item] > 0)
        def _():
            for n, (src, dst) in enumerate(copies):
                cp = pltpu.make_async_copy(src, dst, wsem.at[ws, n])
                cp.start(priority=n % 2) if start else cp.wait()

    @pl.when((i == 0) & (c == 0))
    def _():
        weights(0, 0, 0, True)

    @pl.when(c + 1 < n_chunks)
    def _():
        weights(i, c + 1, 1 - w_slot, True)

    @pl.when((c + 1 == n_chunks) & (i + 1 < n_items))
    def _():
        weights(i + 1, 0, 1 - w_slot, True)

    weights(i, c, w_slot, False)

    def group_wait(src_ref, dst_ref, sem):
        pltpu.make_async_copy(src_ref.at[pl.ds(0, DMA_GROUP), :], dst_ref.at[pl.ds(0, DMA_GROUP), :],
                              sem).wait()

    def gather(item, dst_slot, start):
        row0 = irow_ref[item]
        dst = x_scr.at[dst_slot]
        sem = gsem.at[dst_slot]

        def group(g, carry):
            if start:
                for u in range(DMA_GROUP):
                    r = g * DMA_GROUP + u
                    a = sidx_ref[row0 + r]
                    tok = jnp.where(a >= lp, a - lp, jnp.maximum(a, 0))
                    _row_copy(f_ref, tok, dst, r, sem).start(priority=u % 2)
            else:
                group_wait(f_ref, dst, sem)
            return carry

        lax.fori_loop(0, (irows_ref[item] + DMA_GROUP - 1) // DMA_GROUP, group, 0)

    def scatter(item, start):
        row0 = irow_ref[item]
        n = irows_ref[item]
        n_groups = n // DMA_GROUP

        def group(g, carry):
            if start:
                for u in range(DMA_GROUP):
                    r = g * DMA_GROUP + u
                    _row_copy(y_scr, r, o_ref, sidx_ref[row0 + r], ssem).start(priority=u % 2)
            else:
                group_wait(y_scr, o_ref, ssem)
            return carry

        lax.fori_loop(0, n_groups, group, 0)

        def single(r, carry):
            cp = _row_copy(y_scr, r, o_ref, sidx_ref[row0 + r], ssem)
            cp.start() if start else cp.wait()
            return carry

        lax.fori_loop(n_groups * DMA_GROUP, n, single, 0)

    @pl.when(c == 0)
    def _():
        @pl.when(i == 0)
        def _():
            x_scr[...] = jnp.zeros(x_scr.shape, F32)
            y_scr[...] = jnp.zeros(y_scr.shape, F32)
            gather(0, 0, True)

        @pl.when(i + 1 < n_items)
        def _():
            gather(i + 1, 1 - slot, True)

        gather(i, slot, False)

        @pl.when(i > 0)
        def _():
            scatter(i - 1, False)

    n_blocks = (n_rows + EXPERT_BLOCK - 1) // EXPERT_BLOCK
    for v in range(1, MOE_ITEM_ROWS // EXPERT_BLOCK + 1):
        @pl.when(n_blocks == v)
        def _(m=v * EXPERT_BLOCK):
            x = x_scr[slot, 0:m, :].astype(BF16)
            gate = jnp.dot(x, wg_buf[w_slot].astype(BF16), preferred_element_type=F32)
            up = jnp.dot(x, wu_buf[w_slot].astype(BF16), preferred_element_type=F32)
            hid = ((gate * jax.nn.sigmoid(gate)) * up).astype(BF16)
            y = jnp.dot(hid, wd_buf[w_slot].astype(BF16), preferred_element_type=F32)
            y_scr[0:m, :] = jnp.where(c > 0, y_scr[0:m, :], 0.0) + y

    @pl.when(c == n_chunks - 1)
    def _():
        scatter(i, True)

        @pl.when(i == n_items - 1)
        def _():
            scatter(i, False)


def _moe_experts(item_e, item_row, item_rows, sidx, f, w_gate, w_up, w_down, *, chunk):
    lp, d = f.shape
    d_e = w_gate.shape[2]
    n_items = item_e.shape[0]
    n_chunks = d_e // chunk
    return pl.pallas_call(
        functools.partial(_moe_kernel, n_items=n_items, n_chunks=n_chunks, lp=lp, chunk=chunk),
        grid_spec=pltpu.PrefetchScalarGridSpec(
            num_scalar_prefetch=4,
            grid=(n_items, n_chunks),
            in_specs=[pl.BlockSpec(memory_space=pl.ANY)] * 4,
            out_specs=pl.BlockSpec(memory_space=pl.ANY),
            scratch_shapes=[pltpu.VMEM((2, MOE_ITEM_ROWS, d), F32),
                            pltpu.VMEM((MOE_ITEM_ROWS, d), F32),
                            pltpu.VMEM((2, d, chunk), F32),
                            pltpu.VMEM((2, d, chunk), F32),
                            pltpu.VMEM((2, chunk, d), F32),
                            pltpu.SemaphoreType.DMA((2,)),
                            pltpu.SemaphoreType.DMA(()),
                            pltpu.SemaphoreType.DMA((2, 4))],
        ),
        out_shape=jax.ShapeDtypeStruct((TOP_K * lp, d), F32),
        compiler_params=_params(("arbitrary", "arbitrary")),
        name="moe_experts",
    )(item_e, item_row, item_rows, sidx, f, w_gate, w_up, w_down)


def _combine_kernel(h1_ref, route_ref, y0_ref, y1_ref, g_ref, o_ref):
    route = route_ref[...]
    h2 = h1_ref[...] + (route[:, 2:3] * y0_ref[0] + route[:, 3:4] * y1_ref[0])
    ms = jnp.mean(h2 * h2, axis=-1, keepdims=True)
    o_ref[...] = (h2 * lax.rsqrt(ms + NORM_EPS)) * g_ref[...]


def _combine(h1, route, y, g, *, tm):
    lp, d = h1.shape
    assert PREFIX % tm == 0
    skip = PREFIX // tm
    n_tok = lp - PREFIX
    y3 = y.reshape(TOP_K, lp, d)
    return pl.pallas_call(
        _combine_kernel,
        grid=(n_tok // tm,),
        in_specs=[pl.BlockSpec((tm, d), lambda i: (i + skip, 0)),
                  pl.BlockSpec((tm, LANES), lambda i: (i + skip, 0)),
                  pl.BlockSpec((1, tm, d), lambda i: (0, i + skip, 0)),
                  pl.BlockSpec((1, tm, d), lambda i: (1, i + skip, 0)),
                  pl.BlockSpec((1, d), lambda i: (0, 0))],
        out_specs=pl.BlockSpec((tm, d), lambda i: (i, 0)),
        out_shape=jax.ShapeDtypeStruct((n_tok, d), F32),
        compiler_params=_params(("arbitrary",)),
        name="combine",
    )(h1, route, y3, y3, g)


def _row_tile(lp, candidates):
    for t in candidates:
        if lp % t == 0:
            return t
    raise ValueError(f"no row tile for {lp}")


def kernel(x, meta_tokens, norm_mix_g, w_in, ret_decay_logit_fwd, ret_decay_logit_bwd, ret_norm_g,
           attn_q_norm_g, attn_k_norm_g, attn_out_norm_g, w_out, norm_ffn_g, router_group_w,
           router_group_b, router_expert_w, router_expert_b, expert_w_gate, expert_w_up, expert_w_down,
           norm_final_g):
    batch, n_tok, d = x.shape
    assert batch == 1 and norm_mix_g.shape[0] == 1
    lp = n_tok + PREFIX
    h = jnp.concatenate([jnp.zeros((PAD, d), x.dtype), meta_tokens.astype(x.dtype), x[0]], axis=0)

    cr, sr = _rope_tables(n_tok, RET_QK_DIM)
    ca, sa = _rope_tables(n_tok, ATTN_HEAD_DIM)
    qk, rest = _in_proj(h, norm_mix_g, w_in[0].astype(BF16), cr, sr, ca, sa,
                        attn_q_norm_g, attn_k_norm_g, tm=_row_tile(lp, (640, 128)))

    log_gamma = jnp.stack([jax.nn.log_sigmoid(ret_decay_logit_fwd[0].astype(F32)),
                           jax.nn.log_sigmoid(ret_decay_logit_bwd[0].astype(F32))])
    ret = _retention(log_gamma, qk, rest, ret_norm_g)
    att = _attention(rest, tq=_row_tile(lp, (320, 128, 64)), tk=_row_tile(n_tok, (512, 256, 128)))

    n_route = N_GROUPS + N_EXPERTS
    wr = jnp.pad(jnp.concatenate([router_group_w[0], router_expert_w[0]], axis=1).astype(F32),
                 ((0, 0), (0, LANES - n_route)))
    br = jnp.pad(jnp.concatenate([router_group_b[0], router_expert_b[0]]).astype(F32),
                 (0, LANES - n_route))[None]
    wr_hi = wr.astype(BF16)
    wr_lo = (wr - wr_hi.astype(F32)).astype(BF16)
    wr = jnp.concatenate([wr_hi, wr_hi, wr_lo], axis=0)
    h1, f, route, cnt = _mix_router(ret, att, attn_out_norm_g, w_out[0].astype(BF16), h, norm_ffn_g,
                                    wr, br, tm=_row_tile(lp, (320, 128)))

    expert = route[:, 0:TOP_K].astype(jnp.int32)
    rank = route[:, 4:4 + TOP_K].astype(jnp.int32)
    counts = cnt[0, N_GROUPS:n_route].astype(jnp.int32)
    padded = (counts + EXPERT_BLOCK - 1) // EXPERT_BLOCK * EXPERT_BLOCK
    pad_end = jnp.cumsum(padded)
    pad_start = pad_end - padded
    dest = (pad_start[expert] + rank).T.reshape(-1)
    n_rows = (-(-(lp * TOP_K) // EXPERT_BLOCK) + N_EXPERTS) * EXPERT_BLOCK
    sidx = jnp.full((n_rows,), -1, jnp.int32).at[dest].set(jnp.arange(TOP_K * lp, dtype=jnp.int32))
    groups = (counts + MOE_ITEM_ROWS - 1) // MOE_ITEM_ROWS
    g_end = jnp.cumsum(groups)
    g_start = g_end - groups
    n_items = N_EXPERTS + -(-(lp * TOP_K) // MOE_ITEM_ROWS)
    ii = jnp.arange(n_items, dtype=jnp.int32)
    valid = ii < g_end[-1]
    item_e = jnp.sum((g_end[None, :] <= jnp.minimum(ii, g_end[-1] - 1)[:, None]).astype(jnp.int32), axis=1)
    g_in = ii - g_start[item_e]
    item_row = jnp.where(valid, pad_start[item_e] + g_in * MOE_ITEM_ROWS, 0).astype(jnp.int32)
    item_rows = jnp.where(valid, jnp.minimum(MOE_ITEM_ROWS, counts[item_e] - g_in * MOE_ITEM_ROWS),
                          0).astype(jnp.int32)

    y = _moe_experts(item_e, item_row, item_rows, sidx, f, expert_w_gate[0], expert_w_up[0],
                     expert_w_down[0], chunk=MOE_CHUNK)
    out = _combine(h1, route, y, norm_final_g[None], tm=PREFIX)
    return out[None]
```

```python
import functools

import jax
import jax.numpy as jnp
from jax import lax
from jax.experimental import pallas as pl
from jax.experimental.pallas import tpu as pltpu

F32 = jnp.float32
BF16 = jnp.bfloat16

N_META = 16
CHUNK = 128
PREFIX = CHUNK
PAD = PREFIX - N_META
GRID_W = 64
ROPE_THETA = 10000.0
NORM_EPS = 1e-6
GROUPNORM_EPS = 1e-5

RET_HEADS = 8
RET_QK_DIM = 64
RET_V_DIM = 128
RET_UNROLL = 5
ATTN_HEADS = 8
ATTN_KV_HEADS = 2
ATTN_HEAD_DIM = 128
ATTN_GROUP = ATTN_HEADS // ATTN_KV_HEADS
ATTN_STRIP = 256
ATTN_ONES_ROWS = 16
LOG2_E = 1.4426950408889634

N_GROUPS = 8
EXPERTS_PER_GROUP = 8
N_EXPERTS = N_GROUPS * EXPERTS_PER_GROUP
TOP_K = 2
EXPERT_BLOCK = 128
MOE_ITEM_ROWS = 512
MOE_CHUNK = 512
MOE_ROW_STEP = 64
DMA_GROUP = 8

LANES = 128
PROJ_TILE = 512
VMEM_LIMIT = 56 * 1024 * 1024
NEG_BIG = -1e30


def _params(sem, vmem=VMEM_LIMIT):
    return pltpu.CompilerParams(dimension_semantics=sem, vmem_limit_bytes=vmem)


def _swap_halves(x, hw):
    w = x.shape[-1]
    lane = lax.broadcasted_iota(jnp.int32, x.shape, 1)
    first = (lane & hw) == 0
    return jnp.where(first, pltpu.roll(x, w - hw, 1), pltpu.roll(x, hw, 1))


def _rope_tables(n_tok, head_dim):
    half = head_dim // 2
    freqs = ROPE_THETA ** (-jnp.arange(0, half, 2, dtype=F32) / half)
    rows = n_tok // GRID_W
    ang_r = jnp.arange(rows, dtype=F32)[:, None] * freqs[None, :]
    ang_c = jnp.arange(GRID_W, dtype=F32)[:, None] * freqs[None, :]

    def per_token(tab_r, tab_c, prefix_value):
        pre = jnp.full((PREFIX, tab_r.shape[1]), prefix_value, F32)
        return (jnp.concatenate([pre, jnp.repeat(tab_r, GRID_W, axis=0)], axis=0),
                jnp.concatenate([pre, jnp.tile(tab_c, (rows, 1))], axis=0))

    cos_r, cos_c = per_token(jnp.cos(ang_r), jnp.cos(ang_c), 1.0)
    sin_r, sin_c = per_token(jnp.sin(ang_r), jnp.sin(ang_c), 0.0)
    cos = jnp.concatenate([cos_r, cos_r, cos_c, cos_c], axis=-1)
    sin = jnp.concatenate([-sin_r, sin_r, -sin_c, sin_c], axis=-1)
    reps = LANES // head_dim
    return jnp.tile(cos, (1, reps)), jnp.tile(sin, (1, reps))


def _in_proj_kernel(x_ref, g_ref, w_ref, cr_ref, sr_ref, ca_ref, sa_ref, qg_ref, kg_ref,
                    oqk_ref, orest_ref, a_scr, *, tm):
    i = pl.program_id(0)
    x = x_ref[...]
    ms = jnp.mean(x * x, axis=-1, keepdims=True)
    a_scr[...] = ((x * lax.rsqrt(ms + NORM_EPS)) * g_ref[...]).astype(BF16)
    n_lane_groups = PROJ_TILE // LANES
    ret_tiles = 2
    plain_tiles = (2 * RET_HEADS * RET_V_DIM) // PROJ_TILE
    q_tiles = (ATTN_HEADS * ATTN_HEAD_DIM) // PROJ_TILE

    def tiled(ref):
        return jnp.concatenate([ref[...]] * n_lane_groups, axis=1)

    def qk_norm_rope(xg, gain_ref):
        ms_h = jnp.mean(xg * xg, axis=-1, keepdims=True)
        xn = (xg * lax.rsqrt(ms_h + NORM_EPS)) * gain_ref[...]
        return xn * ca_ref[...] + _swap_halves(xn, ATTN_HEAD_DIM // 4) * sa_ref[...]

    row = i * tm + lax.broadcasted_iota(jnp.int32, (tm, 1), 0)
    k_scale = jnp.where(row >= PAD, RET_QK_DIM ** -0.5, 0.0).astype(F32)
    lo = lax.broadcasted_iota(jnp.int32, (tm, LANES), 1) < RET_QK_DIM

    for j in range(w_ref.shape[1] // PROJ_TILE):
        acc = jnp.dot(a_scr[...], w_ref[:, j * PROJ_TILE:(j + 1) * PROJ_TILE],
                      preferred_element_type=F32)
        if j < ret_tiles:
            y = acc * tiled(cr_ref) + _swap_halves(acc, RET_QK_DIM // 4) * tiled(sr_ref)
            if j == 1:
                y = y * k_scale
            pieces = []
            for c in range(n_lane_groups):
                xg = y[:, c * LANES:(c + 1) * LANES]
                xr = pltpu.roll(xg, RET_QK_DIM, 1)
                pieces.append(jnp.where(lo, xg, xr))
                pieces.append(jnp.where(lo, xr, xg))
            oqk_ref[:, 2 * j * PROJ_TILE:2 * (j + 1) * PROJ_TILE] = (
                jnp.concatenate(pieces, axis=1).astype(BF16))
            continue
        jr = j - ret_tiles
        if jr < plain_tiles:
            out = acc
        elif jr < plain_tiles + q_tiles:
            out = jnp.concatenate(
                [qk_norm_rope(acc[:, c * LANES:(c + 1) * LANES], qg_ref) * (LOG2_E * ATTN_HEAD_DIM ** -0.5)
                 for c in range(n_lane_groups)], axis=1)
        else:
            pieces = [qk_norm_rope(acc[:, c * LANES:(c + 1) * LANES], kg_ref)
                      for c in range(ATTN_KV_HEADS)]
            pieces.append(acc[:, ATTN_KV_HEADS * LANES:])
            out = jnp.concatenate(pieces, axis=1)
        orest_ref[:, jr * PROJ_TILE:(jr + 1) * PROJ_TILE] = out.astype(BF16)


def _in_proj(h, g, w_bf, cr, sr, ca, sa, qg, kg, *, tm):
    lp, d = h.shape
    n_out = w_bf.shape[1]
    assert lp % tm == 0 and n_out == 9 * PROJ_TILE
    qk_w = 2 * 2 * RET_HEADS * RET_QK_DIM
    rest_w = n_out - 2 * RET_HEADS * RET_QK_DIM
    row_tab = pl.BlockSpec((tm, LANES), lambda i: (i, 0))
    vec = lambda n: pl.BlockSpec((1, n), lambda i: (0, 0))
    return pl.pallas_call(
        functools.partial(_in_proj_kernel, tm=tm),
        grid=(lp // tm,),
        in_specs=[
            pl.BlockSpec((tm, d), lambda i: (i, 0)),
            vec(d),
            pl.BlockSpec((d, n_out), lambda i: (0, 0), pipeline_mode=pl.Buffered(1)),
            row_tab, row_tab, row_tab, row_tab,
            vec(LANES), vec(LANES),
        ],
        out_specs=[
            pl.BlockSpec((tm, qk_w), lambda i: (i, 0)),
            pl.BlockSpec((tm, rest_w), lambda i: (i, 0)),
        ],
        out_shape=[jax.ShapeDtypeStruct((lp, qk_w), BF16), jax.ShapeDtypeStruct((lp, rest_w), BF16)],
        scratch_shapes=[pltpu.VMEM((tm, d), BF16)],
        compiler_params=_params(("arbitrary",)),
        name="in_proj",
    )(h, g, w_bf, cr, sr, ca, sa, qg, kg)


def _retention_kernel(lg_ref, q_ref, k_ref, v_ref, g_ref, gn_ref, o_ref, st_ref, dk_ref, dq_ref, dm_ref,
                      *, n_chunks):
    h = pl.program_id(0)
    lgf = lg_ref[0, h]
    lgb = lg_ref[1, h]
    half = CHUNK // 2
    row = lax.broadcasted_iota(jnp.int32, (CHUNK, CHUNK), 0).astype(F32)
    lane = lax.broadcasted_iota(jnp.int32, (CHUNK, CHUNK), 1).astype(F32)
    lane_lo = lane < half
    dk_ref[...] = jnp.exp(jnp.where(lane_lo, lgf * (CHUNK - 1.0 - row), lgb * row))
    dq_ref[...] = jnp.exp(jnp.where(lane_lo, lgf * (row + 1.0), lgb * (CHUNK - row)))
    diff = row - lane
    dm_ref[...] = 0.5 * jnp.where(diff >= 0, jnp.exp(lgf * jnp.maximum(diff, 0.0)),
                                  jnp.exp(lgb * jnp.maximum(-diff, 0.0)))
    cf = jnp.exp(jnp.full((half, CHUNK), lgf * CHUNK, F32))
    cb = jnp.exp(jnp.full((half, CHUNK), lgb * CHUNK, F32))

    def rows(n):
        return pl.ds(pl.multiple_of(n * CHUNK, CHUNK), CHUNK)

    def updates(n, c):
        kd = k_ref[rows(n), :].astype(F32) * dk_ref[...]
        st_ref[n] = jnp.dot(kd.T.astype(BF16), v_ref[rows(n), :], preferred_element_type=F32)
        return c

    unroll = max(u for u in range(1, RET_UNROLL + 1) if n_chunks % u == 0)

    def unrolled(fn):
        def body(t, c):
            for u in range(unroll):
                fn(t * unroll + u, c)
            return c
        return body

    lax.fori_loop(0, n_chunks // unroll, unrolled(updates), 0)

    def scan_fwd(n, s):
        u = st_ref[n, 0:half, :]
        st_ref[n, 0:half, :] = s
        return cf * s + u

    lax.fori_loop(0, n_chunks, scan_fwd, jnp.zeros((half, CHUNK), F32))

    def scan_bwd(t, s):
        n = n_chunks - 1 - t
        u = st_ref[n, half:CHUNK, :]
        st_ref[n, half:CHUNK, :] = s
        return cb * s + u

    lax.fori_loop(0, n_chunks, scan_bwd, jnp.zeros((half, CHUNK), F32))

    def outputs(n, c):
        q = q_ref[rows(n), :]
        k = k_ref[rows(n), :]
        v = v_ref[rows(n), :]
        s2 = lax.dot_general(q, k, (((1,), (1,)), ((), ())), preferred_element_type=F32)
        intra = jnp.dot((s2 * dm_ref[...]).astype(BF16), v, preferred_element_type=F32)
        qd = (q.astype(F32) * dq_ref[...]).astype(BF16)
        cross = jnp.dot(qd, st_ref[n].astype(BF16), preferred_element_type=F32)
        o = intra + cross
        mu = jnp.mean(o, axis=-1, keepdims=True)
        dlt = o - mu
        var = jnp.mean(dlt * dlt, axis=-1, keepdims=True)
        on = dlt * lax.rsqrt(var + GROUPNORM_EPS)
        gate = g_ref[rows(n), :].astype(F32)
        o_ref[rows(n), :] = ((gate * jax.nn.sigmoid(gate)) * (on * gn_ref[...])).astype(BF16)
        return c

    lax.fori_loop(0, n_chunks // unroll, unrolled(outputs), 0)


def _retention(log_gamma, qk, rest, gn_g):
    lp = qk.shape[0]
    n_chunks = lp // CHUNK
    col = lambda off: pl.BlockSpec((lp, LANES), lambda h, lg: (0, h + off))
    return pl.pallas_call(
        functools.partial(_retention_kernel, n_chunks=n_chunks),
        grid_spec=pltpu.PrefetchScalarGridSpec(
            num_scalar_prefetch=1,
            grid=(RET_HEADS,),
            in_specs=[col(0), col(RET_HEADS), col(0), col(RET_HEADS),
                      pl.BlockSpec((1, LANES), lambda h, lg: (0, h))],
            out_specs=pl.BlockSpec((lp, LANES), lambda h, lg: (0, h)),
            scratch_shapes=[pltpu.VMEM((n_chunks, CHUNK, CHUNK), F32),
                            pltpu.VMEM((CHUNK, CHUNK), F32),
                            pltpu.VMEM((CHUNK, CHUNK), F32),
                            pltpu.VMEM((CHUNK, CHUNK), F32)],
        ),
        out_shape=jax.ShapeDtypeStruct((lp, RET_HEADS * RET_V_DIM), BF16),
        compiler_params=_params(("arbitrary",)),
        name="retention",
    )(log_gamma, qk, qk, rest, rest, gn_g)


def _attention_kernel(q_ref, k_ref, vt_ref, km_ref, vtm_ref, o_ref, s_scr, mx_scr, m_scr, acc_scr,
                      *, tq, tk, n_kc):
    q_all = jnp.concatenate([q_ref[:, g * LANES:(g + 1) * LANES] for g in range(ATTN_GROUP)], axis=0)
    n_strips = (ATTN_GROUP * tq) // ATTN_STRIP
    q_strips = [q_all[s * ATTN_STRIP:(s + 1) * ATTN_STRIP, :] for s in range(n_strips)]

    def scores(k, s):
        return lax.dot_general(k, q_strips[s], (((1,), (1,)), ((), ())), preferred_element_type=F32)

    def key_rows(c):
        return pl.ds(pl.multiple_of(PREFIX + c * tk, LANES), tk)

    def score_stage(c, slot):
        k = k_ref[key_rows(c), :]
        for s in range(n_strips):
            st = scores(k, s)
            s_scr[slot, s] = st
            mx_scr[slot, s] = jnp.max(st, axis=0, keepdims=True)

    def value_stage(c, slot):
        vt = vt_ref[0, :, key_rows(c)]
        for s in range(n_strips):
            m_old = m_scr[s]
            m_new = jnp.maximum(m_old, mx_scr[slot, s])
            p = jnp.exp2(s_scr[slot, s] - m_new).astype(BF16)
            acc_scr[s] = jnp.exp2(m_old - m_new) * acc_scr[s] + jnp.dot(vt, p, preferred_element_type=F32)
            m_scr[s] = m_new

    for s in range(n_strips):
        st = scores(km_ref[0], s)
        m0 = jnp.max(st, axis=0, keepdims=True)
        m_scr[s] = m0
        acc_scr[s] = jnp.dot(vtm_ref[0], jnp.exp2(st - m0).astype(BF16), preferred_element_type=F32)

    score_stage(0, 0)

    def body(i, carry):
        c = 2 * i
        score_stage(c + 1, 1)
        value_stage(c, 0)
        score_stage(c + 2, 0)
        value_stage(c + 1, 1)
        return carry

    lax.fori_loop(0, n_kc // 2 - 1, body, 0)
    score_stage(n_kc - 1, 1)
    value_stage(n_kc - 2, 0)
    value_stage(n_kc - 1, 1)
    outs = []
    for s in range(n_strips):
        acc = acc_scr[s]
        outs.append((acc[0:ATTN_HEAD_DIM, :] / acc[ATTN_HEAD_DIM:ATTN_HEAD_DIM + 1, :]).T)
    o = jnp.concatenate(outs, axis=0)
    for g in range(ATTN_GROUP):
        o_ref[:, g * LANES:(g + 1) * LANES] = o[g * tq:(g + 1) * tq, :]


def _attention(rest, *, tq, tk):
    lp = rest.shape[0]
    n_tok = lp - PREFIX
    assert lp % tq == 0 and n_tok % (2 * tk) == 0 and (ATTN_GROUP * tq) % ATTN_STRIP == 0
    gw = ATTN_GROUP * ATTN_HEAD_DIM
    q_off = (2 * RET_HEADS * RET_V_DIM) // gw
    k_col = 2 * RET_HEADS * RET_V_DIM + ATTN_HEADS * ATTN_HEAD_DIM
    k_off = k_col // LANES
    kv_w = ATTN_KV_HEADS * ATTN_HEAD_DIM
    vt = rest[:, k_col + kv_w:k_col + 2 * kv_w].T.reshape(ATTN_KV_HEADS, ATTN_HEAD_DIM, lp)
    vt = jnp.concatenate([vt, jnp.ones((ATTN_KV_HEADS, ATTN_ONES_ROWS, lp), BF16)], axis=1)
    vt_meta = vt[:, :, PAD:PREFIX]
    k_meta = rest[PAD:PREFIX, k_col:k_col + kv_w].reshape(N_META, ATTN_KV_HEADS, ATTN_HEAD_DIM)
    k_meta = jnp.swapaxes(k_meta, 0, 1)
    n_strips = (ATTN_GROUP * tq) // ATTN_STRIP
    vt_rows = ATTN_HEAD_DIM + ATTN_ONES_ROWS
    return pl.pallas_call(
        functools.partial(_attention_kernel, tq=tq, tk=tk, n_kc=n_tok // tk),
        grid=(ATTN_KV_HEADS, lp // tq),
        in_specs=[pl.BlockSpec((tq, gw), lambda kv, i: (i, q_off + kv)),
                  pl.BlockSpec((lp, LANES), lambda kv, i: (0, k_off + kv)),
                  pl.BlockSpec((1, vt_rows, lp), lambda kv, i: (kv, 0, 0)),
                  pl.BlockSpec((1, N_META, ATTN_HEAD_DIM), lambda kv, i: (kv, 0, 0)),
                  pl.BlockSpec((1, vt_rows, N_META), lambda kv, i: (kv, 0, 0))],
        out_specs=pl.BlockSpec((tq, gw), lambda kv, i: (i, kv)),
        out_shape=jax.ShapeDtypeStruct((lp, ATTN_HEADS * ATTN_HEAD_DIM), F32),
        scratch_shapes=[pltpu.VMEM((2, n_strips, tk, ATTN_STRIP), F32),
                        pltpu.VMEM((2, n_strips, 1, ATTN_STRIP), F32),
                        pltpu.VMEM((n_strips, 1, ATTN_STRIP), F32),
                        pltpu.VMEM((n_strips, vt_rows, ATTN_STRIP), F32)],
        compiler_params=_params(("arbitrary", "arbitrary")),
        name="attention",
    )(rest, rest, vt, k_meta, vt_meta)


def _mix_router_kernel(ret_ref, att_ref, ang_ref, wo_ref, h_ref, fg_ref, wr_ref, br_ref,
                       h1_ref, f_ref, route_ref, cnt_ref, carry_scr, *, tm):
    i = pl.program_id(0)

    @pl.when(i == 0)
    def _():
        carry_scr[...] = jnp.zeros(carry_scr.shape, F32)

    att = att_ref[...]
    ms = jnp.mean(att * att, axis=-1, keepdims=True)
    att_n = ((att * lax.rsqrt(ms + NORM_EPS)) * ang_ref[...]).astype(BF16)
    mix_in = jnp.concatenate([ret_ref[...], att_n], axis=1)
    h1 = h_ref[...] + jnp.dot(mix_in, wo_ref[...], preferred_element_type=F32)
    h1_ref[...] = h1
    ms1 = jnp.mean(h1 * h1, axis=-1, keepdims=True)
    f = (h1 * lax.rsqrt(ms1 + NORM_EPS)) * fg_ref[...]
    f_ref[...] = f

    f_hi = f.astype(BF16)
    f_lo = (f - f_hi.astype(F32)).astype(BF16)
    logits = jnp.dot(jnp.concatenate([f_hi, f_lo, f_hi], axis=1), wr_ref[...],
                     preferred_element_type=F32) + br_ref[...]
    lane = lax.broadcasted_iota(jnp.int32, (tm, LANES), 1).astype(F32)
    far = float(LANES)
    g_logit = jnp.where(lane < N_GROUPS, logits, NEG_BIG)
    g_max = jnp.max(g_logit, axis=-1, keepdims=True)
    g_w = 1.0 / jnp.sum(jnp.exp(g_logit - g_max), axis=-1, keepdims=True)
    g_idx = jnp.min(jnp.where(g_logit == g_max, lane, far), axis=-1, keepdims=True)
    first = N_GROUPS + g_idx * EXPERTS_PER_GROUP
    e_logit = jnp.where((lane >= first) & (lane < first + EXPERTS_PER_GROUP), logits, NEG_BIG)
    v1 = jnp.max(e_logit, axis=-1, keepdims=True)
    i1 = jnp.min(jnp.where(e_logit == v1, lane, far), axis=-1, keepdims=True)
    e_rest = jnp.where(lane == i1, NEG_BIG, e_logit)
    v2 = jnp.max(e_rest, axis=-1, keepdims=True)
    i2 = jnp.min(jnp.where(e_rest == v2, lane, far), axis=-1, keepdims=True)
    e2 = jnp.exp(v2 - v1)
    w1 = g_w / (1.0 + e2)
    w2 = g_w * e2 / (1.0 + e2)

    oh1 = (lane == i1).astype(F32)
    oh2 = (lane == i2).astype(F32)
    oh = oh1 + oh2
    r_i = lax.broadcasted_iota(jnp.int32, (tm, tm), 0)
    c_i = lax.broadcasted_iota(jnp.int32, (tm, tm), 1)
    lower = (c_i < r_i).astype(BF16)
    before = jnp.dot(lower, oh.astype(BF16), preferred_element_type=F32) + carry_scr[...]
    rank1 = jnp.sum(before * oh1, axis=-1, keepdims=True)
    rank2 = jnp.sum(before * oh2, axis=-1, keepdims=True)
    carry_scr[...] = carry_scr[...] + jnp.sum(oh, axis=0, keepdims=True)
    cnt_ref[...] = carry_scr[...]

    route = jnp.where(lane == 0, i1 - N_GROUPS, 0.0)
    route = jnp.where(lane == 1, i2 - N_GROUPS, route)
    route = jnp.where(lane == 2, w1, route)
    route = jnp.where(lane == 3, w2, route)
    route = jnp.where(lane == 4, rank1, route)
    route = jnp.where(lane == 5, rank2, route)
    route_ref[...] = route


def _mix_router(ret, att, ang, wo_bf, h, fg, wr, br, *, tm):
    lp, d = h.shape
    assert lp % tm == 0
    rowblk = lambda n: pl.BlockSpec((tm, n), lambda i: (i, 0))
    vec = lambda n: pl.BlockSpec((1, n), lambda i: (0, 0))
    full = lambda a: pl.BlockSpec(a.shape, lambda i: (0, 0))
    return pl.pallas_call(
        functools.partial(_mix_router_kernel, tm=tm),
        grid=(lp // tm,),
        in_specs=[rowblk(ret.shape[1]), rowblk(att.shape[1]), vec(att.shape[1]), full(wo_bf),
                  rowblk(d), vec(d), full(wr), vec(LANES)],
        out_specs=[rowblk(d), rowblk(d), rowblk(LANES), vec(LANES)],
        out_shape=[jax.ShapeDtypeStruct((lp, d), F32), jax.ShapeDtypeStruct((lp, d), F32),
                   jax.ShapeDtypeStruct((lp, LANES), F32), jax.ShapeDtypeStruct((1, LANES), F32)],
        scratch_shapes=[pltpu.VMEM((1, LANES), F32)],
        compiler_params=_params(("arbitrary",)),
        name="mix_router",
    )(ret, att, ang, wo_bf, h, fg, wr, br)


def _row_copy(src_ref, src_row, dst_ref, dst_row, sem):
    return pltpu.make_async_copy(src_ref.at[pl.ds(src_row, 1), :], dst_ref.at[pl.ds(dst_row, 1), :], sem)


def _moe_kernel(ie_ref, irow_ref, irows_ref, sidx_ref, f_ref, wg_hbm, wu_hbm, wd_hbm, o_ref,
                x_scr, y_scr, wg_buf, wu_buf, wd_buf, gsem, ssem, wsem, *, n_items, n_chunks, lp, chunk):
    i = pl.program_id(0)
    c = pl.program_id(1)
    n_rows = irows_ref[i]
    slot = i % 2
    w_slot = (i * n_chunks + c) % 2

    def weights(item, cc, ws, start):
        e = ie_ref[item]
        half = chunk // 2
        col0 = pl.multiple_of(cc * chunk, chunk)
        copies = (
            (wg_hbm.at[e, :, pl.ds(col0, chunk)], wg_buf.at[ws]),
            (wu_hbm.at[e, :, pl.ds(col0, chunk)], wu_buf.at[ws]),
            (wd_hbm.at[e, pl.ds(col0, half), :], wd_buf.at[ws, 0:half, :]),
            (wd_hbm.at[e, pl.ds(col0 + half, half), :], wd_buf.at[ws, half:chunk, :]),
        )

        @pl.when(irows_ref[item] > 0)
        def _():
            for n, (src, dst) in enumerate(copies):
                cp = pltpu.make_async_copy(src, dst, wsem.at[ws, n])
                cp.start(priority=n % 2) if start else cp.wait()

    @pl.when((i == 0) & (c == 0))
    def _():
        weights(0, 0, 0, True)

    @pl.when(c + 1 < n_chunks)
    def _():
        weights(i, c + 1, 1 - w_slot, True)

    @pl.when((c + 1 == n_chunks) & (i + 1 < n_items))
    def _():
        weights(i + 1, 0, 1 - w_slot, True)

    weights(i, c, w_slot, False)

    def group_wait(src_ref, dst_ref, sem):
        pltpu.make_async_copy(src_ref.at[pl.ds(0, DMA_GROUP), :], dst_ref.at[pl.ds(0, DMA_GROUP), :],
                              sem).wait()

    def gather(item, dst_slot, start):
        row0 = irow_ref[item]
        dst = x_scr.at[dst_slot]
        sem = gsem.at[dst_slot]

        def group(g, carry):
            if start:
                for u in range(DMA_GROUP):
                    r = g * DMA_GROUP + u
                    a = sidx_ref[row0 + r]
                    tok = jnp.where(a >= lp, a - lp, jnp.maximum(a, 0))
                    _row_copy(f_ref, tok, dst, r, sem).start(priority=u % 2)
            else:
                group_wait(f_ref, dst, sem)
            return carry

        lax.fori_loop(0, (irows_ref[item] + DMA_GROUP - 1) // DMA_GROUP, group, 0)

    def scatter(item, start):
        row0 = irow_ref[item]
        n = irows_ref[item]
        n_groups = n // DMA_GROUP

        def group(g, carry):
            if start:
                for u in range(DMA_GROUP):
                    r = g * DMA_GROUP + u
                    _row_copy(y_scr, r, o_ref, sidx_ref[row0 + r], ssem).start(priority=u % 2)
            else:
                group_wait(y_scr, o_ref, ssem)
            return carry

        lax.fori_loop(0, n_groups, group, 0)

        def single(r, carry):
            cp = _row_copy(y_scr, r, o_ref, sidx_ref[row0 + r], ssem)
            cp.start() if start else cp.wait()
            return carry

        lax.fori_loop(n_groups * DMA_GROUP, n, single, 0)

    @pl.when(c == 0)
    def _():
        @pl.when(i == 0)
        def _():
            x_scr[...] = jnp.zeros(x_scr.shape, F32)
            y_scr[...] = jnp.zeros(y_scr.shape, F32)
            gather(0, 0, True)

        @pl.when(i + 1 < n_items)
        def _():
            gather(i + 1, 1 - slot, True)

        gather(i, slot, False)

        @pl.when(i > 0)
        def _():
            scatter(i - 1, False)

    n_blocks = (n_rows + MOE_ROW_STEP - 1) // MOE_ROW_STEP
    for v in range(1, MOE_ITEM_ROWS // MOE_ROW_STEP + 1):
        @pl.when(n_blocks == v)
        def _(m=v * MOE_ROW_STEP):
            x = x_scr[slot, 0:m, :].astype(BF16)
            gate = jnp.dot(x, wg_buf[w_slot].astype(BF16), preferred_element_type=F32)
            up = jnp.dot(x, wu_buf[w_slot].astype(BF16), preferred_element_type=F32)
            hid = ((gate * jax.nn.sigmoid(gate)) * up).astype(BF16)
            y = jnp.dot(hid, wd_buf[w_slot].astype(BF16), preferred_element_type=F32)
            y_scr[0:m, :] = jnp.where(c > 0, y_scr[0:m, :], 0.0) + y

    @pl.when(c == n_chunks - 1)
    def _():
        scatter(i, True)

        @pl.when(i == n_items - 1)
        def _():
            scatter(i, False)


def _moe_experts(item_e, item_row, item_rows, sidx, f, w_gate, w_up, w_down, *, chunk):
    lp, d = f.shape
    d_e = w_gate.shape[2]
    n_items = item_e.shape[0]
    n_chunks = d_e // chunk
    return pl.pallas_call(
        functools.partial(_moe_kernel, n_items=n_items, n_chunks=n_chunks, lp=lp, chunk=chunk),
        grid_spec=pltpu.PrefetchScalarGridSpec(
            num_scalar_prefetch=4,
            grid=(n_items, n_chunks),
            in_specs=[pl.BlockSpec(memory_space=pl.ANY)] * 4,
            out_specs=pl.BlockSpec(memory_space=pl.ANY),
            scratch_shapes=[pltpu.VMEM((2, MOE_ITEM_ROWS, d), F32),
                            pltpu.VMEM((MOE_ITEM_ROWS, d), F32),
                            pltpu.VMEM((2, d, chunk), F32),
                            pltpu.VMEM((2, d, chunk), F32),
                            pltpu.VMEM((2, chunk, d), F32),
                            pltpu.SemaphoreType.DMA((2,)),
                            pltpu.SemaphoreType.DMA(()),
                            pltpu.SemaphoreType.DMA((2, 4))],
        ),
        out_shape=jax.ShapeDtypeStruct((TOP_K * lp, d), F32),
        compiler_params=_params(("arbitrary", "arbitrary")),
        name="moe_experts",
    )(item_e, item_row, item_rows, sidx, f, w_gate, w_up, w_down)


def _combine_kernel(h1_ref, route_ref, y0_ref, y1_ref, g_ref, o_ref):
    route = route_ref[...]
    h2 = h1_ref[...] + (route[:, 2:3] * y0_ref[0] + route[:, 3:4] * y1_ref[0])
    ms = jnp.mean(h2 * h2, axis=-1, keepdims=True)
    o_ref[...] = (h2 * lax.rsqrt(ms + NORM_EPS)) * g_ref[...]


def _combine(h1, route, y, g, *, tm):
    lp, d = h1.shape
    assert PREFIX % tm == 0
    skip = PREFIX // tm
    n_tok = lp - PREFIX
    y3 = y.reshape(TOP_K, lp, d)
    return pl.pallas_call(
        _combine_kernel,
        grid=(n_tok // tm,),
        in_specs=[pl.BlockSpec((tm, d), lambda i: (i + skip, 0)),
                  pl.BlockSpec((tm, LANES), lambda i: (i + skip, 0)),
                  pl.BlockSpec((1, tm, d), lambda i: (0, i + skip, 0)),
                  pl.BlockSpec((1, tm, d), lambda i: (1, i + skip, 0)),
                  pl.BlockSpec((1, d), lambda i: (0, 0))],
        out_specs=pl.BlockSpec((tm, d), lambda i: (i, 0)),
        out_shape=jax.ShapeDtypeStruct((n_tok, d), F32),
        compiler_params=_params(("arbitrary",)),
        name="combine",
    )(h1, route, y3, y3, g)


def _row_tile(lp, candidates):
    for t in candidates:
        if lp % t == 0:
            return t
    raise ValueError(f"no row tile for {lp}")


def kernel(x, meta_tokens, norm_mix_g, w_in, ret_decay_logit_fwd, ret_decay_logit_bwd, ret_norm_g,
           attn_q_norm_g, attn_k_norm_g, attn_out_norm_g, w_out, norm_ffn_g, router_group_w,
           router_group_b, router_expert_w, router_expert_b, expert_w_gate, expert_w_up, expert_w_down,
           norm_final_g):
    batch, n_tok, d = x.shape
    assert batch == 1 and norm_mix_g.shape[0] == 1
    lp = n_tok + PREFIX
    h = jnp.concatenate([jnp.zeros((PAD, d), x.dtype), meta_tokens.astype(x.dtype), x[0]], axis=0)

    cr, sr = _rope_tables(n_tok, RET_QK_DIM)
    ca, sa = _rope_tables(n_tok, ATTN_HEAD_DIM)
    qk, rest = _in_proj(h, norm_mix_g, w_in[0].astype(BF16), cr, sr, ca, sa,
                        attn_q_norm_g, attn_k_norm_g, tm=_row_tile(lp, (320, 128)))

    log_gamma = jnp.stack([jax.nn.log_sigmoid(ret_decay_logit_fwd[0].astype(F32)),
                           jax.nn.log_sigmoid(ret_decay_logit_bwd[0].astype(F32))])
    ret = _retention(log_gamma, qk, rest, ret_norm_g)
    att = _attention(rest, tq=_row_tile(lp, (320, 128, 64)), tk=_row_tile(n_tok, (512, 256, 128)))

    n_route = N_GROUPS + N_EXPERTS
    wr = jnp.pad(jnp.concatenate([router_group_w[0], router_expert_w[0]], axis=1).astype(F32),
                 ((0, 0), (0, LANES - n_route)))
    br = jnp.pad(jnp.concatenate([router_group_b[0], router_expert_b[0]]).astype(F32),
                 (0, LANES - n_route))[None]
    wr_hi = wr.astype(BF16)
    wr_lo = (wr - wr_hi.astype(F32)).astype(BF16)
    wr = jnp.concatenate([wr_hi, wr_hi, wr_lo], axis=0)
    h1, f, route, cnt = _mix_router(ret, att, attn_out_norm_g, w_out[0].astype(BF16), h, norm_ffn_g,
                                    wr, br, tm=_row_tile(lp, (320, 128)))

    expert = route[:, 0:TOP_K].astype(jnp.int32)
    rank = route[:, 4:4 + TOP_K].astype(jnp.int32)
    counts = cnt[0, N_GROUPS:n_route].astype(jnp.int32)
    padded = (counts + EXPERT_BLOCK - 1) // EXPERT_BLOCK * EXPERT_BLOCK
    pad_end = jnp.cumsum(padded)
    pad_start = pad_end - padded
    is_expert = expert.T[:, :, None] == jnp.arange(N_EXPERTS, dtype=jnp.int32)
    seg_start = jnp.sum(jnp.where(is_expert, pad_start, 0), axis=-1)
    dest = (seg_start + rank.T).reshape(-1)
    n_rows = (-(-(lp * TOP_K) // EXPERT_BLOCK) + N_EXPERTS) * EXPERT_BLOCK
    sidx = jnp.full((n_rows,), -1, jnp.int32).at[dest].set(jnp.arange(TOP_K * lp, dtype=jnp.int32))
    groups = (counts + MOE_ITEM_ROWS - 1) // MOE_ITEM_ROWS
    g_end = jnp.cumsum(groups)
    g_start = g_end - groups
    n_items = N_EXPERTS + -(-(lp * TOP_K) // MOE_ITEM_ROWS)
    ii = jnp.arange(n_items, dtype=jnp.int32)
    valid = ii < g_end[-1]
    item_e = jnp.sum((g_end[None, :] <= jnp.minimum(ii, g_end[-1] - 1)[:, None]).astype(jnp.int32), axis=1)
    g_in = ii - g_start[item_e]
    item_row = jnp.where(valid, pad_start[item_e] + g_in * MOE_ITEM_ROWS, 0).astype(jnp.int32)
    item_rows = jnp.where(valid, jnp.minimum(MOE_ITEM_ROWS, counts[item_e] - g_in * MOE_ITEM_ROWS),
                          0).astype(jnp.int32)

    y = _moe_experts(item_e, item_row, item_rows, sidx, f, expert_w_gate[0], expert_w_up[0],
                     expert_w_down[0], chunk=MOE_CHUNK)
    out = _combine(h1, route, y, norm_final_g[None], tm=PREFIX)
    return out[None]
```

```python
import functools

import jax
import jax.numpy as jnp
from jax import lax
from jax.experimental import pallas as pl
from jax.experimental.pallas import tpu as pltpu

F32 = jnp.float32
BF16 = jnp.bfloat16

N_META = 16
CHUNK = 128
PREFIX = CHUNK
PAD = PREFIX - N_META
GRID_W = 64
ROPE_THETA = 10000.0
NORM_EPS = 1e-6
GROUPNORM_EPS = 1e-5

RET_HEADS = 8
RET_QK_DIM = 64
RET_V_DIM = 128
RET_UNROLL = 13
ATTN_HEADS = 8
ATTN_KV_HEADS = 2
ATTN_HEAD_DIM = 128
ATTN_GROUP = ATTN_HEADS // ATTN_KV_HEADS
ATTN_STRIP = 256
ATTN_ONES_ROWS = 16
LOG2_E = 1.4426950408889634

N_GROUPS = 8
EXPERTS_PER_GROUP = 8
N_EXPERTS = N_GROUPS * EXPERTS_PER_GROUP
TOP_K = 2
EXPERT_BLOCK = 128
MOE_ITEM_ROWS = 512
MOE_CHUNK = 512
MOE_ROW_STEP = 64
DMA_GROUP = 8

LANES = 128
PROJ_TILE = 512
MIX_SUB_BLOCKS = 2
VMEM_LIMIT = 56 * 1024 * 1024
NEG_BIG = -1e30


def _params(sem, vmem=VMEM_LIMIT):
    return pltpu.CompilerParams(dimension_semantics=sem, vmem_limit_bytes=vmem)


def _swap_halves(x, hw):
    w = x.shape[-1]
    lane = lax.broadcasted_iota(jnp.int32, x.shape, 1)
    first = (lane & hw) == 0
    return jnp.where(first, pltpu.roll(x, w - hw, 1), pltpu.roll(x, hw, 1))


def _rope_tables(n_tok, head_dim):
    half = head_dim // 2
    freqs = ROPE_THETA ** (-jnp.arange(0, half, 2, dtype=F32) / half)
    rows = n_tok // GRID_W
    ang_r = jnp.arange(rows, dtype=F32)[:, None] * freqs[None, :]
    ang_c = jnp.arange(GRID_W, dtype=F32)[:, None] * freqs[None, :]

    def per_token(tab_r, tab_c, prefix_value):
        pre = jnp.full((PREFIX, tab_r.shape[1]), prefix_value, F32)
        return (jnp.concatenate([pre, jnp.repeat(tab_r, GRID_W, axis=0)], axis=0),
                jnp.concatenate([pre, jnp.tile(tab_c, (rows, 1))], axis=0))

    cos_r, cos_c = per_token(jnp.cos(ang_r), jnp.cos(ang_c), 1.0)
    sin_r, sin_c = per_token(jnp.sin(ang_r), jnp.sin(ang_c), 0.0)
    cos = jnp.concatenate([cos_r, cos_r, cos_c, cos_c], axis=-1)
    sin = jnp.concatenate([-sin_r, sin_r, -sin_c, sin_c], axis=-1)
    reps = LANES // head_dim
    return jnp.tile(cos, (1, reps)), jnp.tile(sin, (1, reps))


def _in_proj_kernel(x_ref, g_ref, w_ref, cr_ref, sr_ref, ca_ref, sa_ref, qg_ref, kg_ref,
                    oqk_ref, orest_ref, a_scr, *, tm):
    i = pl.program_id(0)
    x = x_ref[...]
    ms = jnp.mean(x * x, axis=-1, keepdims=True)
    a_scr[...] = ((x * lax.rsqrt(ms + NORM_EPS)) * g_ref[...]).astype(BF16)
    n_lane_groups = PROJ_TILE // LANES
    ret_tiles = 2
    plain_tiles = (2 * RET_HEADS * RET_V_DIM) // PROJ_TILE
    q_tiles = (ATTN_HEADS * ATTN_HEAD_DIM) // PROJ_TILE

    def tiled(ref):
        return jnp.concatenate([ref[...]] * n_lane_groups, axis=1)

    def qk_norm_rope(xg, gain_ref):
        ms_h = jnp.mean(xg * xg, axis=-1, keepdims=True)
        xn = (xg * lax.rsqrt(ms_h + NORM_EPS)) * gain_ref[...]
        return xn * ca_ref[...] + _swap_halves(xn, ATTN_HEAD_DIM // 4) * sa_ref[...]

    row = i * tm + lax.broadcasted_iota(jnp.int32, (tm, 1), 0)
    k_scale = jnp.where(row >= PAD, RET_QK_DIM ** -0.5, 0.0).astype(F32)
    lo = lax.broadcasted_iota(jnp.int32, (tm, LANES), 1) < RET_QK_DIM

    for j in range(w_ref.shape[1] // PROJ_TILE):
        acc = jnp.dot(a_scr[...], w_ref[:, j * PROJ_TILE:(j + 1) * PROJ_TILE],
                      preferred_element_type=F32)
        if j < ret_tiles:
            y = acc * tiled(cr_ref) + _swap_halves(acc, RET_QK_DIM // 4) * tiled(sr_ref)
            if j == 1:
                y = y * k_scale
            pieces = []
            for c in range(n_lane_groups):
                xg = y[:, c * LANES:(c + 1) * LANES]
                xr = pltpu.roll(xg, RET_QK_DIM, 1)
                pieces.append(jnp.where(lo, xg, xr))
                pieces.append(jnp.where(lo, xr, xg))
            oqk_ref[:, 2 * j * PROJ_TILE:2 * (j + 1) * PROJ_TILE] = (
                jnp.concatenate(pieces, axis=1).astype(BF16))
            continue
        jr = j - ret_tiles
        if jr < plain_tiles:
            out = acc
        elif jr < plain_tiles + q_tiles:
            out = jnp.concatenate(
                [qk_norm_rope(acc[:, c * LANES:(c + 1) * LANES], qg_ref) * (LOG2_E * ATTN_HEAD_DIM ** -0.5)
                 for c in range(n_lane_groups)], axis=1)
        else:
            pieces = [qk_norm_rope(acc[:, c * LANES:(c + 1) * LANES], kg_ref)
                      for c in range(ATTN_KV_HEADS)]
            pieces.append(acc[:, ATTN_KV_HEADS * LANES:])
            out = jnp.concatenate(pieces, axis=1)
        orest_ref[:, jr * PROJ_TILE:(jr + 1) * PROJ_TILE] = out.astype(BF16)


def _in_proj(h, g, w_bf, cr, sr, ca, sa, qg, kg, *, tm):
    lp, d = h.shape
    n_out = w_bf.shape[1]
    assert lp % tm == 0 and n_out == 9 * PROJ_TILE
    qk_w = 2 * 2 * RET_HEADS * RET_QK_DIM
    rest_w = n_out - 2 * RET_HEADS * RET_QK_DIM
    row_tab = pl.BlockSpec((tm, LANES), lambda i: (i, 0))
    vec = lambda n: pl.BlockSpec((1, n), lambda i: (0, 0))
    return pl.pallas_call(
        functools.partial(_in_proj_kernel, tm=tm),
        grid=(lp // tm,),
        in_specs=[
            pl.BlockSpec((tm, d), lambda i: (i, 0)),
            vec(d),
            pl.BlockSpec((d, n_out), lambda i: (0, 0), pipeline_mode=pl.Buffered(1)),
            row_tab, row_tab, row_tab, row_tab,
            vec(LANES), vec(LANES),
        ],
        out_specs=[
            pl.BlockSpec((tm, qk_w), lambda i: (i, 0)),
            pl.BlockSpec((tm, rest_w), lambda i: (i, 0)),
        ],
        out_shape=[jax.ShapeDtypeStruct((lp, qk_w), BF16), jax.ShapeDtypeStruct((lp, rest_w), BF16)],
        scratch_shapes=[pltpu.VMEM((tm, d), BF16)],
        compiler_params=_params(("arbitrary",)),
        name="in_proj",
    )(h, g, w_bf, cr, sr, ca, sa, qg, kg)


def _retention_kernel(lg_ref, q_ref, k_ref, v_ref, g_ref, gn_ref, o_ref, st_ref, dk_ref, dq_ref, dm_ref,
                      *, n_chunks):
    h = pl.program_id(0)
    lgf = lg_ref[0, h]
    lgb = lg_ref[1, h]
    half = CHUNK // 2
    row = lax.broadcasted_iota(jnp.int32, (CHUNK, CHUNK), 0).astype(F32)
    lane = lax.broadcasted_iota(jnp.int32, (CHUNK, CHUNK), 1).astype(F32)
    lane_lo = lane < half
    dk_ref[...] = jnp.exp(jnp.where(lane_lo, lgf * (CHUNK - 1.0 - row), lgb * row))
    dq_ref[...] = jnp.exp(jnp.where(lane_lo, lgf * (row + 1.0), lgb * (CHUNK - row)))
    diff = row - lane
    dm_ref[...] = 0.5 * jnp.where(diff >= 0, jnp.exp(lgf * jnp.maximum(diff, 0.0)),
                                  jnp.exp(lgb * jnp.maximum(-diff, 0.0)))
    cf = jnp.exp(jnp.full((half, CHUNK), lgf * CHUNK, F32))
    cb = jnp.exp(jnp.full((half, CHUNK), lgb * CHUNK, F32))

    def rows(n):
        return pl.ds(pl.multiple_of(n * CHUNK, CHUNK), CHUNK)

    def updates(n, c):
        kd = k_ref[rows(n), :].astype(F32) * dk_ref[...]
        st_ref[n] = jnp.dot(kd.T.astype(BF16), v_ref[rows(n), :], preferred_element_type=F32)
        return c

    unroll = max(u for u in range(1, RET_UNROLL + 1) if n_chunks % u == 0)

    def unrolled(fn):
        def body(t, c):
            for u in range(unroll):
                fn(t * unroll + u, c)
            return c
        return body

    lax.fori_loop(0, n_chunks // unroll, unrolled(updates), 0)

    def scan_fwd(n, s):
        u = st_ref[n, 0:half, :]
        st_ref[n, 0:half, :] = s
        return cf * s + u

    lax.fori_loop(0, n_chunks, scan_fwd, jnp.zeros((half, CHUNK), F32))

    def scan_bwd(t, s):
        n = n_chunks - 1 - t
        u = st_ref[n, half:CHUNK, :]
        st_ref[n, half:CHUNK, :] = s
        return cb * s + u

    lax.fori_loop(0, n_chunks, scan_bwd, jnp.zeros((half, CHUNK), F32))

    def outputs(n, c):
        q = q_ref[rows(n), :]
        k = k_ref[rows(n), :]
        v = v_ref[rows(n), :]
        s2 = lax.dot_general(q, k, (((1,), (1,)), ((), ())), preferred_element_type=F32)
        intra = jnp.dot((s2 * dm_ref[...]).astype(BF16), v, preferred_element_type=F32)
        qd = (q.astype(F32) * dq_ref[...]).astype(BF16)
        cross = jnp.dot(qd, st_ref[n].astype(BF16), preferred_element_type=F32)
        o = intra + cross
        mu = jnp.mean(o, axis=-1, keepdims=True)
        dlt = o - mu
        var = jnp.mean(dlt * dlt, axis=-1, keepdims=True)
        on = dlt * lax.rsqrt(var + GROUPNORM_EPS)
        gate = g_ref[rows(n), :].astype(F32)
        o_ref[rows(n), :] = ((gate * jax.nn.sigmoid(gate)) * (on * gn_ref[...])).astype(BF16)
        return c

    lax.fori_loop(0, n_chunks // unroll, unrolled(outputs), 0)


def _retention(log_gamma, qk, rest, gn_g):
    lp = qk.shape[0]
    n_chunks = lp // CHUNK
    col = lambda off: pl.BlockSpec((lp, LANES), lambda h, lg: (0, h + off))
    return pl.pallas_call(
        functools.partial(_retention_kernel, n_chunks=n_chunks),
        grid_spec=pltpu.PrefetchScalarGridSpec(
            num_scalar_prefetch=1,
            grid=(RET_HEADS,),
            in_specs=[col(0), col(RET_HEADS), col(0), col(RET_HEADS),
                      pl.BlockSpec((1, LANES), lambda h, lg: (0, h))],
            out_specs=pl.BlockSpec((lp, LANES), lambda h, lg: (0, h)),
            scratch_shapes=[pltpu.VMEM((n_chunks, CHUNK, CHUNK), F32),
                            pltpu.VMEM((CHUNK, CHUNK), F32),
                            pltpu.VMEM((CHUNK, CHUNK), F32),
                            pltpu.VMEM((CHUNK, CHUNK), F32)],
        ),
        out_shape=jax.ShapeDtypeStruct((lp, RET_HEADS * RET_V_DIM), BF16),
        compiler_params=_params(("arbitrary",)),
        name="retention",
    )(log_gamma, qk, qk, rest, rest, gn_g)


def _attention_kernel(q_ref, k_ref, vt_ref, km_ref, vtm_ref, o_ref, s_scr, mx_scr, m_scr, acc_scr,
                      *, tq, tk, n_kc):
    q_all = jnp.concatenate([q_ref[:, g * LANES:(g + 1) * LANES] for g in range(ATTN_GROUP)], axis=0)
    n_strips = (ATTN_GROUP * tq) // ATTN_STRIP
    q_strips = [q_all[s * ATTN_STRIP:(s + 1) * ATTN_STRIP, :] for s in range(n_strips)]

    def scores(k, s):
        return lax.dot_general(k, q_strips[s], (((1,), (1,)), ((), ())), preferred_element_type=F32)

    def key_rows(c):
        return pl.ds(pl.multiple_of(PREFIX + c * tk, LANES), tk)

    def score_stage(c, slot):
        k = k_ref[key_rows(c), :]
        for s in range(n_strips):
            st = scores(k, s)
            s_scr[slot, s] = st
            mx_scr[slot, s] = jnp.max(st, axis=0, keepdims=True)

    def value_stage(c, slot):
        vt = vt_ref[0, :, key_rows(c)]
        for s in range(n_strips):
            m_old = m_scr[s]
            m_new = jnp.maximum(m_old, mx_scr[slot, s])
            p = jnp.exp2(s_scr[slot, s] - m_new).astype(BF16)
            acc_scr[s] = jnp.exp2(m_old - m_new) * acc_scr[s] + jnp.dot(vt, p, preferred_element_type=F32)
            m_scr[s] = m_new

    for s in range(n_strips):
        st = scores(km_ref[0], s)
        m0 = jnp.max(st, axis=0, keepdims=True)
        m_scr[s] = m0
        acc_scr[s] = jnp.dot(vtm_ref[0], jnp.exp2(st - m0).astype(BF16), preferred_element_type=F32)

    score_stage(0, 0)

    def body(i, carry):
        c = 2 * i
        score_stage(c + 1, 1)
        value_stage(c, 0)
        score_stage(c + 2, 0)
        value_stage(c + 1, 1)
        return carry

    lax.fori_loop(0, n_kc // 2 - 1, body, 0)
    score_stage(n_kc - 1, 1)
    value_stage(n_kc - 2, 0)
    value_stage(n_kc - 1, 1)
    outs = []
    for s in range(n_strips):
        acc = acc_scr[s]
        outs.append((acc[0:ATTN_HEAD_DIM, :] / acc[ATTN_HEAD_DIM:ATTN_HEAD_DIM + 1, :]).T)
    o = jnp.concatenate(outs, axis=0)
    for g in range(ATTN_GROUP):
        o_ref[:, g * LANES:(g + 1) * LANES] = o[g * tq:(g + 1) * tq, :]


def _attention(rest, *, tq, tk):
    lp = rest.shape[0]
    n_tok = lp - PREFIX
    assert lp % tq == 0 and n_tok % (2 * tk) == 0 and (ATTN_GROUP * tq) % ATTN_STRIP == 0
    gw = ATTN_GROUP * ATTN_HEAD_DIM
    q_off = (2 * RET_HEADS * RET_V_DIM) // gw
    k_col = 2 * RET_HEADS * RET_V_DIM + ATTN_HEADS * ATTN_HEAD_DIM
    k_off = k_col // LANES
    kv_w = ATTN_KV_HEADS * ATTN_HEAD_DIM
    vt = rest[:, k_col + kv_w:k_col + 2 * kv_w].T.reshape(ATTN_KV_HEADS, ATTN_HEAD_DIM, lp)
    vt = jnp.concatenate([vt, jnp.ones((ATTN_KV_HEADS, ATTN_ONES_ROWS, lp), BF16)], axis=1)
    vt_meta = vt[:, :, PAD:PREFIX]
    k_meta = rest[PAD:PREFIX, k_col:k_col + kv_w].reshape(N_META, ATTN_KV_HEADS, ATTN_HEAD_DIM)
    k_meta = jnp.swapaxes(k_meta, 0, 1)
    n_strips = (ATTN_GROUP * tq) // ATTN_STRIP
    vt_rows = ATTN_HEAD_DIM + ATTN_ONES_ROWS
    return pl.pallas_call(
        functools.partial(_attention_kernel, tq=tq, tk=tk, n_kc=n_tok // tk),
        grid=(ATTN_KV_HEADS, lp // tq),
        in_specs=[pl.BlockSpec((tq, gw), lambda kv, i: (i, q_off + kv)),
                  pl.BlockSpec((lp, LANES), lambda kv, i: (0, k_off + kv)),
                  pl.BlockSpec((1, vt_rows, lp), lambda kv, i: (kv, 0, 0)),
                  pl.BlockSpec((1, N_META, ATTN_HEAD_DIM), lambda kv, i: (kv, 0, 0)),
                  pl.BlockSpec((1, vt_rows, N_META), lambda kv, i: (kv, 0, 0))],
        out_specs=pl.BlockSpec((tq, gw), lambda kv, i: (i, kv)),
        out_shape=jax.ShapeDtypeStruct((lp, ATTN_HEADS * ATTN_HEAD_DIM), F32),
        scratch_shapes=[pltpu.VMEM((2, n_strips, tk, ATTN_STRIP), F32),
                        pltpu.VMEM((2, n_strips, 1, ATTN_STRIP), F32),
                        pltpu.VMEM((n_strips, 1, ATTN_STRIP), F32),
                        pltpu.VMEM((n_strips, vt_rows, ATTN_STRIP), F32)],
        compiler_params=_params(("arbitrary", "arbitrary")),
        name="attention",
    )(rest, rest, vt, k_meta, vt_meta)


def _mix_router_kernel(ret_ref, att_ref, ang_ref, wo_ref, h_ref, fg_ref, wr_ref, br_ref,
                       h1_ref, f_ref, route_ref, cnt_ref, carry_scr, *, tm):
    i = pl.program_id(0)

    @pl.when(i == 0)
    def _():
        carry_scr[...] = jnp.zeros(carry_scr.shape, F32)

    ts = tm // MIX_SUB_BLOCKS
    for sb in range(MIX_SUB_BLOCKS):
        _mix_router_rows(slice(sb * ts, (sb + 1) * ts), ts, ret_ref, att_ref, ang_ref, wo_ref, h_ref,
                         fg_ref, wr_ref, br_ref, h1_ref, f_ref, route_ref, carry_scr)
    cnt_ref[...] = carry_scr[...]


def _mix_router_rows(rows, tm, ret_ref, att_ref, ang_ref, wo_ref, h_ref, fg_ref, wr_ref, br_ref,
                     h1_ref, f_ref, route_ref, carry_scr):
    att = att_ref[rows, :]
    ms = jnp.mean(att * att, axis=-1, keepdims=True)
    att_n = ((att * lax.rsqrt(ms + NORM_EPS)) * ang_ref[...]).astype(BF16)
    mix_in = jnp.concatenate([ret_ref[rows, :], att_n], axis=1)
    h1 = h_ref[rows, :] + jnp.dot(mix_in, wo_ref[...], preferred_element_type=F32)
    h1_ref[rows, :] = h1
    ms1 = jnp.mean(h1 * h1, axis=-1, keepdims=True)
    f = (h1 * lax.rsqrt(ms1 + NORM_EPS)) * fg_ref[...]
    f_ref[rows, :] = f

    f_hi = f.astype(BF16)
    f_lo = (f - f_hi.astype(F32)).astype(BF16)
    d = f.shape[1]
    k_half = 3 * d // 2
    logits = (jnp.dot(jnp.concatenate([f_hi, f_lo[:, :d // 2]], axis=1), wr_ref[0:k_half, :],
                      preferred_element_type=F32)
              + jnp.dot(jnp.concatenate([f_lo[:, d // 2:], f_hi], axis=1), wr_ref[k_half:, :],
                        preferred_element_type=F32)
              + br_ref[...])
    lane = lax.broadcasted_iota(jnp.int32, (tm, LANES), 1).astype(F32)
    far = float(LANES)
    g_logit = jnp.where(lane < N_GROUPS, logits, NEG_BIG)
    g_max = jnp.max(g_logit, axis=-1, keepdims=True)
    g_w = 1.0 / jnp.sum(jnp.exp(g_logit - g_max), axis=-1, keepdims=True)
    g_idx = jnp.min(jnp.where(g_logit == g_max, lane, far), axis=-1, keepdims=True)
    first = N_GROUPS + g_idx * EXPERTS_PER_GROUP
    e_logit = jnp.where((lane >= first) & (lane < first + EXPERTS_PER_GROUP), logits, NEG_BIG)
    v1 = jnp.max(e_logit, axis=-1, keepdims=True)
    i1 = jnp.min(jnp.where(e_logit == v1, lane, far), axis=-1, keepdims=True)
    e_rest = jnp.where(lane == i1, NEG_BIG, e_logit)
    v2 = jnp.max(e_rest, axis=-1, keepdims=True)
    i2 = jnp.min(jnp.where(e_rest == v2, lane, far), axis=-1, keepdims=True)
    e2 = jnp.exp(v2 - v1)
    w1 = g_w / (1.0 + e2)
    w2 = g_w * e2 / (1.0 + e2)

    oh1 = (lane == i1).astype(F32)
    oh2 = (lane == i2).astype(F32)
    oh = oh1 + oh2
    r_i = lax.broadcasted_iota(jnp.int32, (tm, tm), 0)
    c_i = lax.broadcasted_iota(jnp.int32, (tm, tm), 1)
    lower = (c_i < r_i).astype(BF16)
    before = jnp.dot(lower, oh.astype(BF16), preferred_element_type=F32) + carry_scr[...]
    rank1 = jnp.sum(before * oh1, axis=-1, keepdims=True)
    rank2 = jnp.sum(before * oh2, axis=-1, keepdims=True)
    carry_scr[...] = carry_scr[...] + jnp.sum(oh, axis=0, keepdims=True)

    route = jnp.where(lane == 0, i1 - N_GROUPS, 0.0)
    route = jnp.where(lane == 1, i2 - N_GROUPS, route)
    route = jnp.where(lane == 2, w1, route)
    route = jnp.where(lane == 3, w2, route)
    route = jnp.where(lane == 4, rank1, route)
    route = jnp.where(lane == 5, rank2, route)
    route_ref[rows, :] = route


def _mix_router(ret, att, ang, wo_bf, h, fg, wr, br, *, tm):
    lp, d = h.shape
    assert lp % tm == 0
    rowblk = lambda n: pl.BlockSpec((tm, n), lambda i: (i, 0))
    vec = lambda n: pl.BlockSpec((1, n), lambda i: (0, 0))
    full = lambda a: pl.BlockSpec(a.shape, lambda i: (0, 0))
    return pl.pallas_call(
        functools.partial(_mix_router_kernel, tm=tm),
        grid=(lp // tm,),
        in_specs=[rowblk(ret.shape[1]), rowblk(att.shape[1]), vec(att.shape[1]), full(wo_bf),
                  rowblk(d), vec(d), full(wr), vec(LANES)],
        out_specs=[rowblk(d), rowblk(d), rowblk(LANES), vec(LANES)],
        out_shape=[jax.ShapeDtypeStruct((lp, d), F32), jax.ShapeDtypeStruct((lp, d), F32),
                   jax.ShapeDtypeStruct((lp, LANES), F32), jax.ShapeDtypeStruct((1, LANES), F32)],
        scratch_shapes=[pltpu.VMEM((1, LANES), F32)],
        compiler_params=_params(("arbitrary",)),
        name="mix_router",
    )(ret, att, ang, wo_bf, h, fg, wr, br)


def _moe_kernel(ie_ref, irow_ref, irows_ref, sidx_ref, tok_ref, f_ref, wg_hbm, wu_hbm, wd_hbm, o_ref,
                x_scr, y_scr, wg_buf, wu_buf, wd_buf, gsem, ssem, wsem, *, n_items, n_chunks, lp, chunk):
    i = pl.program_id(0)
    c = pl.program_id(1)
    n_rows = irows_ref[i]
    slot = i % 2
    w_slot = (i * n_chunks + c) % 2

    def weights(item, cc, ws, start):
        e = ie_ref[item]
        half = chunk // 2
        col0 = pl.multiple_of(cc * chunk, chunk)
        copies = (
            (wg_hbm.at[e, :, pl.ds(col0, chunk)], wg_buf.at[ws]),
            (wu_hbm.at[e, :, pl.ds(col0, chunk)], wu_buf.at[ws]),
            (wd_hbm.at[e, pl.ds(col0, half), :], wd_buf.at[ws, 0:half, :]),
            (wd_hbm.at[e, pl.ds(col0 + half, half), :], wd_buf.at[ws, half:chunk, :]),
        )

        @pl.when(irows_ref[item] > 0)
        def _():
            for n, (src, dst) in enumerate(copies):
                cp = pltpu.make_async_copy(src, dst, wsem.at[ws, n])
                cp.start(priority=n % 2) if start else cp.wait()

    @pl.when((i == 0) & (c == 0))
    def _():
        weights(0, 0, 0, True)

    @pl.when(c + 1 < n_chunks)
    def _():
        weights(i, c + 1, 1 - w_slot, True)

    @pl.when((c + 1 == n_chunks) & (i + 1 < n_items))
    def _():
        weights(i + 1, 0, 1 - w_slot, True)

    weights(i, c, w_slot, False)

    def tile_row(ref, g, u):
        return ref.at[g, pl.ds(u, 1), :]

    def group_wait(src_rows, dst_rows, sem):
        pltpu.make_async_copy(src_rows, dst_rows, sem).wait()

    def gather(item, dst_slot, start):
        row0 = irow_ref[item]
        dst = x_scr.at[dst_slot]
        sem = gsem.at[dst_slot]

        def group(g, carry):
            if start:
                for u in range(DMA_GROUP):
                    tok = tok_ref[row0 + g * DMA_GROUP + u]
                    pltpu.make_async_copy(f_ref.at[pl.ds(tok, 1), :], tile_row(dst, g, u),
                                          sem).start(priority=u % 2)
            else:
                group_wait(f_ref.at[pl.ds(0, DMA_GROUP), :], dst.at[0], sem)
            return carry

        lax.fori_loop(0, (irows_ref[item] + DMA_GROUP - 1) // DMA_GROUP, group, 0)

    def scatter(item, start):
        row0 = irow_ref[item]
        n = irows_ref[item]
        n_groups = n // DMA_GROUP

        def group(g, carry):
            if start:
                for u in range(DMA_GROUP):
                    a = sidx_ref[row0 + g * DMA_GROUP + u]
                    pltpu.make_async_copy(tile_row(y_scr, g, u), o_ref.at[pl.ds(a, 1), :],
                                          ssem).start(priority=u % 2)
            else:
                group_wait(y_scr.at[0], o_ref.at[pl.ds(0, DMA_GROUP), :], ssem)
            return carry

        lax.fori_loop(0, n_groups, group, 0)

        def single(r, carry):
            cp = pltpu.make_async_copy(y_scr.at[r // DMA_GROUP, pl.ds(r % DMA_GROUP, 1), :],
                                       o_ref.at[pl.ds(sidx_ref[row0 + r], 1), :], ssem)
            cp.start() if start else cp.wait()
            return carry

        lax.fori_loop(n_groups * DMA_GROUP, n, single, 0)

    @pl.when(c == 0)
    def _():
        @pl.when(i == 0)
        def _():
            x_scr[...] = jnp.zeros(x_scr.shape, F32)
            y_scr[...] = jnp.zeros(y_scr.shape, F32)
            gather(0, 0, True)

        @pl.when(i + 1 < n_items)
        def _():
            gather(i + 1, 1 - slot, True)

        gather(i, slot, False)

        @pl.when(i > 0)
        def _():
            scatter(i - 1, False)

    n_blocks = (n_rows + MOE_ROW_STEP - 1) // MOE_ROW_STEP
    for v in range(1, MOE_ITEM_ROWS // MOE_ROW_STEP + 1):
        @pl.when(n_blocks == v)
        def _(m=v * MOE_ROW_STEP):
            mg = m // DMA_GROUP
            d = x_scr.shape[-1]
            x = x_scr[slot, 0:mg].reshape(m, d).astype(BF16)
            gate = jnp.dot(x, wg_buf[w_slot].astype(BF16), preferred_element_type=F32)
            up = jnp.dot(x, wu_buf[w_slot].astype(BF16), preferred_element_type=F32)
            hid = ((gate * jax.nn.sigmoid(gate)) * up).astype(BF16)
            y = jnp.dot(hid, wd_buf[w_slot].astype(BF16), preferred_element_type=F32)
            y_scr[0:mg] = jnp.where(c > 0, y_scr[0:mg], 0.0) + y.reshape(mg, DMA_GROUP, d)

    @pl.when(c == n_chunks - 1)
    def _():
        scatter(i, True)

        @pl.when(i == n_items - 1)
        def _():
            scatter(i, False)


def _moe_experts(item_e, item_row, item_rows, sidx, tok_row, f, w_gate, w_up, w_down, *, chunk):
    lp, d = f.shape
    d_e = w_gate.shape[2]
    n_items = item_e.shape[0]
    n_chunks = d_e // chunk
    return pl.pallas_call(
        functools.partial(_moe_kernel, n_items=n_items, n_chunks=n_chunks, lp=lp, chunk=chunk),
        grid_spec=pltpu.PrefetchScalarGridSpec(
            num_scalar_prefetch=5,
            grid=(n_items, n_chunks),
            in_specs=[pl.BlockSpec(memory_space=pl.ANY)] * 4,
            out_specs=pl.BlockSpec(memory_space=pl.ANY),
            scratch_shapes=[pltpu.VMEM((2, MOE_ITEM_ROWS // DMA_GROUP, DMA_GROUP, d), F32),
                            pltpu.VMEM((MOE_ITEM_ROWS // DMA_GROUP, DMA_GROUP, d), F32),
                            pltpu.VMEM((2, d, chunk), F32),
                            pltpu.VMEM((2, d, chunk), F32),
                            pltpu.VMEM((2, chunk, d), F32),
                            pltpu.SemaphoreType.DMA((2,)),
                            pltpu.SemaphoreType.DMA(()),
                            pltpu.SemaphoreType.DMA((2, 4))],
        ),
        out_shape=jax.ShapeDtypeStruct((TOP_K * lp, d), F32),
        compiler_params=_params(("arbitrary", "arbitrary")),
        name="moe_experts",
    )(item_e, item_row, item_rows, sidx, tok_row, f, w_gate, w_up, w_down)


def _combine_kernel(h1_ref, route_ref, y0_ref, y1_ref, g_ref, o_ref):
    route = route_ref[...]
    h2 = h1_ref[...] + (route[:, 2:3] * y0_ref[0] + route[:, 3:4] * y1_ref[0])
    ms = jnp.mean(h2 * h2, axis=-1, keepdims=True)
    o_ref[...] = (h2 * lax.rsqrt(ms + NORM_EPS)) * g_ref[...]


def _combine(h1, route, y, g, *, tm):
    lp, d = h1.shape
    assert PREFIX % tm == 0
    skip = PREFIX // tm
    n_tok = lp - PREFIX
    y3 = y.reshape(TOP_K, lp, d)
    return pl.pallas_call(
        _combine_kernel,
        grid=(n_tok // tm,),
        in_specs=[pl.BlockSpec((tm, d), lambda i: (i + skip, 0)),
                  pl.BlockSpec((tm, LANES), lambda i: (i + skip, 0)),
                  pl.BlockSpec((1, tm, d), lambda i: (0, i + skip, 0)),
                  pl.BlockSpec((1, tm, d), lambda i: (1, i + skip, 0)),
                  pl.BlockSpec((1, d), lambda i: (0, 0))],
        out_specs=pl.BlockSpec((tm, d), lambda i: (i, 0)),
        out_shape=jax.ShapeDtypeStruct((n_tok, d), F32),
        compiler_params=_params(("arbitrary",)),
        name="combine",
    )(h1, route, y3, y3, g)


def _row_tile(lp, candidates):
    for t in candidates:
        if lp % t == 0:
            return t
    raise ValueError(f"no row tile for {lp}")


def kernel(x, meta_tokens, norm_mix_g, w_in, ret_decay_logit_fwd, ret_decay_logit_bwd, ret_norm_g,
           attn_q_norm_g, attn_k_norm_g, attn_out_norm_g, w_out, norm_ffn_g, router_group_w,
           router_group_b, router_expert_w, router_expert_b, expert_w_gate, expert_w_up, expert_w_down,
           norm_final_g):
    batch, n_tok, d = x.shape
    assert batch == 1 and norm_mix_g.shape[0] == 1
    lp = n_tok + PREFIX
    h = jnp.concatenate([jnp.zeros((PAD, d), x.dtype), meta_tokens.astype(x.dtype), x[0]], axis=0)

    cr, sr = _rope_tables(n_tok, RET_QK_DIM)
    ca, sa = _rope_tables(n_tok, ATTN_HEAD_DIM)
    qk, rest = _in_proj(h, norm_mix_g, w_in[0].astype(BF16), cr, sr, ca, sa,
                        attn_q_norm_g, attn_k_norm_g, tm=_row_tile(lp, (320, 128)))

    log_gamma = jnp.stack([jax.nn.log_sigmoid(ret_decay_logit_fwd[0].astype(F32)),
                           jax.nn.log_sigmoid(ret_decay_logit_bwd[0].astype(F32))])
    ret = _retention(log_gamma, qk, rest, ret_norm_g)
    att = _attention(rest, tq=_row_tile(lp, (320, 128, 64)), tk=_row_tile(n_tok, (512, 256, 128)))

    n_route = N_GROUPS + N_EXPERTS
    wr = jnp.pad(jnp.concatenate([router_group_w[0], router_expert_w[0]], axis=1).astype(F32),
                 ((0, 0), (0, LANES - n_route)))
    br = jnp.pad(jnp.concatenate([router_group_b[0], router_expert_b[0]]).astype(F32),
                 (0, LANES - n_route))[None]
    wr_hi = wr.astype(BF16)
    wr_lo = (wr - wr_hi.astype(F32)).astype(BF16)
    wr = jnp.concatenate([wr_hi, wr_hi, wr_lo], axis=0)
    h1, f, route, cnt = _mix_router(ret, att, attn_out_norm_g, w_out[0].astype(BF16), h, norm_ffn_g,
                                    wr, br, tm=_row_tile(lp, (320, 128)))

    expert = route[:, 0:TOP_K].astype(jnp.int32)
    rank = route[:, 4:4 + TOP_K].astype(jnp.int32)
    counts = cnt[0, N_GROUPS:n_route].astype(jnp.int32)
    padded = (counts + EXPERT_BLOCK - 1) // EXPERT_BLOCK * EXPERT_BLOCK
    pad_end = jnp.cumsum(padded)
    pad_start = pad_end - padded
    is_expert = expert.T[:, :, None] == jnp.arange(N_EXPERTS, dtype=jnp.int32)
    seg_start = jnp.sum(jnp.where(is_expert, pad_start, 0), axis=-1)
    dest = (seg_start + rank.T).reshape(-1)
    n_rows = (-(-(lp * TOP_K) // EXPERT_BLOCK) + N_EXPERTS) * EXPERT_BLOCK
    sidx = jnp.full((n_rows,), -1, jnp.int32).at[dest].set(jnp.arange(TOP_K * lp, dtype=jnp.int32))
    groups = (counts + MOE_ITEM_ROWS - 1) // MOE_ITEM_ROWS
    g_end = jnp.cumsum(groups)
    g_start = g_end - groups
    n_items = N_EXPERTS + -(-(lp * TOP_K) // MOE_ITEM_ROWS)
    ii = jnp.arange(n_items, dtype=jnp.int32)
    valid = ii < g_end[-1]
    item_e = jnp.sum((g_end[None, :] <= jnp.minimum(ii, g_end[-1] - 1)[:, None]).astype(jnp.int32), axis=1)
    g_in = ii - g_start[item_e]
    item_row = jnp.where(valid, pad_start[item_e] + g_in * MOE_ITEM_ROWS, 0).astype(jnp.int32)
    item_rows = jnp.where(valid, jnp.minimum(MOE_ITEM_ROWS, counts[item_e] - g_in * MOE_ITEM_ROWS),
                          0).astype(jnp.int32)

    tok_row = jnp.where(sidx >= lp, sidx - lp, jnp.maximum(sidx, 0))
    y = _moe_experts(item_e, item_row, item_rows, sidx, tok_row, f, expert_w_gate[0], expert_w_up[0],
                     expert_w_down[0], chunk=MOE_CHUNK)
    out = _combine(h1, route, y, norm_final_g[None], tm=PREFIX)
    return out[None]
```

```python
import functools

import jax
import jax.numpy as jnp
from jax import lax
from jax.experimental import pallas as pl
from jax.experimental.pallas import tpu as pltpu

F32 = jnp.float32
BF16 = jnp.bfloat16

N_META = 16
CHUNK = 128
PREFIX = CHUNK
PAD = PREFIX - N_META
GRID_W = 64
ROPE_THETA = 10000.0
NORM_EPS = 1e-6
GROUPNORM_EPS = 1e-5

RET_HEADS = 8
RET_QK_DIM = 64
RET_V_DIM = 128
RET_UNROLL = 13
ATTN_HEADS = 8
ATTN_KV_HEADS = 2
ATTN_HEAD_DIM = 128
ATTN_GROUP = ATTN_HEADS // ATTN_KV_HEADS
ATTN_STRIP = 256
ATTN_ONES_ROWS = 16
LOG2_E = 1.4426950408889634

N_GROUPS = 8
EXPERTS_PER_GROUP = 8
N_EXPERTS = N_GROUPS * EXPERTS_PER_GROUP
TOP_K = 2
EXPERT_BLOCK = 128
MOE_ITEM_ROWS = 512
MOE_CHUNK = 512
MOE_ROW_STEP = 64
DMA_GROUP = 8

LANES = 128
PROJ_TILE = 512
PROJ_SUB_ROWS = (320, 192, 128)
MIX_SUB_BLOCKS = 2
VMEM_LIMIT = 56 * 1024 * 1024
NEG_BIG = -1e30


def _params(sem, vmem=VMEM_LIMIT):
    return pltpu.CompilerParams(dimension_semantics=sem, vmem_limit_bytes=vmem)


def _swap_halves(x, hw):
    w = x.shape[-1]
    lane = lax.broadcasted_iota(jnp.int32, x.shape, 1)
    first = (lane & hw) == 0
    return jnp.where(first, pltpu.roll(x, w - hw, 1), pltpu.roll(x, hw, 1))


def _rope_tables(n_tok, head_dim):
    half = head_dim // 2
    freqs = ROPE_THETA ** (-jnp.arange(0, half, 2, dtype=F32) / half)
    rows = n_tok // GRID_W
    ang_r = jnp.arange(rows, dtype=F32)[:, None] * freqs[None, :]
    ang_c = jnp.arange(GRID_W, dtype=F32)[:, None] * freqs[None, :]

    def per_token(tab_r, tab_c, prefix_value):
        pre = jnp.full((PREFIX, tab_r.shape[1]), prefix_value, F32)
        return (jnp.concatenate([pre, jnp.repeat(tab_r, GRID_W, axis=0)], axis=0),
                jnp.concatenate([pre, jnp.tile(tab_c, (rows, 1))], axis=0))

    cos_r, cos_c = per_token(jnp.cos(ang_r), jnp.cos(ang_c), 1.0)
    sin_r, sin_c = per_token(jnp.sin(ang_r), jnp.sin(ang_c), 0.0)
    cos = jnp.concatenate([cos_r, cos_r, cos_c, cos_c], axis=-1)
    sin = jnp.concatenate([-sin_r, sin_r, -sin_c, sin_c], axis=-1)
    reps = LANES // head_dim
    return jnp.tile(cos, (1, reps)), jnp.tile(sin, (1, reps))


def _stream_specs(n_blocks, d):
    return [pl.BlockSpec((PREFIX, d), lambda i, j=j: (jnp.maximum(n_blocks * i - 1 + j, 0), 0))
            for j in range(n_blocks)]


def _stream_rows(pre_ref, x_refs):
    first = jnp.where(pl.program_id(0) == 0, pre_ref[...], x_refs[0][...])
    return jnp.concatenate([first] + [r[...] for r in x_refs[1:]], axis=0)


def _stream_blocks(lp):
    n_blocks = lp // PREFIX
    return max(k for k in (5, 3, 1) if n_blocks % k == 0)


def _in_proj_kernel(*refs, tm, n_x):
    pre_ref, x_refs = refs[0], refs[1:1 + n_x]
    (g_ref, w_ref, cr_ref, sr_ref, ca_ref, sa_ref, qg_ref, kg_ref,
     oqk_ref, orest_ref, a_scr) = refs[1 + n_x:]
    i = pl.program_id(0)
    x = _stream_rows(pre_ref, x_refs)
    ms = jnp.mean(x * x, axis=-1, keepdims=True)
    a_scr[...] = ((x * lax.rsqrt(ms + NORM_EPS)) * g_ref[...]).astype(BF16)
    n_lane_groups = PROJ_TILE // LANES
    ret_tiles = 2
    plain_tiles = (2 * RET_HEADS * RET_V_DIM) // PROJ_TILE
    q_tiles = (ATTN_HEADS * ATTN_HEAD_DIM) // PROJ_TILE

    def tiled(ref):
        return jnp.concatenate([ref[...]] * n_lane_groups, axis=1)

    ts = max(t for t in PROJ_SUB_ROWS if tm % t == 0)
    lo = lax.broadcasted_iota(jnp.int32, (ts, LANES), 1) < RET_QK_DIM

    for r0 in range(0, tm, ts):
        rows = slice(r0, r0 + ts)
        row = i * tm + r0 + lax.broadcasted_iota(jnp.int32, (ts, 1), 0)
        k_scale = jnp.where(row >= PAD, RET_QK_DIM ** -0.5, 0.0).astype(F32)
        cos_r, sin_r = tiled(cr_ref.at[rows, :]), tiled(sr_ref.at[rows, :])
        cos_a, sin_a = ca_ref[rows, :], sa_ref[rows, :]

        def qk_norm_rope(xg, gain_ref):
            ms_h = jnp.mean(xg * xg, axis=-1, keepdims=True)
            xn = (xg * lax.rsqrt(ms_h + NORM_EPS)) * gain_ref[...]
            return xn * cos_a + _swap_halves(xn, ATTN_HEAD_DIM // 4) * sin_a

        for j in range(w_ref.shape[1] // PROJ_TILE):
            acc = jnp.dot(a_scr[rows, :], w_ref[:, j * PROJ_TILE:(j + 1) * PROJ_TILE],
                          preferred_element_type=F32)
            if j < ret_tiles:
                y = acc * cos_r + _swap_halves(acc, RET_QK_DIM // 4) * sin_r
                if j == 1:
                    y = y * k_scale
                pieces = []
                for c in range(n_lane_groups):
                    xg = y[:, c * LANES:(c + 1) * LANES]
                    xr = pltpu.roll(xg, RET_QK_DIM, 1)
                    pieces.append(jnp.where(lo, xg, xr))
                    pieces.append(jnp.where(lo, xr, xg))
                oqk_ref[rows, 2 * j * PROJ_TILE:2 * (j + 1) * PROJ_TILE] = (
                    jnp.concatenate(pieces, axis=1).astype(BF16))
                continue
            jr = j - ret_tiles
            if jr < plain_tiles:
                out = acc
            elif jr < plain_tiles + q_tiles:
                out = jnp.concatenate(
                    [qk_norm_rope(acc[:, c * LANES:(c + 1) * LANES], qg_ref)
                     * (LOG2_E * ATTN_HEAD_DIM ** -0.5) for c in range(n_lane_groups)], axis=1)
            else:
                pieces = [qk_norm_rope(acc[:, c * LANES:(c + 1) * LANES], kg_ref)
                          for c in range(ATTN_KV_HEADS)]
                pieces.append(acc[:, ATTN_KV_HEADS * LANES:])
                out = jnp.concatenate(pieces, axis=1)
            orest_ref[rows, jr * PROJ_TILE:(jr + 1) * PROJ_TILE] = out.astype(BF16)


def _in_proj(pre, x, g, w_bf, cr, sr, ca, sa, qg, kg):
    d = x.shape[1]
    lp = x.shape[0] + PREFIX
    n_x = _stream_blocks(lp)
    tm = n_x * PREFIX
    n_out = w_bf.shape[1]
    assert lp % tm == 0 and n_out == 9 * PROJ_TILE
    qk_w = 2 * 2 * RET_HEADS * RET_QK_DIM
    rest_w = n_out - 2 * RET_HEADS * RET_QK_DIM
    row_tab = pl.BlockSpec((tm, LANES), lambda i: (i, 0))
    vec = lambda n: pl.BlockSpec((1, n), lambda i: (0, 0))
    return pl.pallas_call(
        functools.partial(_in_proj_kernel, tm=tm, n_x=n_x),
        grid=(lp // tm,),
        in_specs=[
            pl.BlockSpec((PREFIX, d), lambda i: (0, 0)),
            *_stream_specs(n_x, d),
            vec(d),
            pl.BlockSpec((d, n_out), lambda i: (0, 0), pipeline_mode=pl.Buffered(1)),
            row_tab, row_tab, row_tab, row_tab,
            vec(LANES), vec(LANES),
        ],
        out_specs=[
            pl.BlockSpec((tm, qk_w), lambda i: (i, 0)),
            pl.BlockSpec((tm, rest_w), lambda i: (i, 0)),
        ],
        out_shape=[jax.ShapeDtypeStruct((lp, qk_w), BF16), jax.ShapeDtypeStruct((lp, rest_w), BF16)],
        scratch_shapes=[pltpu.VMEM((tm, d), BF16)],
        compiler_params=_params(("arbitrary",)),
        name="in_proj",
    )(pre, *([x] * n_x), g, w_bf, cr, sr, ca, sa, qg, kg)


def _retention_kernel(lg_ref, q_ref, k_ref, v_ref, g_ref, gn_ref, o_ref, st_ref, dk_ref, dq_ref, dm_ref,
                      *, n_chunks):
    h = pl.program_id(0)
    lgf = lg_ref[0, h]
    lgb = lg_ref[1, h]
    half = CHUNK // 2
    row = lax.broadcasted_iota(jnp.int32, (CHUNK, CHUNK), 0).astype(F32)
    lane = lax.broadcasted_iota(jnp.int32, (CHUNK, CHUNK), 1).astype(F32)
    lane_lo = lane < half
    dk_ref[...] = jnp.exp(jnp.where(lane_lo, lgf * (CHUNK - 1.0 - row), lgb * row))
    dq_ref[...] = jnp.exp(jnp.where(lane_lo, lgf * (row + 1.0), lgb * (CHUNK - row)))
    diff = row - lane
    dm_ref[...] = 0.5 * jnp.where(diff >= 0, jnp.exp(lgf * jnp.maximum(diff, 0.0)),
                                  jnp.exp(lgb * jnp.maximum(-diff, 0.0)))
    cf = jnp.exp(jnp.full((half, CHUNK), lgf * CHUNK, F32))
    cb = jnp.exp(jnp.full((half, CHUNK), lgb * CHUNK, F32))

    def rows(n):
        return pl.ds(pl.multiple_of(n * CHUNK, CHUNK), CHUNK)

    def updates(n, c):
        kd = k_ref[rows(n), :].astype(F32) * dk_ref[...]
        st_ref[n] = jnp.dot(kd.T.astype(BF16), v_ref[rows(n), :], preferred_element_type=F32)
        return c

    unroll = max(u for u in range(1, RET_UNROLL + 1) if n_chunks % u == 0)

    def unrolled(fn):
        def body(t, c):
            for u in range(unroll):
                fn(t * unroll + u, c)
            return c
        return body

    lax.fori_loop(0, n_chunks // unroll, unrolled(updates), 0)

    def scan_fwd(n, s):
        u = st_ref[n, 0:half, :]
        st_ref[n, 0:half, :] = s
        return cf * s + u

    lax.fori_loop(0, n_chunks, scan_fwd, jnp.zeros((half, CHUNK), F32))

    def scan_bwd(t, s):
        n = n_chunks - 1 - t
        u = st_ref[n, half:CHUNK, :]
        st_ref[n, half:CHUNK, :] = s
        return cb * s + u

    lax.fori_loop(0, n_chunks, scan_bwd, jnp.zeros((half, CHUNK), F32))

    def outputs(n, c):
        q = q_ref[rows(n), :]
        k = k_ref[rows(n), :]
        v = v_ref[rows(n), :]
        s2 = lax.dot_general(q, k, (((1,), (1,)), ((), ())), preferred_element_type=F32)
        intra = jnp.dot((s2 * dm_ref[...]).astype(BF16), v, preferred_element_type=F32)
        qd = (q.astype(F32) * dq_ref[...]).astype(BF16)
        cross = jnp.dot(qd, st_ref[n].astype(BF16), preferred_element_type=F32)
        o = intra + cross
        mu = jnp.mean(o, axis=-1, keepdims=True)
        dlt = o - mu
        var = jnp.mean(dlt * dlt, axis=-1, keepdims=True)
        on = dlt * lax.rsqrt(var + GROUPNORM_EPS)
        gate = g_ref[rows(n), :].astype(F32)
        o_ref[rows(n), :] = ((gate * jax.nn.sigmoid(gate)) * (on * gn_ref[...])).astype(BF16)
        return c

    lax.fori_loop(0, n_chunks // unroll, unrolled(outputs), 0)


def _retention(log_gamma, qk, rest, gn_g):
    lp = qk.shape[0]
    n_chunks = lp // CHUNK
    col = lambda off: pl.BlockSpec((lp, LANES), lambda h, lg: (0, h + off))
    return pl.pallas_call(
        functools.partial(_retention_kernel, n_chunks=n_chunks),
        grid_spec=pltpu.PrefetchScalarGridSpec(
            num_scalar_prefetch=1,
            grid=(RET_HEADS,),
            in_specs=[col(0), col(RET_HEADS), col(0), col(RET_HEADS),
                      pl.BlockSpec((1, LANES), lambda h, lg: (0, h))],
            out_specs=pl.BlockSpec((lp, LANES), lambda h, lg: (0, h)),
            scratch_shapes=[pltpu.VMEM((n_chunks, CHUNK, CHUNK), F32),
                            pltpu.VMEM((CHUNK, CHUNK), F32),
                            pltpu.VMEM((CHUNK, CHUNK), F32),
                            pltpu.VMEM((CHUNK, CHUNK), F32)],
        ),
        out_shape=jax.ShapeDtypeStruct((lp, RET_HEADS * RET_V_DIM), BF16),
        compiler_params=_params(("arbitrary",)),
        name="retention",
    )(log_gamma, qk, qk, rest, rest, gn_g)


def _attention_kernel(q_ref, k_ref, vt_ref, km_ref, vtm_ref, o_ref, s_scr, mx_scr, m_scr, acc_scr,
                      *, tq, tk, n_kc):
    q_all = jnp.concatenate([q_ref[:, g * LANES:(g + 1) * LANES] for g in range(ATTN_GROUP)], axis=0)
    n_strips = (ATTN_GROUP * tq) // ATTN_STRIP
    q_strips = [q_all[s * ATTN_STRIP:(s + 1) * ATTN_STRIP, :] for s in range(n_strips)]

    def scores(k, s):
        return lax.dot_general(k, q_strips[s], (((1,), (1,)), ((), ())), preferred_element_type=F32)

    def key_rows(c):
        return pl.ds(pl.multiple_of(PREFIX + c * tk, LANES), tk)

    def score_stage(c, slot):
        k = k_ref[key_rows(c), :]
        for s in range(n_strips):
            st = scores(k, s)
            s_scr[slot, s] = st
            mx_scr[slot, s] = jnp.max(st, axis=0, keepdims=True)

    def value_stage(c, slot):
        vt = vt_ref[0, :, key_rows(c)]
        for s in range(n_strips):
            m_old = m_scr[s]
            m_new = jnp.maximum(m_old, mx_scr[slot, s])
            p = jnp.exp2(s_scr[slot, s] - m_new).astype(BF16)
            acc_scr[s] = jnp.exp2(m_old - m_new) * acc_scr[s] + jnp.dot(vt, p, preferred_element_type=F32)
            m_scr[s] = m_new

    for s in range(n_strips):
        st = scores(km_ref[0], s)
        m0 = jnp.max(st, axis=0, keepdims=True)
        m_scr[s] = m0
        acc_scr[s] = jnp.dot(vtm_ref[0], jnp.exp2(st - m0).astype(BF16), preferred_element_type=F32)

    score_stage(0, 0)

    def body(i, carry):
        c = 2 * i
        score_stage(c + 1, 1)
        value_stage(c, 0)
        score_stage(c + 2, 0)
        value_stage(c + 1, 1)
        return carry

    lax.fori_loop(0, n_kc // 2 - 1, body, 0)
    score_stage(n_kc - 1, 1)
    value_stage(n_kc - 2, 0)
    value_stage(n_kc - 1, 1)
    outs = []
    for s in range(n_strips):
        acc = acc_scr[s]
        outs.append((acc[0:ATTN_HEAD_DIM, :] / acc[ATTN_HEAD_DIM:ATTN_HEAD_DIM + 1, :]).T)
    o = jnp.concatenate(outs, axis=0)
    for g in range(ATTN_GROUP):
        o_ref[:, g * LANES:(g + 1) * LANES] = o[g * tq:(g + 1) * tq, :]


def _attention(rest, *, tq, tk):
    lp = rest.shape[0]
    n_tok = lp - PREFIX
    assert lp % tq == 0 and n_tok % (2 * tk) == 0 and (ATTN_GROUP * tq) % ATTN_STRIP == 0
    gw = ATTN_GROUP * ATTN_HEAD_DIM
    q_off = (2 * RET_HEADS * RET_V_DIM) // gw
    k_col = 2 * RET_HEADS * RET_V_DIM + ATTN_HEADS * ATTN_HEAD_DIM
    k_off = k_col // LANES
    kv_w = ATTN_KV_HEADS * ATTN_HEAD_DIM
    vt = rest[:, k_col + kv_w:k_col + 2 * kv_w].T.reshape(ATTN_KV_HEADS, ATTN_HEAD_DIM, lp)
    vt = jnp.concatenate([vt, jnp.ones((ATTN_KV_HEADS, ATTN_ONES_ROWS, lp), BF16)], axis=1)
    vt_meta = vt[:, :, PAD:PREFIX]
    k_meta = rest[PAD:PREFIX, k_col:k_col + kv_w].reshape(N_META, ATTN_KV_HEADS, ATTN_HEAD_DIM)
    k_meta = jnp.swapaxes(k_meta, 0, 1)
    n_strips = (ATTN_GROUP * tq) // ATTN_STRIP
    vt_rows = ATTN_HEAD_DIM + ATTN_ONES_ROWS
    return pl.pallas_call(
        functools.partial(_attention_kernel, tq=tq, tk=tk, n_kc=n_tok // tk),
        grid=(ATTN_KV_HEADS, lp // tq),
        in_specs=[pl.BlockSpec((tq, gw), lambda kv, i: (i, q_off + kv)),
                  pl.BlockSpec((lp, LANES), lambda kv, i: (0, k_off + kv)),
                  pl.BlockSpec((1, vt_rows, lp), lambda kv, i: (kv, 0, 0)),
                  pl.BlockSpec((1, N_META, ATTN_HEAD_DIM), lambda kv, i: (kv, 0, 0)),
                  pl.BlockSpec((1, vt_rows, N_META), lambda kv, i: (kv, 0, 0))],
        out_specs=pl.BlockSpec((tq, gw), lambda kv, i: (i, kv)),
        out_shape=jax.ShapeDtypeStruct((lp, ATTN_HEADS * ATTN_HEAD_DIM), F32),
        scratch_shapes=[pltpu.VMEM((2, n_strips, tk, ATTN_STRIP), F32),
                        pltpu.VMEM((2, n_strips, 1, ATTN_STRIP), F32),
                        pltpu.VMEM((n_strips, 1, ATTN_STRIP), F32),
                        pltpu.VMEM((n_strips, vt_rows, ATTN_STRIP), F32)],
        compiler_params=_params(("arbitrary", "arbitrary")),
        name="attention",
    )(rest, rest, vt, k_meta, vt_meta)


def _mix_router_kernel(*refs, tm, n_x):
    pre_ref, x_refs = refs[0], refs[1:1 + n_x]
    (ret_ref, att_ref, ang_ref, wo_ref, fg_ref, wr_ref, br_ref,
     h1_ref, f_ref, route_ref, cnt_ref, carry_scr) = refs[1 + n_x:]
    i = pl.program_id(0)

    @pl.when(i == 0)
    def _():
        carry_scr[...] = jnp.zeros(carry_scr.shape, F32)

    blocks = [jnp.where(i == 0, pre_ref[...], x_refs[0][...])] + [r[...] for r in x_refs[1:]]
    for b0 in range(0, n_x, MIX_SUB_BLOCKS):
        b1 = min(b0 + MIX_SUB_BLOCKS, n_x)
        h_rows = blocks[b0] if b1 - b0 == 1 else jnp.concatenate(blocks[b0:b1], axis=0)
        _mix_router_rows(slice(b0 * PREFIX, b1 * PREFIX), (b1 - b0) * PREFIX, h_rows, ret_ref, att_ref,
                         ang_ref, wo_ref, fg_ref, wr_ref, br_ref, h1_ref, f_ref, route_ref, carry_scr)
    cnt_ref[...] = carry_scr[...]


def _mix_router_rows(rows, tm, h_rows, ret_ref, att_ref, ang_ref, wo_ref, fg_ref, wr_ref, br_ref,
                     h1_ref, f_ref, route_ref, carry_scr):
    att = att_ref[rows, :]
    ms = jnp.mean(att * att, axis=-1, keepdims=True)
    att_n = ((att * lax.rsqrt(ms + NORM_EPS)) * ang_ref[...]).astype(BF16)
    mix_in = jnp.concatenate([ret_ref[rows, :], att_n], axis=1)
    h1 = h_rows + jnp.dot(mix_in, wo_ref[...], preferred_element_type=F32)
    h1_ref[rows, :] = h1
    ms1 = jnp.mean(h1 * h1, axis=-1, keepdims=True)
    f = (h1 * lax.rsqrt(ms1 + NORM_EPS)) * fg_ref[...]
    f_ref[rows, :] = f

    f_hi = f.astype(BF16)
    f_lo = (f - f_hi.astype(F32)).astype(BF16)
    d = f.shape[1]
    k_half = 3 * d // 2
    logits = (jnp.dot(jnp.concatenate([f_hi, f_lo[:, :d // 2]], axis=1), wr_ref[0:k_half, :],
                      preferred_element_type=F32)
              + jnp.dot(jnp.concatenate([f_lo[:, d // 2:], f_hi], axis=1), wr_ref[k_half:, :],
                        preferred_element_type=F32)
              + br_ref[...])
    lane = lax.broadcasted_iota(jnp.int32, (tm, LANES), 1).astype(F32)
    far = float(LANES)
    g_logit = jnp.where(lane < N_GROUPS, logits, NEG_BIG)
    g_max = jnp.max(g_logit, axis=-1, keepdims=True)
    g_w = 1.0 / jnp.sum(jnp.exp(g_logit - g_max), axis=-1, keepdims=True)
    g_idx = jnp.min(jnp.where(g_logit == g_max, lane, far), axis=-1, keepdims=True)
    first = N_GROUPS + g_idx * EXPERTS_PER_GROUP
    e_logit = jnp.where((lane >= first) & (lane < first + EXPERTS_PER_GROUP), logits, NEG_BIG)
    v1 = jnp.max(e_logit, axis=-1, keepdims=True)
    i1 = jnp.min(jnp.where(e_logit == v1, lane, far), axis=-1, keepdims=True)
    e_rest = jnp.where(lane == i1, NEG_BIG, e_logit)
    v2 = jnp.max(e_rest, axis=-1, keepdims=True)
    i2 = jnp.min(jnp.where(e_rest == v2, lane, far), axis=-1, keepdims=True)
    e2 = jnp.exp(v2 - v1)
    w1 = g_w / (1.0 + e2)
    w2 = g_w * e2 / (1.0 + e2)

    oh1 = (lane == i1).astype(F32)
    oh2 = (lane == i2).astype(F32)
    oh = oh1 + oh2
    r_i = lax.broadcasted_iota(jnp.int32, (tm, tm), 0)
    c_i = lax.broadcasted_iota(jnp.int32, (tm, tm), 1)
    lower = (c_i < r_i).astype(BF16)
    before = jnp.dot(lower, oh.astype(BF16), preferred_element_type=F32) + carry_scr[...]
    rank1 = jnp.sum(before * oh1, axis=-1, keepdims=True)
    rank2 = jnp.sum(before * oh2, axis=-1, keepdims=True)
    carry_scr[...] = carry_scr[...] + jnp.sum(oh, axis=0, keepdims=True)

    route = jnp.where(lane == 0, i1 - N_GROUPS, 0.0)
    route = jnp.where(lane == 1, i2 - N_GROUPS, route)
    route = jnp.where(lane == 2, w1, route)
    route = jnp.where(lane == 3, w2, route)
    route = jnp.where(lane == 4, rank1, route)
    route = jnp.where(lane == 5, rank2, route)
    route_ref[rows, :] = route


def _mix_router(pre, x, ret, att, ang, wo_bf, fg, wr, br):
    d = x.shape[1]
    lp = x.shape[0] + PREFIX
    n_x = _stream_blocks(lp)
    tm = n_x * PREFIX
    assert lp % tm == 0
    rowblk = lambda n: pl.BlockSpec((tm, n), lambda i: (i, 0))
    vec = lambda n: pl.BlockSpec((1, n), lambda i: (0, 0))
    full = lambda a: pl.BlockSpec(a.shape, lambda i: (0, 0))
    return pl.pallas_call(
        functools.partial(_mix_router_kernel, tm=tm, n_x=n_x),
        grid=(lp // tm,),
        in_specs=[pl.BlockSpec((PREFIX, d), lambda i: (0, 0)), *_stream_specs(n_x, d),
                  rowblk(ret.shape[1]), rowblk(att.shape[1]), vec(att.shape[1]),
                  pl.BlockSpec(wo_bf.shape, lambda i: (0, 0), pipeline_mode=pl.Buffered(1)),
                  vec(d), full(wr), vec(LANES)],
        out_specs=[rowblk(d), rowblk(d), rowblk(LANES), vec(LANES)],
        out_shape=[jax.ShapeDtypeStruct((lp, d), F32), jax.ShapeDtypeStruct((lp, d), F32),
                   jax.ShapeDtypeStruct((lp, LANES), F32), jax.ShapeDtypeStruct((1, LANES), F32)],
        scratch_shapes=[pltpu.VMEM((1, LANES), F32)],
        compiler_params=_params(("arbitrary",)),
        name="mix_router",
    )(pre, *([x] * n_x), ret, att, ang, wo_bf, fg, wr, br)


def _moe_kernel(ie_ref, irow_ref, irows_ref, sidx_ref, tok_ref, f_ref, wg_hbm, wu_hbm, wd_hbm, o_ref,
                x_scr, y_scr, wg_buf, wu_buf, wd_buf, gsem, ssem, wsem, *, n_items, n_chunks, chunk):
    i = pl.program_id(0)
    c = pl.program_id(1)
    n_rows = irows_ref[i]
    slot = i % 2
    w_slot = (i * n_chunks + c) % 2

    def weights(item, cc, ws, start):
        e = ie_ref[item]
        half = chunk // 2
        col0 = pl.multiple_of(cc * chunk, chunk)
        copies = (
            (wg_hbm.at[e, :, pl.ds(col0, chunk)], wg_buf.at[ws]),
            (wu_hbm.at[e, :, pl.ds(col0, chunk)], wu_buf.at[ws]),
            (wd_hbm.at[e, pl.ds(col0, half), :], wd_buf.at[ws, 0:half, :]),
            (wd_hbm.at[e, pl.ds(col0 + half, half), :], wd_buf.at[ws, half:chunk, :]),
        )

        @pl.when(irows_ref[item] > 0)
        def _():
            for n, (src, dst) in enumerate(copies):
                cp = pltpu.make_async_copy(src, dst, wsem.at[ws, n])
                cp.start(priority=n % 2) if start else cp.wait()

    @pl.when((i == 0) & (c == 0))
    def _():
        weights(0, 0, 0, True)

    @pl.when(c + 1 < n_chunks)
    def _():
        weights(i, c + 1, 1 - w_slot, True)

    @pl.when((c + 1 == n_chunks) & (i + 1 < n_items))
    def _():
        weights(i + 1, 0, 1 - w_slot, True)

    weights(i, c, w_slot, False)

    def tile_row(ref, g, u):
        return ref.at[g, pl.ds(u, 1), :]

    def group_wait(src_rows, dst_rows, sem):
        pltpu.make_async_copy(src_rows, dst_rows, sem).wait()

    def gather(item, dst_slot, start):
        row0 = irow_ref[item]
        dst = x_scr.at[dst_slot]
        sem = gsem.at[dst_slot]

        def group(g, carry):
            if start:
                for u in range(DMA_GROUP):
                    tok = tok_ref[row0 + g * DMA_GROUP + u]
                    pltpu.make_async_copy(f_ref.at[pl.ds(tok, 1), :], tile_row(dst, g, u),
                                          sem).start(priority=u % 2)
            else:
                group_wait(f_ref.at[pl.ds(0, DMA_GROUP), :], dst.at[0], sem)
            return carry

        lax.fori_loop(0, (irows_ref[item] + DMA_GROUP - 1) // DMA_GROUP, group, 0)

    def scatter(item, start):
        row0 = irow_ref[item]
        n = irows_ref[item]
        n_groups = n // DMA_GROUP

        def group(g, carry):
            if start:
                for u in range(DMA_GROUP):
                    a = sidx_ref[row0 + g * DMA_GROUP + u]
                    pltpu.make_async_copy(tile_row(y_scr, g, u), o_ref.at[pl.ds(a, 1), :],
                                          ssem).start(priority=u % 2)
            else:
                group_wait(y_scr.at[0], o_ref.at[pl.ds(0, DMA_GROUP), :], ssem)
            return carry

        lax.fori_loop(0, n_groups, group, 0)

        def single(r, carry):
            cp = pltpu.make_async_copy(y_scr.at[r // DMA_GROUP, pl.ds(r % DMA_GROUP, 1), :],
                                       o_ref.at[pl.ds(sidx_ref[row0 + r], 1), :], ssem)
            cp.start() if start else cp.wait()
            return carry

        lax.fori_loop(n_groups * DMA_GROUP, n, single, 0)

    @pl.when(c == 0)
    def _():
        @pl.when(i == 0)
        def _():
            x_scr[...] = jnp.zeros(x_scr.shape, F32)
            y_scr[...] = jnp.zeros(y_scr.shape, F32)
            gather(0, 0, True)

        @pl.when(i + 1 < n_items)
        def _():
            gather(i + 1, 1 - slot, True)

        gather(i, slot, False)

        @pl.when(i > 0)
        def _():
            scatter(i - 1, False)

    n_blocks = (n_rows + MOE_ROW_STEP - 1) // MOE_ROW_STEP
    for v in range(1, MOE_ITEM_ROWS // MOE_ROW_STEP + 1):
        @pl.when(n_blocks == v)
        def _(m=v * MOE_ROW_STEP):
            mg = m // DMA_GROUP
            d = x_scr.shape[-1]
            x = x_scr[slot, 0:mg].reshape(m, d).astype(BF16)
            gate = jnp.dot(x, wg_buf[w_slot].astype(BF16), preferred_element_type=F32)
            up = jnp.dot(x, wu_buf[w_slot].astype(BF16), preferred_element_type=F32)
            hid = ((gate * jax.nn.sigmoid(gate)) * up).astype(BF16)
            y = jnp.dot(hid, wd_buf[w_slot].astype(BF16), preferred_element_type=F32)
            y_scr[0:mg] = jnp.where(c > 0, y_scr[0:mg], 0.0) + y.reshape(mg, DMA_GROUP, d)

    @pl.when(c == n_chunks - 1)
    def _():
        scatter(i, True)

        @pl.when(i == n_items - 1)
        def _():
            scatter(i, False)


def _moe_experts(item_e, item_row, item_rows, sidx, tok_row, f, w_gate, w_up, w_down, *, chunk):
    lp, d = f.shape
    d_e = w_gate.shape[2]
    n_items = item_e.shape[0]
    n_chunks = d_e // chunk
    return pl.pallas_call(
        functools.partial(_moe_kernel, n_items=n_items, n_chunks=n_chunks, chunk=chunk),
        grid_spec=pltpu.PrefetchScalarGridSpec(
            num_scalar_prefetch=5,
            grid=(n_items, n_chunks),
            in_specs=[pl.BlockSpec(memory_space=pl.ANY)] * 4,
            out_specs=pl.BlockSpec(memory_space=pl.ANY),
            scratch_shapes=[pltpu.VMEM((2, MOE_ITEM_ROWS // DMA_GROUP, DMA_GROUP, d), F32),
                            pltpu.VMEM((MOE_ITEM_ROWS // DMA_GROUP, DMA_GROUP, d), F32),
                            pltpu.VMEM((2, d, chunk), F32),
                            pltpu.VMEM((2, d, chunk), F32),
                            pltpu.VMEM((2, chunk, d), F32),
                            pltpu.SemaphoreType.DMA((2,)),
                            pltpu.SemaphoreType.DMA(()),
                            pltpu.SemaphoreType.DMA((2, 4))],
        ),
        out_shape=jax.ShapeDtypeStruct((TOP_K * lp, d), F32),
        compiler_params=_params(("arbitrary", "arbitrary")),
        name="moe_experts",
    )(item_e, item_row, item_rows, sidx, tok_row, f, w_gate, w_up, w_down)


def _combine_kernel(h1_ref, route_ref, y0_ref, y1_ref, g_ref, o_ref):
    route = route_ref[...]
    h2 = h1_ref[...] + (route[:, 2:3] * y0_ref[0] + route[:, 3:4] * y1_ref[0])
    ms = jnp.mean(h2 * h2, axis=-1, keepdims=True)
    o_ref[...] = (h2 * lax.rsqrt(ms + NORM_EPS)) * g_ref[...]


def _combine(h1, route, y, g, *, tm):
    lp, d = h1.shape
    assert PREFIX % tm == 0
    skip = PREFIX // tm
    n_tok = lp - PREFIX
    y3 = y.reshape(TOP_K, lp, d)
    return pl.pallas_call(
        _combine_kernel,
        grid=(n_tok // tm,),
        in_specs=[pl.BlockSpec((tm, d), lambda i: (i + skip, 0)),
                  pl.BlockSpec((tm, LANES), lambda i: (i + skip, 0)),
                  pl.BlockSpec((1, tm, d), lambda i: (0, i + skip, 0)),
                  pl.BlockSpec((1, tm, d), lambda i: (1, i + skip, 0)),
                  pl.BlockSpec((1, d), lambda i: (0, 0))],
        out_specs=pl.BlockSpec((tm, d), lambda i: (i, 0)),
        out_shape=jax.ShapeDtypeStruct((n_tok, d), F32),
        compiler_params=_params(("arbitrary",)),
        name="combine",
    )(h1, route, y3, y3, g)


def _row_tile(lp, candidates):
    for t in candidates:
        if lp % t == 0:
            return t
    raise ValueError(f"no row tile for {lp}")


def kernel(x, meta_tokens, norm_mix_g, w_in, ret_decay_logit_fwd, ret_decay_logit_bwd, ret_norm_g,
           attn_q_norm_g, attn_k_norm_g, attn_out_norm_g, w_out, norm_ffn_g, router_group_w,
           router_group_b, router_expert_w, router_expert_b, expert_w_gate, expert_w_up, expert_w_down,
           norm_final_g):
    batch, n_tok, d = x.shape
    assert batch == 1 and norm_mix_g.shape[0] == 1
    lp = n_tok + PREFIX
    pre = jnp.concatenate([jnp.zeros((PAD, d), x.dtype), meta_tokens.astype(x.dtype)], axis=0)

    cr, sr = _rope_tables(n_tok, RET_QK_DIM)
    ca, sa = _rope_tables(n_tok, ATTN_HEAD_DIM)
    qk, rest = _in_proj(pre, x[0], norm_mix_g, w_in[0].astype(BF16), cr, sr, ca, sa,
                        attn_q_norm_g, attn_k_norm_g)

    log_gamma = jnp.stack([jax.nn.log_sigmoid(ret_decay_logit_fwd[0].astype(F32)),
                           jax.nn.log_sigmoid(ret_decay_logit_bwd[0].astype(F32))])
    ret = _retention(log_gamma, qk, rest, ret_norm_g)
    att = _attention(rest, tq=_row_tile(lp, (320, 128, 64)), tk=_row_tile(n_tok, (512, 256, 128)))

    n_route = N_GROUPS + N_EXPERTS
    wr = jnp.pad(jnp.concatenate([router_group_w[0], router_expert_w[0]], axis=1).astype(F32),
                 ((0, 0), (0, LANES - n_route)))
    br = jnp.pad(jnp.concatenate([router_group_b[0], router_expert_b[0]]).astype(F32),
                 (0, LANES - n_route))[None]
    wr_hi = wr.astype(BF16)
    wr_lo = (wr - wr_hi.astype(F32)).astype(BF16)
    wr = jnp.concatenate([wr_hi, wr_hi, wr_lo], axis=0)
    h1, f, route, cnt = _mix_router(pre, x[0], ret, att, attn_out_norm_g, w_out[0].astype(BF16),
                                    norm_ffn_g, wr, br)

    expert = route[:, 0:TOP_K].astype(jnp.int32)
    rank = route[:, 4:4 + TOP_K].astype(jnp.int32)
    counts = cnt[0, N_GROUPS:n_route].astype(jnp.int32)
    padded = (counts + EXPERT_BLOCK - 1) // EXPERT_BLOCK * EXPERT_BLOCK
    pad_end = jnp.cumsum(padded)
    pad_start = pad_end - padded
    is_expert = expert.T[:, :, None] == jnp.arange(N_EXPERTS, dtype=jnp.int32)
    seg_start = jnp.sum(jnp.where(is_expert, pad_start, 0), axis=-1)
    dest = (seg_start + rank.T).reshape(-1)
    n_rows = (-(-(lp * TOP_K) // EXPERT_BLOCK) + N_EXPERTS) * EXPERT_BLOCK
    sidx = jnp.full((n_rows,), -1, jnp.int32).at[dest].set(jnp.arange(TOP_K * lp, dtype=jnp.int32))
    groups = (counts + MOE_ITEM_ROWS - 1) // MOE_ITEM_ROWS
    g_end = jnp.cumsum(groups)
    g_start = g_end - groups
    n_items = N_EXPERTS + -(-(lp * TOP_K) // MOE_ITEM_ROWS)
    ii = jnp.arange(n_items, dtype=jnp.int32)
    valid = ii < g_end[-1]
    item_e = jnp.sum((g_end[None, :] <= jnp.minimum(ii, g_end[-1] - 1)[:, None]).astype(jnp.int32), axis=1)
    g_in = ii - g_start[item_e]
    item_row = jnp.where(valid, pad_start[item_e] + g_in * MOE_ITEM_ROWS, 0).astype(jnp.int32)
    item_rows = jnp.where(valid, jnp.minimum(MOE_ITEM_ROWS, counts[item_e] - g_in * MOE_ITEM_ROWS),
                          0).astype(jnp.int32)

    tok_row = jnp.where(sidx >= lp, sidx - lp, jnp.maximum(sidx, 0))
    y = _moe_experts(item_e, item_row, item_rows, sidx, tok_row, f, expert_w_gate[0], expert_w_up[0],
                     expert_w_down[0], chunk=MOE_CHUNK)
    out = _combine(h1, route, y, norm_final_g[None], tm=PREFIX)
    return out[None]
```

```python
import functools

import jax
import jax.numpy as jnp
from jax import lax
from jax.experimental import pallas as pl
from jax.experimental.pallas import tpu as pltpu

F32 = jnp.float32
BF16 = jnp.bfloat16

N_META = 16
CHUNK = 128
PREFIX = CHUNK
PAD = PREFIX - N_META
GRID_W = 64
ROPE_THETA = 10000.0
NORM_EPS = 1e-6
GROUPNORM_EPS = 1e-5

RET_HEADS = 8
RET_QK_DIM = 64
RET_V_DIM = 128
RET_UNROLL = 13
ATTN_HEADS = 8
ATTN_KV_HEADS = 2
ATTN_HEAD_DIM = 128
ATTN_GROUP = ATTN_HEADS // ATTN_KV_HEADS
ATTN_STRIP = 256
ATTN_ONES_ROWS = 16
LOG2_E = 1.4426950408889634

N_GROUPS = 8
EXPERTS_PER_GROUP = 8
N_EXPERTS = N_GROUPS * EXPERTS_PER_GROUP
TOP_K = 2
EXPERT_BLOCK = 128
MOE_ITEM_ROWS = 512
MOE_CHUNK = 512
MOE_ROW_STEP = 64
DMA_GROUP = 8

LANES = 128
PROJ_TILE = 512
PROJ_SUB_ROWS = (320, 192, 128)
MIX_SUB_BLOCKS = 2
VMEM_LIMIT = 56 * 1024 * 1024
NEG_BIG = -1e30


def _params(sem, vmem=VMEM_LIMIT):
    return pltpu.CompilerParams(dimension_semantics=sem, vmem_limit_bytes=vmem)


def _swap_halves(x, hw):
    w = x.shape[-1]
    lane = lax.broadcasted_iota(jnp.int32, x.shape, 1)
    first = (lane & hw) == 0
    return jnp.where(first, pltpu.roll(x, w - hw, 1), pltpu.roll(x, hw, 1))


def _rope_tables(n_tok, head_dim):
    half = head_dim // 2
    freqs = ROPE_THETA ** (-jnp.arange(0, half, 2, dtype=F32) / half)
    rows = n_tok // GRID_W
    ang_r = jnp.arange(rows, dtype=F32)[:, None] * freqs[None, :]
    ang_c = jnp.arange(GRID_W, dtype=F32)[:, None] * freqs[None, :]

    def per_token(tab_r, tab_c, prefix_value):
        pre = jnp.full((PREFIX, tab_r.shape[1]), prefix_value, F32)
        return (jnp.concatenate([pre, jnp.repeat(tab_r, GRID_W, axis=0)], axis=0),
                jnp.concatenate([pre, jnp.tile(tab_c, (rows, 1))], axis=0))

    cos_r, cos_c = per_token(jnp.cos(ang_r), jnp.cos(ang_c), 1.0)
    sin_r, sin_c = per_token(jnp.sin(ang_r), jnp.sin(ang_c), 0.0)
    cos = jnp.concatenate([cos_r, cos_r, cos_c, cos_c], axis=-1)
    sin = jnp.concatenate([-sin_r, sin_r, -sin_c, sin_c], axis=-1)
    reps = LANES // head_dim
    return jnp.tile(cos, (1, reps)), jnp.tile(sin, (1, reps))


def _stream_specs(n_blocks, d):
    return [pl.BlockSpec((PREFIX, d), lambda i, j=j: (jnp.maximum(n_blocks * i - 1 + j, 0), 0))
            for j in range(n_blocks)]


def _stream_rows(pre_ref, x_refs):
    first = jnp.where(pl.program_id(0) == 0, pre_ref[...], x_refs[0][...])
    return jnp.concatenate([first] + [r[...] for r in x_refs[1:]], axis=0)


def _stream_blocks(lp):
    n_blocks = lp // PREFIX
    return max(k for k in (5, 3, 1) if n_blocks % k == 0)


def _in_proj_kernel(*refs, tm, n_x):
    pre_ref, x_refs = refs[0], refs[1:1 + n_x]
    (g_ref, w_ref, cr_ref, sr_ref, ca_ref, sa_ref, qg_ref, kg_ref,
     oqk_ref, orest_ref, a_scr) = refs[1 + n_x:]
    i = pl.program_id(0)
    x = _stream_rows(pre_ref, x_refs)
    ms = jnp.mean(x * x, axis=-1, keepdims=True)
    a_scr[...] = ((x * lax.rsqrt(ms + NORM_EPS)) * g_ref[...]).astype(BF16)
    n_lane_groups = PROJ_TILE // LANES
    ret_tiles = 2
    plain_tiles = (2 * RET_HEADS * RET_V_DIM) // PROJ_TILE
    q_tiles = (ATTN_HEADS * ATTN_HEAD_DIM) // PROJ_TILE

    def tiled(ref):
        return jnp.concatenate([ref[...]] * n_lane_groups, axis=1)

    ts = max(t for t in PROJ_SUB_ROWS if tm % t == 0)
    lo = lax.broadcasted_iota(jnp.int32, (ts, LANES), 1) < RET_QK_DIM

    for r0 in range(0, tm, ts):
        rows = slice(r0, r0 + ts)
        row = i * tm + r0 + lax.broadcasted_iota(jnp.int32, (ts, 1), 0)
        k_scale = jnp.where(row >= PAD, RET_QK_DIM ** -0.5, 0.0).astype(F32)
        cos_r, sin_r = tiled(cr_ref.at[rows, :]), tiled(sr_ref.at[rows, :])
        cos_a, sin_a = ca_ref[rows, :], sa_ref[rows, :]

        def qk_norm_rope(xg, gain_ref):
            ms_h = jnp.mean(xg * xg, axis=-1, keepdims=True)
            xn = (xg * lax.rsqrt(ms_h + NORM_EPS)) * gain_ref[...]
            return xn * cos_a + _swap_halves(xn, ATTN_HEAD_DIM // 4) * sin_a

        for j in range(w_ref.shape[1] // PROJ_TILE):
            acc = jnp.dot(a_scr[rows, :], w_ref[:, j * PROJ_TILE:(j + 1) * PROJ_TILE],
                          preferred_element_type=F32)
            if j < ret_tiles:
                y = acc * cos_r + _swap_halves(acc, RET_QK_DIM // 4) * sin_r
                if j == 1:
                    y = y * k_scale
                pieces = []
                for c in range(n_lane_groups):
                    xg = y[:, c * LANES:(c + 1) * LANES]
                    xr = pltpu.roll(xg, RET_QK_DIM, 1)
                    pieces.append(jnp.where(lo, xg, xr))
                    pieces.append(jnp.where(lo, xr, xg))
                oqk_ref[rows, 2 * j * PROJ_TILE:2 * (j + 1) * PROJ_TILE] = (
                    jnp.concatenate(pieces, axis=1).astype(BF16))
                continue
            jr = j - ret_tiles
            if jr < plain_tiles:
                out = acc
            elif jr < plain_tiles + q_tiles:
                out = jnp.concatenate(
                    [qk_norm_rope(acc[:, c * LANES:(c + 1) * LANES], qg_ref)
                     * (LOG2_E * ATTN_HEAD_DIM ** -0.5) for c in range(n_lane_groups)], axis=1)
            else:
                pieces = [qk_norm_rope(acc[:, c * LANES:(c + 1) * LANES], kg_ref)
                          for c in range(ATTN_KV_HEADS)]
                pieces.append(acc[:, ATTN_KV_HEADS * LANES:])
                out = jnp.concatenate(pieces, axis=1)
            orest_ref[rows, jr * PROJ_TILE:(jr + 1) * PROJ_TILE] = out.astype(BF16)


def _in_proj(pre, x, g, w_bf, cr, sr, ca, sa, qg, kg):
    d = x.shape[1]
    lp = x.shape[0] + PREFIX
    n_x = _stream_blocks(lp)
    tm = n_x * PREFIX
    n_out = w_bf.shape[1]
    assert lp % tm == 0 and n_out == 9 * PROJ_TILE
    qk_w = 2 * 2 * RET_HEADS * RET_QK_DIM
    rest_w = n_out - 2 * RET_HEADS * RET_QK_DIM
    row_tab = pl.BlockSpec((tm, LANES), lambda i: (i, 0))
    vec = lambda n: pl.BlockSpec((1, n), lambda i: (0, 0))
    return pl.pallas_call(
        functools.partial(_in_proj_kernel, tm=tm, n_x=n_x),
        grid=(lp // tm,),
        in_specs=[
            pl.BlockSpec((PREFIX, d), lambda i: (0, 0)),
            *_stream_specs(n_x, d),
            vec(d),
            pl.BlockSpec((d, n_out), lambda i: (0, 0), pipeline_mode=pl.Buffered(1)),
            row_tab, row_tab, row_tab, row_tab,
            vec(LANES), vec(LANES),
        ],
        out_specs=[
            pl.BlockSpec((tm, qk_w), lambda i: (i, 0)),
            pl.BlockSpec((tm, rest_w), lambda i: (i, 0)),
        ],
        out_shape=[jax.ShapeDtypeStruct((lp, qk_w), BF16), jax.ShapeDtypeStruct((lp, rest_w), BF16)],
        scratch_shapes=[pltpu.VMEM((tm, d), BF16)],
        compiler_params=_params(("arbitrary",)),
        name="in_proj",
    )(pre, *([x] * n_x), g, w_bf, cr, sr, ca, sa, qg, kg)


def _retention_kernel(lg_ref, q_ref, k_ref, v_ref, g_ref, gn_ref, o_ref, st_ref, dk_ref, dq_ref, dm_ref,
                      *, n_chunks):
    h = pl.program_id(0)
    lgf = lg_ref[0, h]
    lgb = lg_ref[1, h]
    half = CHUNK // 2
    row = lax.broadcasted_iota(jnp.int32, (CHUNK, CHUNK), 0).astype(F32)
    lane = lax.broadcasted_iota(jnp.int32, (CHUNK, CHUNK), 1).astype(F32)
    lane_lo = lane < half
    dk_ref[...] = jnp.exp(jnp.where(lane_lo, lgf * (CHUNK - 1.0 - row), lgb * row))
    dq_ref[...] = jnp.exp(jnp.where(lane_lo, lgf * (row + 1.0), lgb * (CHUNK - row)))
    diff = row - lane
    dm_ref[...] = 0.5 * jnp.where(diff >= 0, jnp.exp(lgf * jnp.maximum(diff, 0.0)),
                                  jnp.exp(lgb * jnp.maximum(-diff, 0.0)))
    cf = jnp.exp(jnp.full((half, CHUNK), lgf * CHUNK, F32))
    cb = jnp.exp(jnp.full((half, CHUNK), lgb * CHUNK, F32))

    def rows(n):
        return pl.ds(pl.multiple_of(n * CHUNK, CHUNK), CHUNK)

    def updates(n, c):
        kd = k_ref[rows(n), :].astype(F32) * dk_ref[...]
        st_ref[n] = jnp.dot(kd.T.astype(BF16), v_ref[rows(n), :], preferred_element_type=F32)
        return c

    unroll = max(u for u in range(1, RET_UNROLL + 1) if n_chunks % u == 0)

    def unrolled(fn):
        def body(t, c):
            for u in range(unroll):
                fn(t * unroll + u, c)
            return c
        return body

    lax.fori_loop(0, n_chunks // unroll, unrolled(updates), 0)

    def scan_fwd(n, s):
        u = st_ref[n, 0:half, :]
        st_ref[n, 0:half, :] = s
        return cf * s + u

    lax.fori_loop(0, n_chunks, scan_fwd, jnp.zeros((half, CHUNK), F32))

    def scan_bwd(t, s):
        n = n_chunks - 1 - t
        u = st_ref[n, half:CHUNK, :]
        st_ref[n, half:CHUNK, :] = s
        return cb * s + u

    lax.fori_loop(0, n_chunks, scan_bwd, jnp.zeros((half, CHUNK), F32))

    def outputs(n, c):
        q = q_ref[rows(n), :]
        k = k_ref[rows(n), :]
        v = v_ref[rows(n), :]
        s2 = lax.dot_general(q, k, (((1,), (1,)), ((), ())), preferred_element_type=F32)
        intra = jnp.dot((s2 * dm_ref[...]).astype(BF16), v, preferred_element_type=F32)
        qd = (q.astype(F32) * dq_ref[...]).astype(BF16)
        cross = jnp.dot(qd, st_ref[n].astype(BF16), preferred_element_type=F32)
        o = intra + cross
        mu = jnp.mean(o, axis=-1, keepdims=True)
        dlt = o - mu
        var = jnp.mean(dlt * dlt, axis=-1, keepdims=True)
        on = dlt * lax.rsqrt(var + GROUPNORM_EPS)
        gate = g_ref[rows(n), :].astype(F32)
        o_ref[rows(n), :] = ((gate * jax.nn.sigmoid(gate)) * (on * gn_ref[...])).astype(BF16)
        return c

    lax.fori_loop(0, n_chunks // unroll, unrolled(outputs), 0)


def _retention(log_gamma, qk, rest, gn_g):
    lp = qk.shape[0]
    n_chunks = lp // CHUNK
    col = lambda off: pl.BlockSpec((lp, LANES), lambda h, lg: (0, h + off))
    return pl.pallas_call(
        functools.partial(_retention_kernel, n_chunks=n_chunks),
        grid_spec=pltpu.PrefetchScalarGridSpec(
            num_scalar_prefetch=1,
            grid=(RET_HEADS,),
            in_specs=[col(0), col(RET_HEADS), col(0), col(RET_HEADS),
                      pl.BlockSpec((1, LANES), lambda h, lg: (0, h))],
            out_specs=pl.BlockSpec((lp, LANES), lambda h, lg: (0, h)),
            scratch_shapes=[pltpu.VMEM((n_chunks, CHUNK, CHUNK), F32),
                            pltpu.VMEM((CHUNK, CHUNK), F32),
                            pltpu.VMEM((CHUNK, CHUNK), F32),
                            pltpu.VMEM((CHUNK, CHUNK), F32)],
        ),
        out_shape=jax.ShapeDtypeStruct((lp, RET_HEADS * RET_V_DIM), BF16),
        compiler_params=_params(("arbitrary",)),
        name="retention",
    )(log_gamma, qk, qk, rest, rest, gn_g)


def _attention_kernel(q_ref, k_ref, vt_ref, km_ref, vtm_ref, o_ref, s_scr, mx_scr, m_scr, acc_scr,
                      *, tq, tk, n_kc):
    q_all = jnp.concatenate([q_ref[:, g * LANES:(g + 1) * LANES] for g in range(ATTN_GROUP)], axis=0)
    n_strips = (ATTN_GROUP * tq) // ATTN_STRIP
    q_strips = [q_all[s * ATTN_STRIP:(s + 1) * ATTN_STRIP, :] for s in range(n_strips)]

    def scores(k, s):
        return lax.dot_general(k, q_strips[s], (((1,), (1,)), ((), ())), preferred_element_type=F32)

    def key_rows(c):
        return pl.ds(pl.multiple_of(PREFIX + c * tk, LANES), tk)

    def score_stage(c, slot):
        k = k_ref[key_rows(c), :]
        for s in range(n_strips):
            st = scores(k, s)
            s_scr[slot, s] = st
            mx_scr[slot, s] = jnp.max(st, axis=0, keepdims=True)

    def value_stage(c, slot):
        vt = vt_ref[0, :, key_rows(c)]
        for s in range(n_strips):
            m_old = m_scr[s]
            m_new = jnp.maximum(m_old, mx_scr[slot, s])
            p = jnp.exp2(s_scr[slot, s] - m_new).astype(BF16)
            acc_scr[s] = jnp.exp2(m_old - m_new) * acc_scr[s] + jnp.dot(vt, p, preferred_element_type=F32)
            m_scr[s] = m_new

    for s in range(n_strips):
        st = scores(km_ref[0], s)
        m0 = jnp.max(st, axis=0, keepdims=True)
        m_scr[s] = m0
        acc_scr[s] = jnp.dot(vtm_ref[0], jnp.exp2(st - m0).astype(BF16), preferred_element_type=F32)

    score_stage(0, 0)

    def body(i, carry):
        c = 2 * i
        score_stage(c + 1, 1)
        value_stage(c, 0)
        score_stage(c + 2, 0)
        value_stage(c + 1, 1)
        return carry

    lax.fori_loop(0, n_kc // 2 - 1, body, 0)
    score_stage(n_kc - 1, 1)
    value_stage(n_kc - 2, 0)
    value_stage(n_kc - 1, 1)
    outs = []
    for s in range(n_strips):
        acc = acc_scr[s]
        outs.append((acc[0:ATTN_HEAD_DIM, :] / acc[ATTN_HEAD_DIM:ATTN_HEAD_DIM + 1, :]).T)
    o = jnp.concatenate(outs, axis=0)
    for g in range(ATTN_GROUP):
        o_ref[:, g * LANES:(g + 1) * LANES] = o[g * tq:(g + 1) * tq, :]


def _attention(rest, *, tq, tk):
    lp = rest.shape[0]
    n_tok = lp - PREFIX
    assert lp % tq == 0 and n_tok % (2 * tk) == 0 and (ATTN_GROUP * tq) % ATTN_STRIP == 0
    gw = ATTN_GROUP * ATTN_HEAD_DIM
    q_off = (2 * RET_HEADS * RET_V_DIM) // gw
    k_col = 2 * RET_HEADS * RET_V_DIM + ATTN_HEADS * ATTN_HEAD_DIM
    k_off = k_col // LANES
    kv_w = ATTN_KV_HEADS * ATTN_HEAD_DIM
    vt = rest[:, k_col + kv_w:k_col + 2 * kv_w].T.reshape(ATTN_KV_HEADS, ATTN_HEAD_DIM, lp)
    vt = jnp.concatenate([vt, jnp.ones((ATTN_KV_HEADS, ATTN_ONES_ROWS, lp), BF16)], axis=1)
    vt_meta = vt[:, :, PAD:PREFIX]
    k_meta = rest[PAD:PREFIX, k_col:k_col + kv_w].reshape(N_META, ATTN_KV_HEADS, ATTN_HEAD_DIM)
    k_meta = jnp.swapaxes(k_meta, 0, 1)
    n_strips = (ATTN_GROUP * tq) // ATTN_STRIP
    vt_rows = ATTN_HEAD_DIM + ATTN_ONES_ROWS
    return pl.pallas_call(
        functools.partial(_attention_kernel, tq=tq, tk=tk, n_kc=n_tok // tk),
        grid=(ATTN_KV_HEADS, lp // tq),
        in_specs=[pl.BlockSpec((tq, gw), lambda kv, i: (i, q_off + kv)),
                  pl.BlockSpec((lp, LANES), lambda kv, i: (0, k_off + kv)),
                  pl.BlockSpec((1, vt_rows, lp), lambda kv, i: (kv, 0, 0)),
                  pl.BlockSpec((1, N_META, ATTN_HEAD_DIM), lambda kv, i: (kv, 0, 0)),
                  pl.BlockSpec((1, vt_rows, N_META), lambda kv, i: (kv, 0, 0))],
        out_specs=pl.BlockSpec((tq, gw), lambda kv, i: (i, kv)),
        out_shape=jax.ShapeDtypeStruct((lp, ATTN_HEADS * ATTN_HEAD_DIM), F32),
        scratch_shapes=[pltpu.VMEM((2, n_strips, tk, ATTN_STRIP), F32),
                        pltpu.VMEM((2, n_strips, 1, ATTN_STRIP), F32),
                        pltpu.VMEM((n_strips, 1, ATTN_STRIP), F32),
                        pltpu.VMEM((n_strips, vt_rows, ATTN_STRIP), F32)],
        compiler_params=_params(("arbitrary", "arbitrary")),
        name="attention",
    )(rest, rest, vt, k_meta, vt_meta)


def _mix_router_kernel(*refs, tm, n_x):
    pre_ref, x_refs = refs[0], refs[1:1 + n_x]
    (ret_ref, att_ref, ang_ref, wo_ref, fg_ref, wr_ref, br_ref,
     h1_ref, f_ref, route_ref, cnt_ref, carry_scr) = refs[1 + n_x:]
    i = pl.program_id(0)

    @pl.when(i == 0)
    def _():
        carry_scr[...] = jnp.zeros(carry_scr.shape, F32)

    blocks = [jnp.where(i == 0, pre_ref[...], x_refs[0][...])] + [r[...] for r in x_refs[1:]]
    for b0 in range(0, n_x, MIX_SUB_BLOCKS):
        b1 = min(b0 + MIX_SUB_BLOCKS, n_x)
        h_rows = blocks[b0] if b1 - b0 == 1 else jnp.concatenate(blocks[b0:b1], axis=0)
        _mix_router_rows(slice(b0 * PREFIX, b1 * PREFIX), (b1 - b0) * PREFIX, h_rows, ret_ref, att_ref,
                         ang_ref, wo_ref, fg_ref, wr_ref, br_ref, h1_ref, f_ref, route_ref, carry_scr)
    cnt_ref[...] = carry_scr[...]


def _mix_router_rows(rows, tm, h_rows, ret_ref, att_ref, ang_ref, wo_ref, fg_ref, wr_ref, br_ref,
                     h1_ref, f_ref, route_ref, carry_scr):
    att = att_ref[rows, :]
    ms = jnp.mean(att * att, axis=-1, keepdims=True)
    att_n = ((att * lax.rsqrt(ms + NORM_EPS)) * ang_ref[...]).astype(BF16)
    mix_in = jnp.concatenate([ret_ref[rows, :], att_n], axis=1)
    h1 = h_rows + jnp.dot(mix_in, wo_ref[...], preferred_element_type=F32)
    h1_ref[rows, :] = h1
    ms1 = jnp.mean(h1 * h1, axis=-1, keepdims=True)
    f = (h1 * lax.rsqrt(ms1 + NORM_EPS)) * fg_ref[...]
    f_ref[rows, :] = f

    f_hi = f.astype(BF16)
    f_lo = (f - f_hi.astype(F32)).astype(BF16)
    d = f.shape[1]
    k_half = 3 * d // 2
    logits = (jnp.dot(jnp.concatenate([f_hi, f_lo[:, :d // 2]], axis=1), wr_ref[0:k_half, :],
                      preferred_element_type=F32)
              + jnp.dot(jnp.concatenate([f_lo[:, d // 2:], f_hi], axis=1), wr_ref[k_half:, :],
                        preferred_element_type=F32)
              + br_ref[...])
    lane = lax.broadcasted_iota(jnp.int32, (tm, LANES), 1).astype(F32)
    far = float(LANES)
    g_logit = jnp.where(lane < N_GROUPS, logits, NEG_BIG)
    g_max = jnp.max(g_logit, axis=-1, keepdims=True)
    g_w = 1.0 / jnp.sum(jnp.exp(g_logit - g_max), axis=-1, keepdims=True)
    g_idx = jnp.min(jnp.where(g_logit == g_max, lane, far), axis=-1, keepdims=True)
    first = N_GROUPS + g_idx * EXPERTS_PER_GROUP
    e_logit = jnp.where((lane >= first) & (lane < first + EXPERTS_PER_GROUP), logits, NEG_BIG)
    v1 = jnp.max(e_logit, axis=-1, keepdims=True)
    i1 = jnp.min(jnp.where(e_logit == v1, lane, far), axis=-1, keepdims=True)
    e_rest = jnp.where(lane == i1, NEG_BIG, e_logit)
    v2 = jnp.max(e_rest, axis=-1, keepdims=True)
    i2 = jnp.min(jnp.where(e_rest == v2, lane, far), axis=-1, keepdims=True)
    e2 = jnp.exp(v2 - v1)
    w1 = g_w / (1.0 + e2)
    w2 = g_w * e2 / (1.0 + e2)

    oh1 = (lane == i1).astype(F32)
    oh2 = (lane == i2).astype(F32)
    oh = oh1 + oh2
    r_i = lax.broadcasted_iota(jnp.int32, (tm, tm), 0)
    c_i = lax.broadcasted_iota(jnp.int32, (tm, tm), 1)
    lower = (c_i < r_i).astype(BF16)
    before = jnp.dot(lower, oh.astype(BF16), preferred_element_type=F32) + carry_scr[...]
    rank1 = jnp.sum(before * oh1, axis=-1, keepdims=True)
    rank2 = jnp.sum(before * oh2, axis=-1, keepdims=True)
    carry_scr[...] = carry_scr[...] + jnp.sum(oh, axis=0, keepdims=True)

    route = jnp.where(lane == 0, i1 - N_GROUPS, 0.0)
    route = jnp.where(lane == 1, i2 - N_GROUPS, route)
    route = jnp.where(lane == 2, w1, route)
    route = jnp.where(lane == 3, w2, route)
    route = jnp.where(lane == 4, rank1, route)
    route = jnp.where(lane == 5, rank2, route)
    route_ref[rows, :] = route


def _mix_router(pre, x, ret, att, ang, wo_bf, fg, wr, br):
    d = x.shape[1]
    lp = x.shape[0] + PREFIX
    n_x = _stream_blocks(lp)
    tm = n_x * PREFIX
    assert lp % tm == 0
    rowblk = lambda n: pl.BlockSpec((tm, n), lambda i: (i, 0))
    vec = lambda n: pl.BlockSpec((1, n), lambda i: (0, 0))
    full = lambda a: pl.BlockSpec(a.shape, lambda i: (0, 0))
    return pl.pallas_call(
        functools.partial(_mix_router_kernel, tm=tm, n_x=n_x),
        grid=(lp // tm,),
        in_specs=[pl.BlockSpec((PREFIX, d), lambda i: (0, 0)), *_stream_specs(n_x, d),
                  rowblk(ret.shape[1]), rowblk(att.shape[1]), vec(att.shape[1]),
                  pl.BlockSpec(wo_bf.shape, lambda i: (0, 0), pipeline_mode=pl.Buffered(1)),
                  vec(d), full(wr), vec(LANES)],
        out_specs=[rowblk(d), rowblk(d), rowblk(LANES), vec(LANES)],
        out_shape=[jax.ShapeDtypeStruct((lp, d), F32), jax.ShapeDtypeStruct((lp, d), F32),
                   jax.ShapeDtypeStruct((lp, LANES), F32), jax.ShapeDtypeStruct((1, LANES), F32)],
        scratch_shapes=[pltpu.VMEM((1, LANES), F32)],
        compiler_params=_params(("arbitrary",)),
        name="mix_router",
    )(pre, *([x] * n_x), ret, att, ang, wo_bf, fg, wr, br)


def _moe_kernel(ie_ref, irow_ref, irows_ref, sidx_ref, tok_ref, f_ref, wg_hbm, wu_hbm, wd_hbm, o_ref,
                x_scr, y_scr, wg_buf, wu_buf, wd_buf, gsem, ssem, wsem, *, n_items, n_chunks, chunk):
    i = pl.program_id(0)
    c = pl.program_id(1)
    n_rows = irows_ref[i]
    slot = i % 2
    w_slot = (i * n_chunks + c) % 2

    def weights(item, cc, ws, start):
        e = ie_ref[item]
        half = chunk // 2
        col0 = pl.multiple_of(cc * chunk, chunk)
        copies = (
            (wg_hbm.at[e, :, pl.ds(col0, chunk)], wg_buf.at[ws]),
            (wu_hbm.at[e, :, pl.ds(col0, chunk)], wu_buf.at[ws]),
            (wd_hbm.at[e, pl.ds(col0, half), :], wd_buf.at[ws, 0:half, :]),
            (wd_hbm.at[e, pl.ds(col0 + half, half), :], wd_buf.at[ws, half:chunk, :]),
        )

        @pl.when(irows_ref[item] > 0)
        def _():
            for n, (src, dst) in enumerate(copies):
                cp = pltpu.make_async_copy(src, dst, wsem.at[ws, n])
                cp.start(priority=n % 2) if start else cp.wait()

    @pl.when((i == 0) & (c == 0))
    def _():
        weights(0, 0, 0, True)

    @pl.when(c + 1 < n_chunks)
    def _():
        weights(i, c + 1, 1 - w_slot, True)

    @pl.when((c + 1 == n_chunks) & (i + 1 < n_items))
    def _():
        weights(i + 1, 0, 1 - w_slot, True)

    weights(i, c, w_slot, False)

    def tile_row(ref, g, u):
        return ref.at[g, pl.ds(u, 1), :]

    def group_wait(src_rows, dst_rows, sem):
        pltpu.make_async_copy(src_rows, dst_rows, sem).wait()

    def gather(item, dst_slot, start):
        row0 = irow_ref[item]
        dst = x_scr.at[dst_slot]
        sem = gsem.at[dst_slot]

        def group(g, carry):
            if start:
                for u in range(DMA_GROUP):
                    tok = tok_ref[row0 + g * DMA_GROUP + u]
                    pltpu.make_async_copy(f_ref.at[pl.ds(tok, 1), :], tile_row(dst, g, u),
                                          sem).start(priority=u % 2)
            else:
                group_wait(f_ref.at[pl.ds(0, DMA_GROUP), :], dst.at[0], sem)
            return carry

        lax.fori_loop(0, (irows_ref[item] + DMA_GROUP - 1) // DMA_GROUP, group, 0)

    def scatter(item, start):
        row0 = irow_ref[item]
        n = irows_ref[item]
        n_groups = n // DMA_GROUP

        def group(g, carry):
            if start:
                for u in range(DMA_GROUP):
                    a = sidx_ref[row0 + g * DMA_GROUP + u]
                    pltpu.make_async_copy(tile_row(y_scr, g, u), o_ref.at[pl.ds(a, 1), :],
                                          ssem).start(priority=u % 2)
            else:
                group_wait(y_scr.at[0], o_ref.at[pl.ds(0, DMA_GROUP), :], ssem)
            return carry

        lax.fori_loop(0, n_groups, group, 0)

        def single(r, carry):
            cp = pltpu.make_async_copy(y_scr.at[r // DMA_GROUP, pl.ds(r % DMA_GROUP, 1), :],
                                       o_ref.at[pl.ds(sidx_ref[row0 + r], 1), :], ssem)
            cp.start() if start else cp.wait()
            return carry

        lax.fori_loop(n_groups * DMA_GROUP, n, single, 0)

    @pl.when(c == 0)
    def _():
        @pl.when(i == 0)
        def _():
            x_scr[...] = jnp.zeros(x_scr.shape, F32)
            y_scr[...] = jnp.zeros(y_scr.shape, F32)
            gather(0, 0, True)

        @pl.when(i + 1 < n_items)
        def _():
            gather(i + 1, 1 - slot, True)

        gather(i, slot, False)

        @pl.when(i > 0)
        def _():
            scatter(i - 1, False)

    n_blocks = (n_rows + MOE_ROW_STEP - 1) // MOE_ROW_STEP
    for v in range(1, MOE_ITEM_ROWS // MOE_ROW_STEP + 1):
        @pl.when(n_blocks == v)
        def _(m=v * MOE_ROW_STEP):
            mg = m // DMA_GROUP
            d = x_scr.shape[-1]
            x = x_scr[slot, 0:mg].reshape(m, d).astype(BF16)
            gate = jnp.dot(x, wg_buf[w_slot].astype(BF16), preferred_element_type=F32)
            up = jnp.dot(x, wu_buf[w_slot].astype(BF16), preferred_element_type=F32)
            hid = ((gate * jax.nn.sigmoid(gate)) * up).astype(BF16)
            y = jnp.dot(hid, wd_buf[w_slot].astype(BF16), preferred_element_type=F32)
            y_scr[0:mg] = jnp.where(c > 0, y_scr[0:mg], 0.0) + y.reshape(mg, DMA_GROUP, d)

    @pl.when(c == n_chunks - 1)
    def _():
        scatter(i, True)

        @pl.when(i == n_items - 1)
        def _():
            scatter(i, False)


def _moe_experts(item_e, item_row, item_rows, sidx, tok_row, f, w_gate, w_up, w_down, *, chunk):
    lp, d = f.shape
    d_e = w_gate.shape[2]
    n_items = item_e.shape[0]
    n_chunks = d_e // chunk
    return pl.pallas_call(
        functools.partial(_moe_kernel, n_items=n_items, n_chunks=n_chunks, chunk=chunk),
        grid_spec=pltpu.PrefetchScalarGridSpec(
            num_scalar_prefetch=5,
            grid=(n_items, n_chunks),
            in_specs=[pl.BlockSpec(memory_space=pl.ANY)] * 4,
            out_specs=pl.BlockSpec(memory_space=pl.ANY),
            scratch_shapes=[pltpu.VMEM((2, MOE_ITEM_ROWS // DMA_GROUP, DMA_GROUP, d), F32),
                            pltpu.VMEM((MOE_ITEM_ROWS // DMA_GROUP, DMA_GROUP, d), F32),
                            pltpu.VMEM((2, d, chunk), F32),
                            pltpu.VMEM((2, d, chunk), F32),
                            pltpu.VMEM((2, chunk, d), F32),
                            pltpu.SemaphoreType.DMA((2,)),
                            pltpu.SemaphoreType.DMA(()),
                            pltpu.SemaphoreType.DMA((2, 4))],
        ),
        out_shape=jax.ShapeDtypeStruct((TOP_K * lp, d), F32),
        compiler_params=_params(("arbitrary", "arbitrary")),
        name="moe_experts",
    )(item_e, item_row, item_rows, sidx, tok_row, f, w_gate, w_up, w_down)


def _combine_kernel(h1_ref, route_ref, y0_ref, y1_ref, g_ref, o_ref):
    route = route_ref[...]
    h2 = h1_ref[...] + (route[:, 2:3] * y0_ref[0] + route[:, 3:4] * y1_ref[0])
    ms = jnp.mean(h2 * h2, axis=-1, keepdims=True)
    o_ref[...] = (h2 * lax.rsqrt(ms + NORM_EPS)) * g_ref[...]


def _combine(h1, route, y, g, *, tm):
    lp, d = h1.shape
    assert PREFIX % tm == 0
    skip = PREFIX // tm
    n_tok = lp - PREFIX
    y3 = y.reshape(TOP_K, lp, d)
    return pl.pallas_call(
        _combine_kernel,
        grid=(n_tok // tm,),
        in_specs=[pl.BlockSpec((tm, d), lambda i: (i + skip, 0)),
                  pl.BlockSpec((tm, LANES), lambda i: (i + skip, 0)),
                  pl.BlockSpec((1, tm, d), lambda i: (0, i + skip, 0)),
                  pl.BlockSpec((1, tm, d), lambda i: (1, i + skip, 0)),
                  pl.BlockSpec((1, d), lambda i: (0, 0))],
        out_specs=pl.BlockSpec((tm, d), lambda i: (i, 0)),
        out_shape=jax.ShapeDtypeStruct((n_tok, d), F32),
        compiler_params=_params(("arbitrary",)),
        name="combine",
    )(h1, route, y3, y3, g)


def _row_tile(lp, candidates):
    for t in candidates:
        if lp % t == 0:
            return t
    raise ValueError(f"no row tile for {lp}")


def kernel(x, meta_tokens, norm_mix_g, w_in, ret_decay_logit_fwd, ret_decay_logit_bwd, ret_norm_g,
           attn_q_norm_g, attn_k_norm_g, attn_out_norm_g, w_out, norm_ffn_g, router_group_w,
           router_group_b, router_expert_w, router_expert_b, expert_w_gate, expert_w_up, expert_w_down,
           norm_final_g):
    batch, n_tok, d = x.shape
    assert batch == 1 and norm_mix_g.shape[0] == 1
    lp = n_tok + PREFIX
    pre = jnp.concatenate([jnp.zeros((PAD, d), x.dtype), meta_tokens.astype(x.dtype)], axis=0)

    cr, sr = _rope_tables(n_tok, RET_QK_DIM)
    ca, sa = _rope_tables(n_tok, ATTN_HEAD_DIM)
    qk, rest = _in_proj(pre, x[0], norm_mix_g, w_in[0].astype(BF16), cr, sr, ca, sa,
                        attn_q_norm_g, attn_k_norm_g)

    log_gamma = jnp.stack([jax.nn.log_sigmoid(ret_decay_logit_fwd[0].astype(F32)),
                           jax.nn.log_sigmoid(ret_decay_logit_bwd[0].astype(F32))])
    ret = _retention(log_gamma, qk, rest, ret_norm_g)
    att = _attention(rest, tq=_row_tile(lp, (320, 128, 64)), tk=_row_tile(n_tok // 2, (1024, 512, 256, 128)))

    n_route = N_GROUPS + N_EXPERTS
    wr = jnp.pad(jnp.concatenate([router_group_w[0], router_expert_w[0]], axis=1).astype(F32),
                 ((0, 0), (0, LANES - n_route)))
    br = jnp.pad(jnp.concatenate([router_group_b[0], router_expert_b[0]]).astype(F32),
                 (0, LANES - n_route))[None]
    wr_hi = wr.astype(BF16)
    wr_lo = (wr - wr_hi.astype(F32)).astype(BF16)
    wr = jnp.concatenate([wr_hi, wr_hi, wr_lo], axis=0)
    h1, f, route, cnt = _mix_router(pre, x[0], ret, att, attn_out_norm_g, w_out[0].astype(BF16),
                                    norm_ffn_g, wr, br)

    expert = route[:, 0:TOP_K].astype(jnp.int32)
    rank = route[:, 4:4 + TOP_K].astype(jnp.int32)
    counts = cnt[0, N_GROUPS:n_route].astype(jnp.int32)
    padded = (counts + EXPERT_BLOCK - 1) // EXPERT_BLOCK * EXPERT_BLOCK
    pad_end = jnp.cumsum(padded)
    pad_start = pad_end - padded
    is_expert = expert.T[:, :, None] == jnp.arange(N_EXPERTS, dtype=jnp.int32)
    seg_start = jnp.sum(jnp.where(is_expert, pad_start, 0), axis=-1)
    dest = (seg_start + rank.T).reshape(-1)
    n_rows = (-(-(lp * TOP_K) // EXPERT_BLOCK) + N_EXPERTS) * EXPERT_BLOCK
    sidx = jnp.full((n_rows,), -1, jnp.int32).at[dest].set(jnp.arange(TOP_K * lp, dtype=jnp.int32))
    groups = (counts + MOE_ITEM_ROWS - 1) // MOE_ITEM_ROWS
    g_end = jnp.cumsum(groups)
    g_start = g_end - groups
    n_items = N_EXPERTS + -(-(lp * TOP_K) // MOE_ITEM_ROWS)
    ii = jnp.arange(n_items, dtype=jnp.int32)
    valid = ii < g_end[-1]
    item_e = jnp.sum((g_end[None, :] <= jnp.minimum(ii, g_end[-1] - 1)[:, None]).astype(jnp.int32), axis=1)
    g_in = ii - g_start[item_e]
    item_row = jnp.where(valid, pad_start[item_e] + g_in * MOE_ITEM_ROWS, 0).astype(jnp.int32)
    item_rows = jnp.where(valid, jnp.minimum(MOE_ITEM_ROWS, counts[item_e] - g_in * MOE_ITEM_ROWS),
                          0).astype(jnp.int32)

    tok_row = jnp.where(sidx >= lp, sidx - lp, jnp.maximum(sidx, 0))
    y = _moe_experts(item_e, item_row, item_rows, sidx, tok_row, f, expert_w_gate[0], expert_w_up[0],
                     expert_w_down[0], chunk=MOE_CHUNK)
    out = _combine(h1, route, y, norm_final_g[None], tm=PREFIX)
    return out[None]
```

```python
import functools

import jax
import jax.numpy as jnp
from jax import lax
from jax.experimental import pallas as pl
from jax.experimental.pallas import tpu as pltpu

F32 = jnp.float32
BF16 = jnp.bfloat16

N_META = 16
CHUNK = 128
PREFIX = CHUNK
PAD = PREFIX - N_META
GRID_W = 64
ROPE_THETA = 10000.0
NORM_EPS = 1e-6
GROUPNORM_EPS = 1e-5

RET_HEADS = 8
RET_QK_DIM = 64
RET_V_DIM = 128
RET_UNROLL = 13
ATTN_HEADS = 8
ATTN_KV_HEADS = 2
ATTN_HEAD_DIM = 128
ATTN_GROUP = ATTN_HEADS // ATTN_KV_HEADS
ATTN_STRIP = 256
ATTN_ONES_ROWS = 16
LOG2_E = 1.4426950408889634

N_GROUPS = 8
EXPERTS_PER_GROUP = 8
N_EXPERTS = N_GROUPS * EXPERTS_PER_GROUP
TOP_K = 2
EXPERT_BLOCK = 128
MOE_ITEM_ROWS = 512
MOE_CHUNK = 512
MOE_ROW_STEP = 64
DMA_GROUP = 8

LANES = 128
PROJ_TILE = 512
PROJ_SUB_ROWS = (320, 192, 128)
MIX_SUB_BLOCKS = 5
VMEM_LIMIT = 56 * 1024 * 1024
NEG_BIG = -1e30


def _params(sem, vmem=VMEM_LIMIT):
    return pltpu.CompilerParams(dimension_semantics=sem, vmem_limit_bytes=vmem)


def _swap_halves(x, hw):
    w = x.shape[-1]
    lane = lax.broadcasted_iota(jnp.int32, x.shape, 1)
    first = (lane & hw) == 0
    return jnp.where(first, pltpu.roll(x, w - hw, 1), pltpu.roll(x, hw, 1))


def _rope_tables(n_tok, head_dim):
    half = head_dim // 2
    freqs = ROPE_THETA ** (-jnp.arange(0, half, 2, dtype=F32) / half)
    rows = n_tok // GRID_W
    ang_r = jnp.arange(rows, dtype=F32)[:, None] * freqs[None, :]
    ang_c = jnp.arange(GRID_W, dtype=F32)[:, None] * freqs[None, :]

    def per_token(tab_r, tab_c, prefix_value):
        pre = jnp.full((PREFIX, tab_r.shape[1]), prefix_value, F32)
        return (jnp.concatenate([pre, jnp.repeat(tab_r, GRID_W, axis=0)], axis=0),
                jnp.concatenate([pre, jnp.tile(tab_c, (rows, 1))], axis=0))

    cos_r, cos_c = per_token(jnp.cos(ang_r), jnp.cos(ang_c), 1.0)
    sin_r, sin_c = per_token(jnp.sin(ang_r), jnp.sin(ang_c), 0.0)
    cos = jnp.concatenate([cos_r, cos_r, cos_c, cos_c], axis=-1)
    sin = jnp.concatenate([-sin_r, sin_r, -sin_c, sin_c], axis=-1)
    reps = LANES // head_dim
    return jnp.tile(cos, (1, reps)), jnp.tile(sin, (1, reps))


def _stream_specs(n_blocks, d):
    return [pl.BlockSpec((PREFIX, d), lambda i, j=j: (jnp.maximum(n_blocks * i - 1 + j, 0), 0))
            for j in range(n_blocks)]


def _stream_rows(pre_ref, x_refs):
    first = jnp.where(pl.program_id(0) == 0, pre_ref[...], x_refs[0][...])
    return jnp.concatenate([first] + [r[...] for r in x_refs[1:]], axis=0)


def _stream_blocks(lp):
    n_blocks = lp // PREFIX
    return max(k for k in (5, 3, 1) if n_blocks % k == 0)


def _in_proj_kernel(*refs, tm, n_x):
    pre_ref, x_refs = refs[0], refs[1:1 + n_x]
    (g_ref, w_ref, cr_ref, sr_ref, ca_ref, sa_ref, qg_ref, kg_ref,
     oqk_ref, orest_ref, a_scr) = refs[1 + n_x:]
    i = pl.program_id(0)
    x = _stream_rows(pre_ref, x_refs)
    ms = jnp.mean(x * x, axis=-1, keepdims=True)
    a_scr[...] = ((x * lax.rsqrt(ms + NORM_EPS)) * g_ref[...]).astype(BF16)
    n_lane_groups = PROJ_TILE // LANES
    ret_tiles = 2
    plain_tiles = (2 * RET_HEADS * RET_V_DIM) // PROJ_TILE
    q_tiles = (ATTN_HEADS * ATTN_HEAD_DIM) // PROJ_TILE

    def tiled(ref):
        return jnp.concatenate([ref[...]] * n_lane_groups, axis=1)

    ts = max(t for t in PROJ_SUB_ROWS if tm % t == 0)
    lo = lax.broadcasted_iota(jnp.int32, (ts, LANES), 1) < RET_QK_DIM

    for r0 in range(0, tm, ts):
        rows = slice(r0, r0 + ts)
        row = i * tm + r0 + lax.broadcasted_iota(jnp.int32, (ts, 1), 0)
        k_scale = jnp.where(row >= PAD, RET_QK_DIM ** -0.5, 0.0).astype(F32)
        cos_r, sin_r = tiled(cr_ref.at[rows, :]), tiled(sr_ref.at[rows, :])
        cos_a, sin_a = ca_ref[rows, :], sa_ref[rows, :]

        def qk_norm_rope(xg, gain_ref):
            ms_h = jnp.mean(xg * xg, axis=-1, keepdims=True)
            xn = (xg * lax.rsqrt(ms_h + NORM_EPS)) * gain_ref[...]
            return xn * cos_a + _swap_halves(xn, ATTN_HEAD_DIM // 4) * sin_a

        for j in range(w_ref.shape[1] // PROJ_TILE):
            acc = jnp.dot(a_scr[rows, :], w_ref[:, j * PROJ_TILE:(j + 1) * PROJ_TILE],
                          preferred_element_type=F32)
            if j < ret_tiles:
                y = acc * cos_r + _swap_halves(acc, RET_QK_DIM // 4) * sin_r
                if j == 1:
                    y = y * k_scale
                pieces = []
                for c in range(n_lane_groups):
                    xg = y[:, c * LANES:(c + 1) * LANES]
                    xr = pltpu.roll(xg, RET_QK_DIM, 1)
                    pieces.append(jnp.where(lo, xg, xr))
                    pieces.append(jnp.where(lo, xr, xg))
                oqk_ref[rows, 2 * j * PROJ_TILE:2 * (j + 1) * PROJ_TILE] = (
                    jnp.concatenate(pieces, axis=1).astype(BF16))
                continue
            jr = j - ret_tiles
            if jr < plain_tiles:
                out = acc
            elif jr < plain_tiles + q_tiles:
                out = jnp.concatenate(
                    [qk_norm_rope(acc[:, c * LANES:(c + 1) * LANES], qg_ref)
                     * (LOG2_E * ATTN_HEAD_DIM ** -0.5) for c in range(n_lane_groups)], axis=1)
            else:
                pieces = [qk_norm_rope(acc[:, c * LANES:(c + 1) * LANES], kg_ref)
                          for c in range(ATTN_KV_HEADS)]
                pieces.append(acc[:, ATTN_KV_HEADS * LANES:])
                out = jnp.concatenate(pieces, axis=1)
            orest_ref[rows, jr * PROJ_TILE:(jr + 1) * PROJ_TILE] = out.astype(BF16)


def _in_proj(pre, x, g, w_bf, cr, sr, ca, sa, qg, kg):
    d = x.shape[1]
    lp = x.shape[0] + PREFIX
    n_x = _stream_blocks(lp)
    tm = n_x * PREFIX
    n_out = w_bf.shape[1]
    assert lp % tm == 0 and n_out == 9 * PROJ_TILE
    qk_w = 2 * 2 * RET_HEADS * RET_QK_DIM
    rest_w = n_out - 2 * RET_HEADS * RET_QK_DIM
    row_tab = pl.BlockSpec((tm, LANES), lambda i: (i, 0))
    vec = lambda n: pl.BlockSpec((1, n), lambda i: (0, 0))
    return pl.pallas_call(
        functools.partial(_in_proj_kernel, tm=tm, n_x=n_x),
        grid=(lp // tm,),
        in_specs=[
            pl.BlockSpec((PREFIX, d), lambda i: (0, 0)),
            *_stream_specs(n_x, d),
            vec(d),
            pl.BlockSpec((d, n_out), lambda i: (0, 0), pipeline_mode=pl.Buffered(1)),
            row_tab, row_tab, row_tab, row_tab,
            vec(LANES), vec(LANES),
        ],
        out_specs=[
            pl.BlockSpec((tm, qk_w), lambda i: (i, 0)),
            pl.BlockSpec((tm, rest_w), lambda i: (i, 0)),
        ],
        out_shape=[jax.ShapeDtypeStruct((lp, qk_w), BF16), jax.ShapeDtypeStruct((lp, rest_w), BF16)],
        scratch_shapes=[pltpu.VMEM((tm, d), BF16)],
        compiler_params=_params(("arbitrary",)),
        name="in_proj",
    )(pre, *([x] * n_x), g, w_bf, cr, sr, ca, sa, qg, kg)


def _retention_kernel(lg_ref, q_ref, k_ref, v_ref, g_ref, gn_ref, o_ref, st_ref, dk_ref, dq_ref, dm_ref,
                      *, n_chunks):
    h = pl.program_id(0)
    lgf = lg_ref[0, h]
    lgb = lg_ref[1, h]
    half = CHUNK // 2
    row = lax.broadcasted_iota(jnp.int32, (CHUNK, CHUNK), 0).astype(F32)
    lane = lax.broadcasted_iota(jnp.int32, (CHUNK, CHUNK), 1).astype(F32)
    lane_lo = lane < half
    dk_ref[...] = jnp.exp(jnp.where(lane_lo, lgf * (CHUNK - 1.0 - row), lgb * row))
    dq_ref[...] = jnp.exp(jnp.where(lane_lo, lgf * (row + 1.0), lgb * (CHUNK - row)))
    diff = row - lane
    dm_ref[...] = 0.5 * jnp.where(diff >= 0, jnp.exp(lgf * jnp.maximum(diff, 0.0)),
                                  jnp.exp(lgb * jnp.maximum(-diff, 0.0)))
    cf = jnp.exp(jnp.full((half, CHUNK), lgf * CHUNK, F32))
    cb = jnp.exp(jnp.full((half, CHUNK), lgb * CHUNK, F32))

    def rows(n):
        return pl.ds(pl.multiple_of(n * CHUNK, CHUNK), CHUNK)

    def updates(n, c):
        kd = k_ref[rows(n), :].astype(F32) * dk_ref[...]
        st_ref[n] = jnp.dot(kd.T.astype(BF16), v_ref[rows(n), :], preferred_element_type=F32)
        return c

    unroll = max(u for u in range(1, RET_UNROLL + 1) if n_chunks % u == 0)

    def unrolled(fn):
        def body(t, c):
            for u in range(unroll):
                fn(t * unroll + u, c)
            return c
        return body

    lax.fori_loop(0, n_chunks // unroll, unrolled(updates), 0)

    def scan_fwd(n, s):
        u = st_ref[n, 0:half, :]
        st_ref[n, 0:half, :] = s
        return cf * s + u

    lax.fori_loop(0, n_chunks, scan_fwd, jnp.zeros((half, CHUNK), F32))

    def scan_bwd(t, s):
        n = n_chunks - 1 - t
        u = st_ref[n, half:CHUNK, :]
        st_ref[n, half:CHUNK, :] = s
        return cb * s + u

    lax.fori_loop(0, n_chunks, scan_bwd, jnp.zeros((half, CHUNK), F32))

    def outputs(n, c):
        q = q_ref[rows(n), :]
        k = k_ref[rows(n), :]
        v = v_ref[rows(n), :]
        s2 = lax.dot_general(q, k, (((1,), (1,)), ((), ())), preferred_element_type=F32)
        intra = jnp.dot((s2 * dm_ref[...]).astype(BF16), v, preferred_element_type=F32)
        qd = (q.astype(F32) * dq_ref[...]).astype(BF16)
        cross = jnp.dot(qd, st_ref[n].astype(BF16), preferred_element_type=F32)
        o = intra + cross
        mu = jnp.mean(o, axis=-1, keepdims=True)
        dlt = o - mu
        var = jnp.mean(dlt * dlt, axis=-1, keepdims=True)
        on = dlt * lax.rsqrt(var + GROUPNORM_EPS)
        gate = g_ref[rows(n), :].astype(F32)
        o_ref[rows(n), :] = ((gate * jax.nn.sigmoid(gate)) * (on * gn_ref[...])).astype(BF16)
        return c

    lax.fori_loop(0, n_chunks // unroll, unrolled(outputs), 0)


def _retention(log_gamma, qk, rest, gn_g):
    lp = qk.shape[0]
    n_chunks = lp // CHUNK
    col = lambda off: pl.BlockSpec((lp, LANES), lambda h, lg: (0, h + off))
    return pl.pallas_call(
        functools.partial(_retention_kernel, n_chunks=n_chunks),
        grid_spec=pltpu.PrefetchScalarGridSpec(
            num_scalar_prefetch=1,
            grid=(RET_HEADS,),
            in_specs=[col(0), col(RET_HEADS), col(0), col(RET_HEADS),
                      pl.BlockSpec((1, LANES), lambda h, lg: (0, h))],
            out_specs=pl.BlockSpec((lp, LANES), lambda h, lg: (0, h)),
            scratch_shapes=[pltpu.VMEM((n_chunks, CHUNK, CHUNK), F32),
                            pltpu.VMEM((CHUNK, CHUNK), F32),
                            pltpu.VMEM((CHUNK, CHUNK), F32),
                            pltpu.VMEM((CHUNK, CHUNK), F32)],
        ),
        out_shape=jax.ShapeDtypeStruct((lp, RET_HEADS * RET_V_DIM), BF16),
        compiler_params=_params(("arbitrary",)),
        name="retention",
    )(log_gamma, qk, qk, rest, rest, gn_g)


def _attention_kernel(q_ref, k_ref, vt_ref, km_ref, vtm_ref, o_ref, s_scr, mx_scr, m_scr, acc_scr,
                      *, tq, tk, n_kc):
    q_all = jnp.concatenate([q_ref[:, g * LANES:(g + 1) * LANES] for g in range(ATTN_GROUP)], axis=0)
    n_strips = (ATTN_GROUP * tq) // ATTN_STRIP
    q_strips = [q_all[s * ATTN_STRIP:(s + 1) * ATTN_STRIP, :] for s in range(n_strips)]

    def scores(k, s):
        return lax.dot_general(k, q_strips[s], (((1,), (1,)), ((), ())), preferred_element_type=F32)

    def key_rows(c):
        return pl.ds(pl.multiple_of(PREFIX + c * tk, LANES), tk)

    def score_stage(c, slot):
        k = k_ref[key_rows(c), :]
        for s in range(n_strips):
            st = scores(k, s)
            s_scr[slot, s] = st
            mx_scr[slot, s] = jnp.max(st, axis=0, keepdims=True)

    def value_stage(c, slot):
        vt = vt_ref[0, :, key_rows(c)]
        for s in range(n_strips):
            m_old = m_scr[s]
            m_new = jnp.maximum(m_old, mx_scr[slot, s])
            p = jnp.exp2(s_scr[slot, s] - m_new).astype(BF16)
            acc_scr[s] = jnp.exp2(m_old - m_new) * acc_scr[s] + jnp.dot(vt, p, preferred_element_type=F32)
            m_scr[s] = m_new

    for s in range(n_strips):
        st = scores(km_ref[0], s)
        m0 = jnp.max(st, axis=0, keepdims=True)
        m_scr[s] = m0
        acc_scr[s] = jnp.dot(vtm_ref[0], jnp.exp2(st - m0).astype(BF16), preferred_element_type=F32)

    score_stage(0, 0)

    def body(i, carry):
        c = 2 * i
        score_stage(c + 1, 1)
        value_stage(c, 0)
        score_stage(c + 2, 0)
        value_stage(c + 1, 1)
        return carry

    lax.fori_loop(0, n_kc // 2 - 1, body, 0)
    score_stage(n_kc - 1, 1)
    value_stage(n_kc - 2, 0)
    value_stage(n_kc - 1, 1)
    outs = []
    for s in range(n_strips):
        acc = acc_scr[s]
        outs.append((acc[0:ATTN_HEAD_DIM, :] / acc[ATTN_HEAD_DIM:ATTN_HEAD_DIM + 1, :]).T)
    o = jnp.concatenate(outs, axis=0)
    for g in range(ATTN_GROUP):
        o_ref[:, g * LANES:(g + 1) * LANES] = o[g * tq:(g + 1) * tq, :]


def _attention(rest, *, tq, tk):
    lp = rest.shape[0]
    n_tok = lp - PREFIX
    assert lp % tq == 0 and n_tok % (2 * tk) == 0 and (ATTN_GROUP * tq) % ATTN_STRIP == 0
    gw = ATTN_GROUP * ATTN_HEAD_DIM
    q_off = (2 * RET_HEADS * RET_V_DIM) // gw
    k_col = 2 * RET_HEADS * RET_V_DIM + ATTN_HEADS * ATTN_HEAD_DIM
    k_off = k_col // LANES
    kv_w = ATTN_KV_HEADS * ATTN_HEAD_DIM
    vt = rest[:, k_col + kv_w:k_col + 2 * kv_w].T.reshape(ATTN_KV_HEADS, ATTN_HEAD_DIM, lp)
    vt = jnp.concatenate([vt, jnp.ones((ATTN_KV_HEADS, ATTN_ONES_ROWS, lp), BF16)], axis=1)
    vt_meta = vt[:, :, PAD:PREFIX]
    k_meta = rest[PAD:PREFIX, k_col:k_col + kv_w].reshape(N_META, ATTN_KV_HEADS, ATTN_HEAD_DIM)
    k_meta = jnp.swapaxes(k_meta, 0, 1)
    n_strips = (ATTN_GROUP * tq) // ATTN_STRIP
    vt_rows = ATTN_HEAD_DIM + ATTN_ONES_ROWS
    return pl.pallas_call(
        functools.partial(_attention_kernel, tq=tq, tk=tk, n_kc=n_tok // tk),
        grid=(ATTN_KV_HEADS, lp // tq),
        in_specs=[pl.BlockSpec((tq, gw), lambda kv, i: (i, q_off + kv)),
                  pl.BlockSpec((lp, LANES), lambda kv, i: (0, k_off + kv)),
                  pl.BlockSpec((1, vt_rows, lp), lambda kv, i: (kv, 0, 0)),
                  pl.BlockSpec((1, N_META, ATTN_HEAD_DIM), lambda kv, i: (kv, 0, 0)),
                  pl.BlockSpec((1, vt_rows, N_META), lambda kv, i: (kv, 0, 0))],
        out_specs=pl.BlockSpec((tq, gw), lambda kv, i: (i, kv)),
        out_shape=jax.ShapeDtypeStruct((lp, ATTN_HEADS * ATTN_HEAD_DIM), F32),
        scratch_shapes=[pltpu.VMEM((2, n_strips, tk, ATTN_STRIP), F32),
                        pltpu.VMEM((2, n_strips, 1, ATTN_STRIP), F32),
                        pltpu.VMEM((n_strips, 1, ATTN_STRIP), F32),
                        pltpu.VMEM((n_strips, vt_rows, ATTN_STRIP), F32)],
        compiler_params=_params(("arbitrary", "arbitrary")),
        name="attention",
    )(rest, rest, vt, k_meta, vt_meta)


def _mix_router_kernel(*refs, tm, n_x):
    pre_ref, x_refs = refs[0], refs[1:1 + n_x]
    (ret_ref, att_ref, ang_ref, wo_ref, fg_ref, wr_ref, br_ref,
     h1_ref, f_ref, route_ref, cnt_ref, carry_scr) = refs[1 + n_x:]
    i = pl.program_id(0)

    @pl.when(i == 0)
    def _():
        carry_scr[...] = jnp.zeros(carry_scr.shape, F32)

    blocks = [jnp.where(i == 0, pre_ref[...], x_refs[0][...])] + [r[...] for r in x_refs[1:]]
    for b0 in range(0, n_x, MIX_SUB_BLOCKS):
        b1 = min(b0 + MIX_SUB_BLOCKS, n_x)
        h_rows = blocks[b0] if b1 - b0 == 1 else jnp.concatenate(blocks[b0:b1], axis=0)
        _mix_router_rows(slice(b0 * PREFIX, b1 * PREFIX), (b1 - b0) * PREFIX, h_rows, ret_ref, att_ref,
                         ang_ref, wo_ref, fg_ref, wr_ref, br_ref, h1_ref, f_ref, route_ref, carry_scr)
    cnt_ref[...] = carry_scr[...]


def _mix_router_rows(rows, tm, h_rows, ret_ref, att_ref, ang_ref, wo_ref, fg_ref, wr_ref, br_ref,
                     h1_ref, f_ref, route_ref, carry_scr):
    att = att_ref[rows, :]
    ms = jnp.mean(att * att, axis=-1, keepdims=True)
    att_n = ((att * lax.rsqrt(ms + NORM_EPS)) * ang_ref[...]).astype(BF16)
    mix_in = jnp.concatenate([ret_ref[rows, :], att_n], axis=1)
    h1 = h_rows + jnp.dot(mix_in, wo_ref[...], preferred_element_type=F32)
    h1_ref[rows, :] = h1
    ms1 = jnp.mean(h1 * h1, axis=-1, keepdims=True)
    f = (h1 * lax.rsqrt(ms1 + NORM_EPS)) * fg_ref[...]
    f_ref[rows, :] = f

    f_hi = f.astype(BF16)
    f_lo = (f - f_hi.astype(F32)).astype(BF16)
    d = f.shape[1]
    k_half = 3 * d // 2
    logits = (jnp.dot(jnp.concatenate([f_hi, f_lo[:, :d // 2]], axis=1), wr_ref[0:k_half, :],
                      preferred_element_type=F32)
              + jnp.dot(jnp.concatenate([f_lo[:, d // 2:], f_hi], axis=1), wr_ref[k_half:, :],
                        preferred_element_type=F32)
              + br_ref[...])
    lane = lax.broadcasted_iota(jnp.int32, (tm, LANES), 1).astype(F32)
    far = float(LANES)
    g_logit = jnp.where(lane < N_GROUPS, logits, NEG_BIG)
    g_max = jnp.max(g_logit, axis=-1, keepdims=True)
    g_w = 1.0 / jnp.sum(jnp.exp(g_logit - g_max), axis=-1, keepdims=True)
    g_idx = jnp.min(jnp.where(g_logit == g_max, lane, far), axis=-1, keepdims=True)
    first = N_GROUPS + g_idx * EXPERTS_PER_GROUP
    e_logit = jnp.where((lane >= first) & (lane < first + EXPERTS_PER_GROUP), logits, NEG_BIG)
    v1 = jnp.max(e_logit, axis=-1, keepdims=True)
    i1 = jnp.min(jnp.where(e_logit == v1, lane, far), axis=-1, keepdims=True)
    e_rest = jnp.where(lane == i1, NEG_BIG, e_logit)
    v2 = jnp.max(e_rest, axis=-1, keepdims=True)
    i2 = jnp.min(jnp.where(e_rest == v2, lane, far), axis=-1, keepdims=True)
    e2 = jnp.exp(v2 - v1)
    w1 = g_w / (1.0 + e2)
    w2 = g_w * e2 / (1.0 + e2)

    oh1 = (lane == i1).astype(F32)
    oh2 = (lane == i2).astype(F32)
    oh = oh1 + oh2
    r_i = lax.broadcasted_iota(jnp.int32, (tm, tm), 0)
    c_i = lax.broadcasted_iota(jnp.int32, (tm, tm), 1)
    lower = (c_i < r_i).astype(BF16)
    before = jnp.dot(lower, oh.astype(BF16), preferred_element_type=F32) + carry_scr[...]
    rank1 = jnp.sum(before * oh1, axis=-1, keepdims=True)
    rank2 = jnp.sum(before * oh2, axis=-1, keepdims=True)
    carry_scr[...] = carry_scr[...] + jnp.sum(oh, axis=0, keepdims=True)

    route = jnp.where(lane == 0, i1 - N_GROUPS, 0.0)
    route = jnp.where(lane == 1, i2 - N_GROUPS, route)
    route = jnp.where(lane == 2, w1, route)
    route = jnp.where(lane == 3, w2, route)
    route = jnp.where(lane == 4, rank1, route)
    route = jnp.where(lane == 5, rank2, route)
    route_ref[rows, :] = route


def _mix_router(pre, x, ret, att, ang, wo_bf, fg, wr, br):
    d = x.shape[1]
    lp = x.shape[0] + PREFIX
    n_x = _stream_blocks(lp)
    tm = n_x * PREFIX
    assert lp % tm == 0
    rowblk = lambda n: pl.BlockSpec((tm, n), lambda i: (i, 0))
    vec = lambda n: pl.BlockSpec((1, n), lambda i: (0, 0))
    full = lambda a: pl.BlockSpec(a.shape, lambda i: (0, 0))
    return pl.pallas_call(
        functools.partial(_mix_router_kernel, tm=tm, n_x=n_x),
        grid=(lp // tm,),
        in_specs=[pl.BlockSpec((PREFIX, d), lambda i: (0, 0)), *_stream_specs(n_x, d),
                  rowblk(ret.shape[1]), rowblk(att.shape[1]), vec(att.shape[1]),
                  pl.BlockSpec(wo_bf.shape, lambda i: (0, 0), pipeline_mode=pl.Buffered(1)),
                  vec(d), full(wr), vec(LANES)],
        out_specs=[rowblk(d), rowblk(d), rowblk(LANES), vec(LANES)],
        out_shape=[jax.ShapeDtypeStruct((lp, d), F32), jax.ShapeDtypeStruct((lp, d), F32),
                   jax.ShapeDtypeStruct((lp, LANES), F32), jax.ShapeDtypeStruct((1, LANES), F32)],
        scratch_shapes=[pltpu.VMEM((1, LANES), F32)],
        compiler_params=_params(("arbitrary",)),
        name="mix_router",
    )(pre, *([x] * n_x), ret, att, ang, wo_bf, fg, wr, br)


def _moe_kernel(ie_ref, irow_ref, irows_ref, sidx_ref, tok_ref, f_ref, wg_hbm, wu_hbm, wd_hbm, o_ref,
                x_scr, y_scr, wg_buf, wu_buf, wd_buf, gsem, ssem, wsem, *, n_items, n_chunks, chunk):
    i = pl.program_id(0)
    c = pl.program_id(1)
    n_rows = irows_ref[i]
    slot = i % 2
    w_slot = (i * n_chunks + c) % 2

    def weights(item, cc, ws, start):
        e = ie_ref[item]
        half = chunk // 2
        col0 = pl.multiple_of(cc * chunk, chunk)
        copies = (
            (wg_hbm.at[e, :, pl.ds(col0, chunk)], wg_buf.at[ws]),
            (wu_hbm.at[e, :, pl.ds(col0, chunk)], wu_buf.at[ws]),
            (wd_hbm.at[e, pl.ds(col0, half), :], wd_buf.at[ws, 0:half, :]),
            (wd_hbm.at[e, pl.ds(col0 + half, half), :], wd_buf.at[ws, half:chunk, :]),
        )

        @pl.when(irows_ref[item] > 0)
        def _():
            for n, (src, dst) in enumerate(copies):
                cp = pltpu.make_async_copy(src, dst, wsem.at[ws, n])
                cp.start(priority=n % 2) if start else cp.wait()

    @pl.when((i == 0) & (c == 0))
    def _():
        weights(0, 0, 0, True)

    @pl.when(c + 1 < n_chunks)
    def _():
        weights(i, c + 1, 1 - w_slot, True)

    @pl.when((c + 1 == n_chunks) & (i + 1 < n_items))
    def _():
        weights(i + 1, 0, 1 - w_slot, True)

    weights(i, c, w_slot, False)

    def tile_row(ref, g, u):
        return ref.at[g, pl.ds(u, 1), :]

    def group_wait(src_rows, dst_rows, sem):
        pltpu.make_async_copy(src_rows, dst_rows, sem).wait()

    def gather(item, dst_slot, start):
        row0 = irow_ref[item]
        dst = x_scr.at[dst_slot]
        sem = gsem.at[dst_slot]

        def group(g, carry):
            if start:
                for u in range(DMA_GROUP):
                    tok = tok_ref[row0 + g * DMA_GROUP + u]
                    pltpu.make_async_copy(f_ref.at[pl.ds(tok, 1), :], tile_row(dst, g, u),
                                          sem).start(priority=u % 2)
            else:
                group_wait(f_ref.at[pl.ds(0, DMA_GROUP), :], dst.at[0], sem)
            return carry

        lax.fori_loop(0, (irows_ref[item] + DMA_GROUP - 1) // DMA_GROUP, group, 0)

    def scatter(item, start):
        row0 = irow_ref[item]
        n = irows_ref[item]
        n_groups = n // DMA_GROUP

        def group(g, carry):
            if start:
                for u in range(DMA_GROUP):
                    a = sidx_ref[row0 + g * DMA_GROUP + u]
                    pltpu.make_async_copy(tile_row(y_scr, g, u), o_ref.at[pl.ds(a, 1), :],
                                          ssem).start(priority=u % 2)
            else:
                group_wait(y_scr.at[0], o_ref.at[pl.ds(0, DMA_GROUP), :], ssem)
            return carry

        lax.fori_loop(0, n_groups, group, 0)

        def single(r, carry):
            cp = pltpu.make_async_copy(y_scr.at[r // DMA_GROUP, pl.ds(r % DMA_GROUP, 1), :],
                                       o_ref.at[pl.ds(sidx_ref[row0 + r], 1), :], ssem)
            cp.start() if start else cp.wait()
            return carry

        lax.fori_loop(n_groups * DMA_GROUP, n, single, 0)

    @pl.when(c == 0)
    def _():
        @pl.when(i == 0)
        def _():
            x_scr[...] = jnp.zeros(x_scr.shape, F32)
            y_scr[...] = jnp.zeros(y_scr.shape, F32)
            gather(0, 0, True)

        @pl.when(i + 1 < n_items)
        def _():
            gather(i + 1, 1 - slot, True)

        gather(i, slot, False)

        @pl.when(i > 0)
        def _():
            scatter(i - 1, False)

    n_blocks = (n_rows + MOE_ROW_STEP - 1) // MOE_ROW_STEP
    for v in range(1, MOE_ITEM_ROWS // MOE_ROW_STEP + 1):
        @pl.when(n_blocks == v)
        def _(m=v * MOE_ROW_STEP):
            mg = m // DMA_GROUP
            d = x_scr.shape[-1]
            x = x_scr[slot, 0:mg].reshape(m, d).astype(BF16)
            gate = jnp.dot(x, wg_buf[w_slot].astype(BF16), preferred_element_type=F32)
            up = jnp.dot(x, wu_buf[w_slot].astype(BF16), preferred_element_type=F32)
            hid = ((gate * jax.nn.sigmoid(gate)) * up).astype(BF16)
            y = jnp.dot(hid, wd_buf[w_slot].astype(BF16), preferred_element_type=F32)
            y_scr[0:mg] = jnp.where(c > 0, y_scr[0:mg], 0.0) + y.reshape(mg, DMA_GROUP, d)

    @pl.when(c == n_chunks - 1)
    def _():
        scatter(i, True)

        @pl.when(i == n_items - 1)
        def _():
            scatter(i, False)


def _moe_experts(item_e, item_row, item_rows, sidx, tok_row, f, w_gate, w_up, w_down, *, chunk):
    lp, d = f.shape
    d_e = w_gate.shape[2]
    n_items = item_e.shape[0]
    n_chunks = d_e // chunk
    return pl.pallas_call(
        functools.partial(_moe_kernel, n_items=n_items, n_chunks=n_chunks, chunk=chunk),
        grid_spec=pltpu.PrefetchScalarGridSpec(
            num_scalar_prefetch=5,
            grid=(n_items, n_chunks),
            in_specs=[pl.BlockSpec(memory_space=pl.ANY)] * 4,
            out_specs=pl.BlockSpec(memory_space=pl.ANY),
            scratch_shapes=[pltpu.VMEM((2, MOE_ITEM_ROWS // DMA_GROUP, DMA_GROUP, d), F32),
                            pltpu.VMEM((MOE_ITEM_ROWS // DMA_GROUP, DMA_GROUP, d), F32),
                            pltpu.VMEM((2, d, chunk), F32),
                            pltpu.VMEM((2, d, chunk), F32),
                            pltpu.VMEM((2, chunk, d), F32),
                            pltpu.SemaphoreType.DMA((2,)),
                            pltpu.SemaphoreType.DMA(()),
                            pltpu.SemaphoreType.DMA((2, 4))],
        ),
        out_shape=jax.ShapeDtypeStruct((TOP_K * lp, d), F32),
        compiler_params=_params(("arbitrary", "arbitrary")),
        name="moe_experts",
    )(item_e, item_row, item_rows, sidx, tok_row, f, w_gate, w_up, w_down)


def _combine_kernel(*refs, n_b):
    h1_refs, route_refs = refs[0:n_b], refs[n_b:2 * n_b]
    y0_refs, y1_refs = refs[2 * n_b:3 * n_b], refs[3 * n_b:4 * n_b]
    g_ref, o_ref = refs[4 * n_b:]
    for b in range(n_b):
        route = route_refs[b][...]
        h2 = h1_refs[b][...] + (route[:, 2:3] * y0_refs[b][0] + route[:, 3:4] * y1_refs[b][0])
        ms = jnp.mean(h2 * h2, axis=-1, keepdims=True)
        o_ref[b * PREFIX:(b + 1) * PREFIX, :] = (h2 * lax.rsqrt(ms + NORM_EPS)) * g_ref[...]


def _combine(h1, route, y, g):
    lp, d = h1.shape
    n_tok = lp - PREFIX
    n_b = max(k for k in (4, 2, 1) if (n_tok // PREFIX) % k == 0)
    y3 = y.reshape(TOP_K, lp, d)
    blk = lambda width: [pl.BlockSpec((PREFIX, width), lambda i, b=b: (n_b * i + 1 + b, 0))
                         for b in range(n_b)]
    y_blk = lambda k: [pl.BlockSpec((1, PREFIX, d), lambda i, b=b: (k, n_b * i + 1 + b, 0))
                       for b in range(n_b)]
    return pl.pallas_call(
        functools.partial(_combine_kernel, n_b=n_b),
        grid=(n_tok // (n_b * PREFIX),),
        in_specs=[*blk(d), *blk(LANES), *y_blk(0), *y_blk(1), pl.BlockSpec((1, d), lambda i: (0, 0))],
        out_specs=pl.BlockSpec((n_b * PREFIX, d), lambda i: (i, 0)),
        out_shape=jax.ShapeDtypeStruct((n_tok, d), F32),
        compiler_params=_params(("arbitrary",)),
        name="combine",
    )(*([h1] * n_b), *([route] * n_b), *([y3] * (2 * n_b)), g)


def _row_tile(lp, candidates):
    for t in candidates:
        if lp % t == 0:
            return t
    raise ValueError(f"no row tile for {lp}")


def kernel(x, meta_tokens, norm_mix_g, w_in, ret_decay_logit_fwd, ret_decay_logit_bwd, ret_norm_g,
           attn_q_norm_g, attn_k_norm_g, attn_out_norm_g, w_out, norm_ffn_g, router_group_w,
           router_group_b, router_expert_w, router_expert_b, expert_w_gate, expert_w_up, expert_w_down,
           norm_final_g):
    batch, n_tok, d = x.shape
    assert batch == 1 and norm_mix_g.shape[0] == 1
    lp = n_tok + PREFIX
    pre = jnp.concatenate([jnp.zeros((PAD, d), x.dtype), meta_tokens.astype(x.dtype)], axis=0)

    cr, sr = _rope_tables(n_tok, RET_QK_DIM)
    ca, sa = _rope_tables(n_tok, ATTN_HEAD_DIM)
    qk, rest = _in_proj(pre, x[0], norm_mix_g, w_in[0].astype(BF16), cr, sr, ca, sa,
                        attn_q_norm_g, attn_k_norm_g)

    log_gamma = jnp.stack([jax.nn.log_sigmoid(ret_decay_logit_fwd[0].astype(F32)),
                           jax.nn.log_sigmoid(ret_decay_logit_bwd[0].astype(F32))])
    ret = _retention(log_gamma, qk, rest, ret_norm_g)
    att = _attention(rest, tq=_row_tile(lp, (320, 128, 64)), tk=_row_tile(n_tok // 2, (1024, 512, 256, 128)))

    n_route = N_GROUPS + N_EXPERTS
    wr = jnp.pad(jnp.concatenate([router_group_w[0], router_expert_w[0]], axis=1).astype(F32),
                 ((0, 0), (0, LANES - n_route)))
    br = jnp.pad(jnp.concatenate([router_group_b[0], router_expert_b[0]]).astype(F32),
                 (0, LANES - n_route))[None]
    wr_hi = wr.astype(BF16)
    wr_lo = (wr - wr_hi.astype(F32)).astype(BF16)
    wr = jnp.concatenate([wr_hi, wr_hi, wr_lo], axis=0)
    h1, f, route, cnt = _mix_router(pre, x[0], ret, att, attn_out_norm_g, w_out[0].astype(BF16),
                                    norm_ffn_g, wr, br)

    expert = route[:, 0:TOP_K].astype(jnp.int32)
    rank = route[:, 4:4 + TOP_K].astype(jnp.int32)
    counts = cnt[0, N_GROUPS:n_route].astype(jnp.int32)
    padded = (counts + EXPERT_BLOCK - 1) // EXPERT_BLOCK * EXPERT_BLOCK
    pad_end = jnp.cumsum(padded)
    pad_start = pad_end - padded
    is_expert = expert.T[:, :, None] == jnp.arange(N_EXPERTS, dtype=jnp.int32)
    seg_start = jnp.sum(jnp.where(is_expert, pad_start, 0), axis=-1)
    dest = (seg_start + rank.T).reshape(-1)
    n_rows = (-(-(lp * TOP_K) // EXPERT_BLOCK) + N_EXPERTS) * EXPERT_BLOCK
    sidx = jnp.full((n_rows,), -1, jnp.int32).at[dest].set(jnp.arange(TOP_K * lp, dtype=jnp.int32))
    groups = (counts + MOE_ITEM_ROWS - 1) // MOE_ITEM_ROWS
    g_end = jnp.cumsum(groups)
    g_start = g_end - groups
    n_items = N_EXPERTS + -(-(lp * TOP_K) // MOE_ITEM_ROWS)
    ii = jnp.arange(n_items, dtype=jnp.int32)
    valid = ii < g_end[-1]
    item_e = jnp.sum((g_end[None, :] <= jnp.minimum(ii, g_end[-1] - 1)[:, None]).astype(jnp.int32), axis=1)
    g_in = ii - g_start[item_e]
    item_row = jnp.where(valid, pad_start[item_e] + g_in * MOE_ITEM_ROWS, 0).astype(jnp.int32)
    item_rows = jnp.where(valid, jnp.minimum(MOE_ITEM_ROWS, counts[item_e] - g_in * MOE_ITEM_ROWS),
                          0).astype(jnp.int32)

    tok_row = jnp.where(sidx >= lp, sidx - lp, jnp.maximum(sidx, 0))
    y = _moe_experts(item_e, item_row, item_rows, sidx, tok_row, f, expert_w_gate[0], expert_w_up[0],
                     expert_w_down[0], chunk=MOE_CHUNK)
    out = _combine(h1, route, y, norm_final_g[None])
    return out[None]
```

```python
import functools

import jax
import jax.numpy as jnp
from jax import lax
from jax.experimental import pallas as pl
from jax.experimental.pallas import tpu as pltpu

F32 = jnp.float32
BF16 = jnp.bfloat16

N_META = 16
CHUNK = 128
PREFIX = CHUNK
PAD = PREFIX - N_META
GRID_W = 64
ROPE_THETA = 10000.0
NORM_EPS = 1e-6
GROUPNORM_EPS = 1e-5

RET_HEADS = 8
RET_QK_DIM = 64
RET_V_DIM = 128
RET_UNROLL = 13
ATTN_HEADS = 8
ATTN_KV_HEADS = 2
ATTN_HEAD_DIM = 128
ATTN_GROUP = ATTN_HEADS // ATTN_KV_HEADS
ATTN_STRIP = 256
ATTN_ONES_ROWS = 16
LOG2_E = 1.4426950408889634

N_GROUPS = 8
EXPERTS_PER_GROUP = 8
N_EXPERTS = N_GROUPS * EXPERTS_PER_GROUP
TOP_K = 2
MOE_ITEM_ROWS = 512
MOE_CHUNK = 512
MOE_ROW_STEP = 64
DMA_GROUP = 8

LANES = 128
PROJ_TILE = 512
PROJ_SUB_ROWS = (320, 192, 128)
MIX_SUB_BLOCKS = 5
VMEM_LIMIT = 56 * 1024 * 1024
NEG_BIG = -1e30


def _params(sem, vmem=VMEM_LIMIT):
    return pltpu.CompilerParams(dimension_semantics=sem, vmem_limit_bytes=vmem)


def _swap_halves(x, hw):
    w = x.shape[-1]
    lane = lax.broadcasted_iota(jnp.int32, x.shape, 1)
    first = (lane & hw) == 0
    return jnp.where(first, pltpu.roll(x, w - hw, 1), pltpu.roll(x, hw, 1))


def _rope_tables(n_tok, head_dim):
    half = head_dim // 2
    freqs = ROPE_THETA ** (-jnp.arange(0, half, 2, dtype=F32) / half)
    rows = n_tok // GRID_W
    ang_r = jnp.arange(rows, dtype=F32)[:, None] * freqs[None, :]
    ang_c = jnp.arange(GRID_W, dtype=F32)[:, None] * freqs[None, :]

    def per_token(tab_r, tab_c, prefix_value):
        pre = jnp.full((PREFIX, tab_r.shape[1]), prefix_value, F32)
        return (jnp.concatenate([pre, jnp.repeat(tab_r, GRID_W, axis=0)], axis=0),
                jnp.concatenate([pre, jnp.tile(tab_c, (rows, 1))], axis=0))

    cos_r, cos_c = per_token(jnp.cos(ang_r), jnp.cos(ang_c), 1.0)
    sin_r, sin_c = per_token(jnp.sin(ang_r), jnp.sin(ang_c), 0.0)
    cos = jnp.concatenate([cos_r, cos_r, cos_c, cos_c], axis=-1)
    sin = jnp.concatenate([-sin_r, sin_r, -sin_c, sin_c], axis=-1)
    reps = LANES // head_dim
    return jnp.tile(cos, (1, reps)), jnp.tile(sin, (1, reps))


def _stream_specs(n_blocks, d):
    return [pl.BlockSpec((PREFIX, d), lambda i, j=j: (jnp.maximum(n_blocks * i - 1 + j, 0), 0))
            for j in range(n_blocks)]


def _stream_rows(pre_ref, x_refs):
    first = jnp.where(pl.program_id(0) == 0, pre_ref[...], x_refs[0][...])
    return jnp.concatenate([first] + [r[...] for r in x_refs[1:]], axis=0)


def _stream_blocks(lp):
    n_blocks = lp // PREFIX
    return max(k for k in (5, 3, 1) if n_blocks % k == 0)


def _in_proj_kernel(*refs, tm, n_x):
    pre_ref, x_refs = refs[0], refs[1:1 + n_x]
    (g_ref, w_ref, cr_ref, sr_ref, ca_ref, sa_ref, qg_ref, kg_ref,
     oqk_ref, orest_ref, a_scr) = refs[1 + n_x:]
    i = pl.program_id(0)
    x = _stream_rows(pre_ref, x_refs)
    ms = jnp.mean(x * x, axis=-1, keepdims=True)
    a_scr[...] = ((x * lax.rsqrt(ms + NORM_EPS)) * g_ref[...]).astype(BF16)
    n_lane_groups = PROJ_TILE // LANES
    ret_tiles = 2
    plain_tiles = (2 * RET_HEADS * RET_V_DIM) // PROJ_TILE
    q_tiles = (ATTN_HEADS * ATTN_HEAD_DIM) // PROJ_TILE

    def tiled(ref):
        return jnp.concatenate([ref[...]] * n_lane_groups, axis=1)

    ts = max(t for t in PROJ_SUB_ROWS if tm % t == 0)
    lo = lax.broadcasted_iota(jnp.int32, (ts, LANES), 1) < RET_QK_DIM

    for r0 in range(0, tm, ts):
        rows = slice(r0, r0 + ts)
        row = i * tm + r0 + lax.broadcasted_iota(jnp.int32, (ts, 1), 0)
        k_scale = jnp.where(row >= PAD, RET_QK_DIM ** -0.5, 0.0).astype(F32)
        cos_r, sin_r = tiled(cr_ref.at[rows, :]), tiled(sr_ref.at[rows, :])
        cos_a, sin_a = ca_ref[rows, :], sa_ref[rows, :]

        def qk_norm_rope(xg, gain_ref):
            ms_h = jnp.mean(xg * xg, axis=-1, keepdims=True)
            xn = (xg * lax.rsqrt(ms_h + NORM_EPS)) * gain_ref[...]
            return xn * cos_a + _swap_halves(xn, ATTN_HEAD_DIM // 4) * sin_a

        for j in range(w_ref.shape[1] // PROJ_TILE):
            acc = jnp.dot(a_scr[rows, :], w_ref[:, j * PROJ_TILE:(j + 1) * PROJ_TILE],
                          preferred_element_type=F32)
            if j < ret_tiles:
                y = acc * cos_r + _swap_halves(acc, RET_QK_DIM // 4) * sin_r
                if j == 1:
                    y = y * k_scale
                pieces = []
                for c in range(n_lane_groups):
                    xg = y[:, c * LANES:(c + 1) * LANES]
                    xr = pltpu.roll(xg, RET_QK_DIM, 1)
                    pieces.append(jnp.where(lo, xg, xr))
                    pieces.append(jnp.where(lo, xr, xg))
                oqk_ref[rows, 2 * j * PROJ_TILE:2 * (j + 1) * PROJ_TILE] = (
                    jnp.concatenate(pieces, axis=1).astype(BF16))
                continue
            jr = j - ret_tiles
            if jr < plain_tiles:
                out = acc
            elif jr < plain_tiles + q_tiles:
                out = jnp.concatenate(
                    [qk_norm_rope(acc[:, c * LANES:(c + 1) * LANES], qg_ref)
                     * (LOG2_E * ATTN_HEAD_DIM ** -0.5) for c in range(n_lane_groups)], axis=1)
            else:
                pieces = [qk_norm_rope(acc[:, c * LANES:(c + 1) * LANES], kg_ref)
                          for c in range(ATTN_KV_HEADS)]
                pieces.append(acc[:, ATTN_KV_HEADS * LANES:])
                out = jnp.concatenate(pieces, axis=1)
            orest_ref[rows, jr * PROJ_TILE:(jr + 1) * PROJ_TILE] = out.astype(BF16)


def _in_proj(pre, x, g, w_bf, cr, sr, ca, sa, qg, kg):
    d = x.shape[1]
    lp = x.shape[0] + PREFIX
    n_x = _stream_blocks(lp)
    tm = n_x * PREFIX
    n_out = w_bf.shape[1]
    assert lp % tm == 0 and n_out == 9 * PROJ_TILE
    qk_w = 2 * 2 * RET_HEADS * RET_QK_DIM
    rest_w = n_out - 2 * RET_HEADS * RET_QK_DIM
    row_tab = pl.BlockSpec((tm, LANES), lambda i: (i, 0))
    vec = lambda n: pl.BlockSpec((1, n), lambda i: (0, 0))
    return pl.pallas_call(
        functools.partial(_in_proj_kernel, tm=tm, n_x=n_x),
        grid=(lp // tm,),
        in_specs=[
            pl.BlockSpec((PREFIX, d), lambda i: (0, 0)),
            *_stream_specs(n_x, d),
            vec(d),
            pl.BlockSpec((d, n_out), lambda i: (0, 0), pipeline_mode=pl.Buffered(1)),
            row_tab, row_tab, row_tab, row_tab,
            vec(LANES), vec(LANES),
        ],
        out_specs=[
            pl.BlockSpec((tm, qk_w), lambda i: (i, 0)),
            pl.BlockSpec((tm, rest_w), lambda i: (i, 0)),
        ],
        out_shape=[jax.ShapeDtypeStruct((lp, qk_w), BF16), jax.ShapeDtypeStruct((lp, rest_w), BF16)],
        scratch_shapes=[pltpu.VMEM((tm, d), BF16)],
        compiler_params=_params(("arbitrary",)),
        name="in_proj",
    )(pre, *([x] * n_x), g, w_bf, cr, sr, ca, sa, qg, kg)


def _retention_kernel(lg_ref, q_ref, k_ref, v_ref, g_ref, gn_ref, o_ref, st_ref, dk_ref, dq_ref, dm_ref,
                      *, n_chunks):
    h = pl.program_id(0)
    lgf = lg_ref[0, h]
    lgb = lg_ref[1, h]
    half = CHUNK // 2
    row = lax.broadcasted_iota(jnp.int32, (CHUNK, CHUNK), 0).astype(F32)
    lane = lax.broadcasted_iota(jnp.int32, (CHUNK, CHUNK), 1).astype(F32)
    lane_lo = lane < half
    dk_ref[...] = jnp.exp(jnp.where(lane_lo, lgf * (CHUNK - 1.0 - row), lgb * row))
    dq_ref[...] = jnp.exp(jnp.where(lane_lo, lgf * (row + 1.0), lgb * (CHUNK - row)))
    diff = row - lane
    dm_ref[...] = 0.5 * jnp.where(diff >= 0, jnp.exp(lgf * jnp.maximum(diff, 0.0)),
                                  jnp.exp(lgb * jnp.maximum(-diff, 0.0)))
    cf = jnp.exp(jnp.full((half, CHUNK), lgf * CHUNK, F32))
    cb = jnp.exp(jnp.full((half, CHUNK), lgb * CHUNK, F32))

    def rows(n):
        return pl.ds(pl.multiple_of(n * CHUNK, CHUNK), CHUNK)

    def updates(n, c):
        kd = k_ref[rows(n), :].astype(F32) * dk_ref[...]
        st_ref[n] = jnp.dot(kd.T.astype(BF16), v_ref[rows(n), :], preferred_element_type=F32)
        return c

    unroll = max(u for u in range(1, RET_UNROLL + 1) if n_chunks % u == 0)

    def unrolled(fn):
        def body(t, c):
            for u in range(unroll):
                fn(t * unroll + u, c)
            return c
        return body

    lax.fori_loop(0, n_chunks // unroll, unrolled(updates), 0)

    def scan_fwd(n, s):
        u = st_ref[n, 0:half, :]
        st_ref[n, 0:half, :] = s
        return cf * s + u

    lax.fori_loop(0, n_chunks, scan_fwd, jnp.zeros((half, CHUNK), F32))

    def scan_bwd(t, s):
        n = n_chunks - 1 - t
        u = st_ref[n, half:CHUNK, :]
        st_ref[n, half:CHUNK, :] = s
        return cb * s + u

    lax.fori_loop(0, n_chunks, scan_bwd, jnp.zeros((half, CHUNK), F32))

    def outputs(n, c):
        q = q_ref[rows(n), :]
        k = k_ref[rows(n), :]
        v = v_ref[rows(n), :]
        s2 = lax.dot_general(q, k, (((1,), (1,)), ((), ())), preferred_element_type=F32)
        intra = jnp.dot((s2 * dm_ref[...]).astype(BF16), v, preferred_element_type=F32)
        qd = (q.astype(F32) * dq_ref[...]).astype(BF16)
        cross = jnp.dot(qd, st_ref[n].astype(BF16), preferred_element_type=F32)
        o = intra + cross
        mu = jnp.mean(o, axis=-1, keepdims=True)
        dlt = o - mu
        var = jnp.mean(dlt * dlt, axis=-1, keepdims=True)
        on = dlt * lax.rsqrt(var + GROUPNORM_EPS)
        gate = g_ref[rows(n), :].astype(F32)
        o_ref[rows(n), :] = ((gate * jax.nn.sigmoid(gate)) * (on * gn_ref[...])).astype(BF16)
        return c

    lax.fori_loop(0, n_chunks // unroll, unrolled(outputs), 0)


def _retention(log_gamma, qk, rest, gn_g):
    lp = qk.shape[0]
    n_chunks = lp // CHUNK
    col = lambda off: pl.BlockSpec((lp, LANES), lambda h, lg: (0, h + off))
    return pl.pallas_call(
        functools.partial(_retention_kernel, n_chunks=n_chunks),
        grid_spec=pltpu.PrefetchScalarGridSpec(
            num_scalar_prefetch=1,
            grid=(RET_HEADS,),
            in_specs=[col(0), col(RET_HEADS), col(0), col(RET_HEADS),
                      pl.BlockSpec((1, LANES), lambda h, lg: (0, h))],
            out_specs=pl.BlockSpec((lp, LANES), lambda h, lg: (0, h)),
            scratch_shapes=[pltpu.VMEM((n_chunks, CHUNK, CHUNK), F32),
                            pltpu.VMEM((CHUNK, CHUNK), F32),
                            pltpu.VMEM((CHUNK, CHUNK), F32),
                            pltpu.VMEM((CHUNK, CHUNK), F32)],
        ),
        out_shape=jax.ShapeDtypeStruct((lp, RET_HEADS * RET_V_DIM), BF16),
        compiler_params=_params(("arbitrary",)),
        name="retention",
    )(log_gamma, qk, qk, rest, rest, gn_g)


def _attention_kernel(q_ref, k_ref, vt_ref, km_ref, vtm_ref, o_ref, s_scr, mx_scr, m_scr, acc_scr,
                      *, tq, tk, n_kc):
    q_all = jnp.concatenate([q_ref[:, g * LANES:(g + 1) * LANES] for g in range(ATTN_GROUP)], axis=0)
    n_strips = (ATTN_GROUP * tq) // ATTN_STRIP
    q_strips = [q_all[s * ATTN_STRIP:(s + 1) * ATTN_STRIP, :] for s in range(n_strips)]

    def scores(k, s):
        return lax.dot_general(k, q_strips[s], (((1,), (1,)), ((), ())), preferred_element_type=F32)

    def key_rows(c):
        return pl.ds(pl.multiple_of(PREFIX + c * tk, LANES), tk)

    def score_stage(c, slot):
        k = k_ref[key_rows(c), :]
        for s in range(n_strips):
            st = scores(k, s)
            s_scr[slot, s] = st
            mx_scr[slot, s] = jnp.max(st, axis=0, keepdims=True)

    def value_stage(c, slot):
        vt = vt_ref[0, :, key_rows(c)]
        for s in range(n_strips):
            m_old = m_scr[s]
            m_new = jnp.maximum(m_old, mx_scr[slot, s])
            p = jnp.exp2(s_scr[slot, s] - m_new).astype(BF16)
            acc_scr[s] = jnp.exp2(m_old - m_new) * acc_scr[s] + jnp.dot(vt, p, preferred_element_type=F32)
            m_scr[s] = m_new

    for s in range(n_strips):
        st = scores(km_ref[0], s)
        m0 = jnp.max(st, axis=0, keepdims=True)
        m_scr[s] = m0
        acc_scr[s] = jnp.dot(vtm_ref[0], jnp.exp2(st - m0).astype(BF16), preferred_element_type=F32)

    score_stage(0, 0)

    def body(i, carry):
        c = 2 * i
        score_stage(c + 1, 1)
        value_stage(c, 0)
        score_stage(c + 2, 0)
        value_stage(c + 1, 1)
        return carry

    lax.fori_loop(0, n_kc // 2 - 1, body, 0)
    score_stage(n_kc - 1, 1)
    value_stage(n_kc - 2, 0)
    value_stage(n_kc - 1, 1)
    outs = []
    for s in range(n_strips):
        acc = acc_scr[s]
        outs.append((acc[0:ATTN_HEAD_DIM, :] / acc[ATTN_HEAD_DIM:ATTN_HEAD_DIM + 1, :]).T)
    o = jnp.concatenate(outs, axis=0)
    for g in range(ATTN_GROUP):
        o_ref[:, g * LANES:(g + 1) * LANES] = o[g * tq:(g + 1) * tq, :]


def _attention(rest, *, tq, tk):
    lp = rest.shape[0]
    n_tok = lp - PREFIX
    assert lp % tq == 0 and n_tok % (2 * tk) == 0 and (ATTN_GROUP * tq) % ATTN_STRIP == 0
    gw = ATTN_GROUP * ATTN_HEAD_DIM
    q_off = (2 * RET_HEADS * RET_V_DIM) // gw
    k_col = 2 * RET_HEADS * RET_V_DIM + ATTN_HEADS * ATTN_HEAD_DIM
    k_off = k_col // LANES
    kv_w = ATTN_KV_HEADS * ATTN_HEAD_DIM
    vt = rest[:, k_col + kv_w:k_col + 2 * kv_w].T.reshape(ATTN_KV_HEADS, ATTN_HEAD_DIM, lp)
    vt = jnp.concatenate([vt, jnp.ones((ATTN_KV_HEADS, ATTN_ONES_ROWS, lp), BF16)], axis=1)
    vt_meta = vt[:, :, PAD:PREFIX]
    k_meta = rest[PAD:PREFIX, k_col:k_col + kv_w].reshape(N_META, ATTN_KV_HEADS, ATTN_HEAD_DIM)
    k_meta = jnp.swapaxes(k_meta, 0, 1)
    n_strips = (ATTN_GROUP * tq) // ATTN_STRIP
    vt_rows = ATTN_HEAD_DIM + ATTN_ONES_ROWS
    return pl.pallas_call(
        functools.partial(_attention_kernel, tq=tq, tk=tk, n_kc=n_tok // tk),
        grid=(ATTN_KV_HEADS, lp // tq),
        in_specs=[pl.BlockSpec((tq, gw), lambda kv, i: (i, q_off + kv)),
                  pl.BlockSpec((lp, LANES), lambda kv, i: (0, k_off + kv)),
                  pl.BlockSpec((1, vt_rows, lp), lambda kv, i: (kv, 0, 0)),
                  pl.BlockSpec((1, N_META, ATTN_HEAD_DIM), lambda kv, i: (kv, 0, 0)),
                  pl.BlockSpec((1, vt_rows, N_META), lambda kv, i: (kv, 0, 0))],
        out_specs=pl.BlockSpec((tq, gw), lambda kv, i: (i, kv)),
        out_shape=jax.ShapeDtypeStruct((lp, ATTN_HEADS * ATTN_HEAD_DIM), F32),
        scratch_shapes=[pltpu.VMEM((2, n_strips, tk, ATTN_STRIP), F32),
                        pltpu.VMEM((2, n_strips, 1, ATTN_STRIP), F32),
                        pltpu.VMEM((n_strips, 1, ATTN_STRIP), F32),
                        pltpu.VMEM((n_strips, vt_rows, ATTN_STRIP), F32)],
        compiler_params=_params(("arbitrary", "arbitrary")),
        name="attention",
    )(rest, rest, vt, k_meta, vt_meta)


def _mix_router_kernel(*refs, tm, n_x):
    pre_ref, x_refs = refs[0], refs[1:1 + n_x]
    (ret_ref, att_ref, ang_ref, wo_ref, fg_ref, wr_ref, br_ref,
     h1_ref, f_ref, route_ref, cnt_ref, carry_scr) = refs[1 + n_x:]
    i = pl.program_id(0)

    @pl.when(i == 0)
    def _():
        carry_scr[...] = jnp.zeros(carry_scr.shape, F32)

    blocks = [jnp.where(i == 0, pre_ref[...], x_refs[0][...])] + [r[...] for r in x_refs[1:]]
    for b0 in range(0, n_x, MIX_SUB_BLOCKS):
        b1 = min(b0 + MIX_SUB_BLOCKS, n_x)
        h_rows = blocks[b0] if b1 - b0 == 1 else jnp.concatenate(blocks[b0:b1], axis=0)
        _mix_router_rows(slice(b0 * PREFIX, b1 * PREFIX), (b1 - b0) * PREFIX, h_rows, ret_ref, att_ref,
                         ang_ref, wo_ref, fg_ref, wr_ref, br_ref, h1_ref, f_ref, route_ref, carry_scr)
    cnt_ref[...] = carry_scr[...]


def _mix_router_rows(rows, tm, h_rows, ret_ref, att_ref, ang_ref, wo_ref, fg_ref, wr_ref, br_ref,
                     h1_ref, f_ref, route_ref, carry_scr):
    att = att_ref[rows, :]
    ms = jnp.mean(att * att, axis=-1, keepdims=True)
    att_n = ((att * lax.rsqrt(ms + NORM_EPS)) * ang_ref[...]).astype(BF16)
    mix_in = jnp.concatenate([ret_ref[rows, :], att_n], axis=1)
    h1 = h_rows + jnp.dot(mix_in, wo_ref[...], preferred_element_type=F32)
    h1_ref[rows, :] = h1
    ms1 = jnp.mean(h1 * h1, axis=-1, keepdims=True)
    f = (h1 * lax.rsqrt(ms1 + NORM_EPS)) * fg_ref[...]
    f_ref[rows, :] = f

    f_hi = f.astype(BF16)
    f_lo = (f - f_hi.astype(F32)).astype(BF16)
    d = f.shape[1]
    k_half = 3 * d // 2
    logits = (jnp.dot(jnp.concatenate([f_hi, f_lo[:, :d // 2]], axis=1), wr_ref[0:k_half, :],
                      preferred_element_type=F32)
              + jnp.dot(jnp.concatenate([f_lo[:, d // 2:], f_hi], axis=1), wr_ref[k_half:, :],
                        preferred_element_type=F32)
              + br_ref[...])
    lane = lax.broadcasted_iota(jnp.int32, (tm, LANES), 1).astype(F32)
    far = float(LANES)
    g_logit = jnp.where(lane < N_GROUPS, logits, NEG_BIG)
    g_max = jnp.max(g_logit, axis=-1, keepdims=True)
    g_w = 1.0 / jnp.sum(jnp.exp(g_logit - g_max), axis=-1, keepdims=True)
    g_idx = jnp.min(jnp.where(g_logit == g_max, lane, far), axis=-1, keepdims=True)
    first = N_GROUPS + g_idx * EXPERTS_PER_GROUP
    e_logit = jnp.where((lane >= first) & (lane < first + EXPERTS_PER_GROUP), logits, NEG_BIG)
    v1 = jnp.max(e_logit, axis=-1, keepdims=True)
    i1 = jnp.min(jnp.where(e_logit == v1, lane, far), axis=-1, keepdims=True)
    e_rest = jnp.where(lane == i1, NEG_BIG, e_logit)
    v2 = jnp.max(e_rest, axis=-1, keepdims=True)
    i2 = jnp.min(jnp.where(e_rest == v2, lane, far), axis=-1, keepdims=True)
    e2 = jnp.exp(v2 - v1)
    w1 = g_w / (1.0 + e2)
    w2 = g_w * e2 / (1.0 + e2)

    oh1 = (lane == i1).astype(F32)
    oh2 = (lane == i2).astype(F32)
    oh = oh1 + oh2
    r_i = lax.broadcasted_iota(jnp.int32, (tm, tm), 0)
    c_i = lax.broadcasted_iota(jnp.int32, (tm, tm), 1)
    lower = (c_i < r_i).astype(BF16)
    before = jnp.dot(lower, oh.astype(BF16), preferred_element_type=F32) + carry_scr[...]
    rank1 = jnp.sum(before * oh1, axis=-1, keepdims=True)
    rank2 = jnp.sum(before * oh2, axis=-1, keepdims=True)
    carry_scr[...] = carry_scr[...] + jnp.sum(oh, axis=0, keepdims=True)

    route = jnp.where(lane == 0, i1 - N_GROUPS, 0.0)
    route = jnp.where(lane == 1, i2 - N_GROUPS, route)
    route = jnp.where(lane == 2, w1, route)
    route = jnp.where(lane == 3, w2, route)
    route = jnp.where(lane == 4, rank1, route)
    route = jnp.where(lane == 5, rank2, route)
    route_ref[rows, :] = route


def _mix_router(pre, x, ret, att, ang, wo_bf, fg, wr, br):
    d = x.shape[1]
    lp = x.shape[0] + PREFIX
    n_x = _stream_blocks(lp)
    tm = n_x * PREFIX
    assert lp % tm == 0
    rowblk = lambda n: pl.BlockSpec((tm, n), lambda i: (i, 0))
    vec = lambda n: pl.BlockSpec((1, n), lambda i: (0, 0))
    full = lambda a: pl.BlockSpec(a.shape, lambda i: (0, 0))
    return pl.pallas_call(
        functools.partial(_mix_router_kernel, tm=tm, n_x=n_x),
        grid=(lp // tm,),
        in_specs=[pl.BlockSpec((PREFIX, d), lambda i: (0, 0)), *_stream_specs(n_x, d),
                  rowblk(ret.shape[1]), rowblk(att.shape[1]), vec(att.shape[1]),
                  pl.BlockSpec(wo_bf.shape, lambda i: (0, 0), pipeline_mode=pl.Buffered(1)),
                  vec(d), full(wr), vec(LANES)],
        out_specs=[rowblk(d), rowblk(d), rowblk(LANES), vec(LANES)],
        out_shape=[jax.ShapeDtypeStruct((lp, d), F32), jax.ShapeDtypeStruct((lp, d), F32),
                   jax.ShapeDtypeStruct((lp, LANES), F32), jax.ShapeDtypeStruct((1, LANES), F32)],
        scratch_shapes=[pltpu.VMEM((1, LANES), F32)],
        compiler_params=_params(("arbitrary",)),
        name="mix_router",
    )(pre, *([x] * n_x), ret, att, ang, wo_bf, fg, wr, br)


def _moe_kernel(ie_ref, irow_ref, irows_ref, sidx_ref, tok_ref, f_ref, wg_hbm, wu_hbm, wd_hbm, o_ref,
                x_scr, y_scr, wg_buf, wu_buf, wd_buf, gsem, ssem, wsem, *, n_items, n_chunks, chunk):
    i = pl.program_id(0)
    c = pl.program_id(1)
    n_rows = irows_ref[i]
    slot = i % 2
    w_slot = (i * n_chunks + c) % 2

    def weights(item, cc, ws, start):
        e = ie_ref[item]
        half = chunk // 2
        col0 = pl.multiple_of(cc * chunk, chunk)
        copies = (
            (wg_hbm.at[e, :, pl.ds(col0, chunk)], wg_buf.at[ws]),
            (wu_hbm.at[e, :, pl.ds(col0, chunk)], wu_buf.at[ws]),
            (wd_hbm.at[e, pl.ds(col0, half), :], wd_buf.at[ws, 0:half, :]),
            (wd_hbm.at[e, pl.ds(col0 + half, half), :], wd_buf.at[ws, half:chunk, :]),
        )

        @pl.when(irows_ref[item] > 0)
        def _():
            for n, (src, dst) in enumerate(copies):
                cp = pltpu.make_async_copy(src, dst, wsem.at[ws, n])
                cp.start(priority=n % 2) if start else cp.wait()

    @pl.when((i == 0) & (c == 0))
    def _():
        weights(0, 0, 0, True)

    @pl.when(c + 1 < n_chunks)
    def _():
        weights(i, c + 1, 1 - w_slot, True)

    @pl.when((c + 1 == n_chunks) & (i + 1 < n_items))
    def _():
        weights(i + 1, 0, 1 - w_slot, True)

    weights(i, c, w_slot, False)

    def tile_row(ref, g, u):
        return ref.at[g, pl.ds(u, 1), :]

    def group_wait(src_rows, dst_rows, sem):
        pltpu.make_async_copy(src_rows, dst_rows, sem).wait()

    def gather(item, dst_slot, start):
        row0 = irow_ref[item]
        dst = x_scr.at[dst_slot]
        sem = gsem.at[dst_slot]

        def group(g, carry):
            if start:
                for u in range(DMA_GROUP):
                    tok = tok_ref[row0 + g * DMA_GROUP + u]
                    pltpu.make_async_copy(f_ref.at[pl.ds(tok, 1), :], tile_row(dst, g, u),
                                          sem).start(priority=u % 2)
            else:
                group_wait(f_ref.at[pl.ds(0, DMA_GROUP), :], dst.at[0], sem)
            return carry

        lax.fori_loop(0, (irows_ref[item] + DMA_GROUP - 1) // DMA_GROUP, group, 0)

    def scatter(item, start):
        row0 = irow_ref[item]
        n = irows_ref[item]
        n_groups = n // DMA_GROUP

        def group(g, carry):
            if start:
                for u in range(DMA_GROUP):
                    a = sidx_ref[row0 + g * DMA_GROUP + u]
                    pltpu.make_async_copy(tile_row(y_scr, g, u), o_ref.at[pl.ds(a, 1), :],
                                          ssem).start(priority=u % 2)
            else:
                group_wait(y_scr.at[0], o_ref.at[pl.ds(0, DMA_GROUP), :], ssem)
            return carry

        lax.fori_loop(0, n_groups, group, 0)

        def single(r, carry):
            cp = pltpu.make_async_copy(y_scr.at[r // DMA_GROUP, pl.ds(r % DMA_GROUP, 1), :],
                                       o_ref.at[pl.ds(sidx_ref[row0 + r], 1), :], ssem)
            cp.start() if start else cp.wait()
            return carry

        lax.fori_loop(n_groups * DMA_GROUP, n, single, 0)

    @pl.when(c == 0)
    def _():
        @pl.when(i == 0)
        def _():
            x_scr[...] = jnp.zeros(x_scr.shape, F32)
            y_scr[...] = jnp.zeros(y_scr.shape, F32)
            gather(0, 0, True)

        @pl.when(i + 1 < n_items)
        def _():
            gather(i + 1, 1 - slot, True)

        gather(i, slot, False)

        @pl.when(i > 0)
        def _():
            scatter(i - 1, False)

    n_blocks = (n_rows + MOE_ROW_STEP - 1) // MOE_ROW_STEP
    for v in range(1, MOE_ITEM_ROWS // MOE_ROW_STEP + 1):
        @pl.when(n_blocks == v)
        def _(m=v * MOE_ROW_STEP):
            mg = m // DMA_GROUP
            d = x_scr.shape[-1]
            x = x_scr[slot, 0:mg].reshape(m, d).astype(BF16)
            gate = jnp.dot(x, wg_buf[w_slot].astype(BF16), preferred_element_type=F32)
            up = jnp.dot(x, wu_buf[w_slot].astype(BF16), preferred_element_type=F32)
            hid = ((gate * jax.nn.sigmoid(gate)) * up).astype(BF16)
            y = jnp.dot(hid, wd_buf[w_slot].astype(BF16), preferred_element_type=F32)
            y_scr[0:mg] = jnp.where(c > 0, y_scr[0:mg], 0.0) + y.reshape(mg, DMA_GROUP, d)

    @pl.when(c == n_chunks - 1)
    def _():
        scatter(i, True)

        @pl.when(i == n_items - 1)
        def _():
            scatter(i, False)


def _moe_experts(item_e, item_row, item_rows, sidx, tok_row, f, w_gate, w_up, w_down, *, chunk):
    lp, d = f.shape
    d_e = w_gate.shape[2]
    n_items = item_e.shape[0]
    n_chunks = d_e // chunk
    return pl.pallas_call(
        functools.partial(_moe_kernel, n_items=n_items, n_chunks=n_chunks, chunk=chunk),
        grid_spec=pltpu.PrefetchScalarGridSpec(
            num_scalar_prefetch=5,
            grid=(n_items, n_chunks),
            in_specs=[pl.BlockSpec(memory_space=pl.ANY)] * 4,
            out_specs=pl.BlockSpec(memory_space=pl.ANY),
            scratch_shapes=[pltpu.VMEM((2, MOE_ITEM_ROWS // DMA_GROUP, DMA_GROUP, d), F32),
                            pltpu.VMEM((MOE_ITEM_ROWS // DMA_GROUP, DMA_GROUP, d), F32),
                            pltpu.VMEM((2, d, chunk), F32),
                            pltpu.VMEM((2, d, chunk), F32),
                            pltpu.VMEM((2, chunk, d), F32),
                            pltpu.SemaphoreType.DMA((2,)),
                            pltpu.SemaphoreType.DMA(()),
                            pltpu.SemaphoreType.DMA((2, 4))],
        ),
        out_shape=jax.ShapeDtypeStruct((TOP_K * lp, d), F32),
        compiler_params=_params(("arbitrary", "arbitrary")),
        name="moe_experts",
    )(item_e, item_row, item_rows, sidx, tok_row, f, w_gate, w_up, w_down)


def _combine_kernel(*refs, n_b):
    h1_refs, route_refs = refs[0:n_b], refs[n_b:2 * n_b]
    y0_refs, y1_refs = refs[2 * n_b:3 * n_b], refs[3 * n_b:4 * n_b]
    g_ref, o_ref = refs[4 * n_b:]
    for b in range(n_b):
        route = route_refs[b][...]
        h2 = h1_refs[b][...] + (route[:, 2:3] * y0_refs[b][0] + route[:, 3:4] * y1_refs[b][0])
        ms = jnp.mean(h2 * h2, axis=-1, keepdims=True)
        o_ref[b * PREFIX:(b + 1) * PREFIX, :] = (h2 * lax.rsqrt(ms + NORM_EPS)) * g_ref[...]


def _combine(h1, route, y, g):
    lp, d = h1.shape
    n_tok = lp - PREFIX
    n_b = max(k for k in (4, 2, 1) if (n_tok // PREFIX) % k == 0)
    y3 = y.reshape(TOP_K, lp, d)
    blk = lambda width: [pl.BlockSpec((PREFIX, width), lambda i, b=b: (n_b * i + 1 + b, 0))
                         for b in range(n_b)]
    y_blk = lambda k: [pl.BlockSpec((1, PREFIX, d), lambda i, b=b: (k, n_b * i + 1 + b, 0))
                       for b in range(n_b)]
    return pl.pallas_call(
        functools.partial(_combine_kernel, n_b=n_b),
        grid=(n_tok // (n_b * PREFIX),),
        in_specs=[*blk(d), *blk(LANES), *y_blk(0), *y_blk(1), pl.BlockSpec((1, d), lambda i: (0, 0))],
        out_specs=pl.BlockSpec((n_b * PREFIX, d), lambda i: (i, 0)),
        out_shape=jax.ShapeDtypeStruct((n_tok, d), F32),
        compiler_params=_params(("arbitrary",)),
        name="combine",
    )(*([h1] * n_b), *([route] * n_b), *([y3] * (2 * n_b)), g)


def _row_tile(lp, candidates):
    for t in candidates:
        if lp % t == 0:
            return t
    raise ValueError(f"no row tile for {lp}")


def kernel(x, meta_tokens, norm_mix_g, w_in, ret_decay_logit_fwd, ret_decay_logit_bwd, ret_norm_g,
           attn_q_norm_g, attn_k_norm_g, attn_out_norm_g, w_out, norm_ffn_g, router_group_w,
           router_group_b, router_expert_w, router_expert_b, expert_w_gate, expert_w_up, expert_w_down,
           norm_final_g):
    batch, n_tok, d = x.shape
    assert batch == 1 and norm_mix_g.shape[0] == 1
    lp = n_tok + PREFIX
    pre = jnp.concatenate([jnp.zeros((PAD, d), x.dtype), meta_tokens.astype(x.dtype)], axis=0)

    cr, sr = _rope_tables(n_tok, RET_QK_DIM)
    ca, sa = _rope_tables(n_tok, ATTN_HEAD_DIM)
    qk, rest = _in_proj(pre, x[0], norm_mix_g, w_in[0].astype(BF16), cr, sr, ca, sa,
                        attn_q_norm_g, attn_k_norm_g)

    log_gamma = jnp.stack([jax.nn.log_sigmoid(ret_decay_logit_fwd[0].astype(F32)),
                           jax.nn.log_sigmoid(ret_decay_logit_bwd[0].astype(F32))])
    ret = _retention(log_gamma, qk, rest, ret_norm_g)
    att = _attention(rest, tq=_row_tile(lp, (320, 128, 64)), tk=_row_tile(n_tok // 2, (1024, 512, 256, 128)))

    n_route = N_GROUPS + N_EXPERTS
    wr = jnp.pad(jnp.concatenate([router_group_w[0], router_expert_w[0]], axis=1).astype(F32),
                 ((0, 0), (0, LANES - n_route)))
    br = jnp.pad(jnp.concatenate([router_group_b[0], router_expert_b[0]]).astype(F32),
                 (0, LANES - n_route))[None]
    wr_hi = wr.astype(BF16)
    wr_lo = (wr - wr_hi.astype(F32)).astype(BF16)
    wr = jnp.concatenate([wr_hi, wr_hi, wr_lo], axis=0)
    h1, f, route, cnt = _mix_router(pre, x[0], ret, att, attn_out_norm_g, w_out[0].astype(BF16),
                                    norm_ffn_g, wr, br)

    expert = route[:, 0:TOP_K].astype(jnp.int32)
    rank = route[:, 4:4 + TOP_K].astype(jnp.int32)
    counts = cnt[0, N_GROUPS:n_route].astype(jnp.int32)
    pad_start = jnp.cumsum(counts) - counts
    is_expert = expert.T[:, :, None] == jnp.arange(N_EXPERTS, dtype=jnp.int32)
    seg_start = jnp.sum(jnp.where(is_expert, pad_start, 0), axis=-1)
    dest = (seg_start + rank.T).reshape(-1)
    sidx = jnp.concatenate([jnp.argsort(dest).astype(jnp.int32), jnp.zeros((DMA_GROUP,), jnp.int32)])
    groups = (counts + MOE_ITEM_ROWS - 1) // MOE_ITEM_ROWS
    g_end = jnp.cumsum(groups)
    g_start = g_end - groups
    n_items = N_EXPERTS + -(-(lp * TOP_K) // MOE_ITEM_ROWS)
    ii = jnp.arange(n_items, dtype=jnp.int32)
    valid = ii < g_end[-1]
    item_e = jnp.sum((g_end[None, :] <= jnp.minimum(ii, g_end[-1] - 1)[:, None]).astype(jnp.int32), axis=1)
    g_in = ii - g_start[item_e]
    item_row = jnp.where(valid, pad_start[item_e] + g_in * MOE_ITEM_ROWS, 0).astype(jnp.int32)
    item_rows = jnp.where(valid, jnp.minimum(MOE_ITEM_ROWS, counts[item_e] - g_in * MOE_ITEM_ROWS),
                          0).astype(jnp.int32)

    tok_row = jnp.where(sidx >= lp, sidx - lp, sidx)
    y = _moe_experts(item_e, item_row, item_rows, sidx, tok_row, f, expert_w_gate[0], expert_w_up[0],
                     expert_w_down[0], chunk=MOE_CHUNK)
    out = _combine(h1, route, y, norm_final_g[None])
    return out[None]
```

```python
import functools

import jax
import jax.numpy as jnp
import numpy as np
from jax import lax
from jax.experimental import pallas as pl
from jax.experimental.pallas import tpu as pltpu

F32 = jnp.float32
BF16 = jnp.bfloat16

N_META = 16
CHUNK = 128
PREFIX = CHUNK
PAD = PREFIX - N_META
GRID_W = 64
ROPE_THETA = 10000.0
NORM_EPS = 1e-6
GROUPNORM_EPS = 1e-5

RET_HEADS = 8
RET_QK_DIM = 64
RET_V_DIM = 128
RET_UNROLL = 65
ATTN_HEADS = 8
ATTN_KV_HEADS = 2
ATTN_HEAD_DIM = 128
ATTN_GROUP = ATTN_HEADS // ATTN_KV_HEADS
ATTN_STRIP = 256
ATTN_ONES_ROWS = 16
LOG2_E = 1.4426950408889634

N_GROUPS = 8
EXPERTS_PER_GROUP = 8
N_EXPERTS = N_GROUPS * EXPERTS_PER_GROUP
TOP_K = 2
MOE_ITEM_ROWS = 512
MOE_CHUNK = 512
MOE_ROW_STEP = 64
DMA_GROUP = 8

LANES = 128
PROJ_TILE = 512
PROJ_SUB_ROWS = (320, 192, 128)
MIX_SUB_BLOCKS = 5
ROUTE_T_ROWS = 8
VMEM_LIMIT = 56 * 1024 * 1024
NEG_BIG = -1e30


def _params(sem, vmem=VMEM_LIMIT):
    return pltpu.CompilerParams(dimension_semantics=sem, vmem_limit_bytes=vmem)


def _swap_halves(x, hw):
    w = x.shape[-1]
    lane = lax.broadcasted_iota(jnp.int32, x.shape, 1)
    first = (lane & hw) == 0
    return jnp.where(first, pltpu.roll(x, w - hw, 1), pltpu.roll(x, hw, 1))


def _rope_tables(n_tok, head_dim):
    half = head_dim // 2
    freqs = ROPE_THETA ** (-np.arange(0, half, 2, dtype=np.float64) / half)
    rows = n_tok // GRID_W
    pos_r = np.concatenate([np.zeros(PREFIX), np.repeat(np.arange(rows), GRID_W)]).astype(np.float64)
    pos_c = np.concatenate([np.zeros(PREFIX), np.tile(np.arange(GRID_W), rows)]).astype(np.float64)
    ang_r = pos_r[:, None] * freqs[None, :]
    ang_c = pos_c[:, None] * freqs[None, :]
    cos = np.concatenate([np.cos(ang_r), np.cos(ang_r), np.cos(ang_c), np.cos(ang_c)], axis=-1)
    sin = np.concatenate([-np.sin(ang_r), np.sin(ang_r), -np.sin(ang_c), np.sin(ang_c)], axis=-1)
    reps = LANES // head_dim
    return (jnp.asarray(np.tile(cos, (1, reps)).astype(np.float32)),
            jnp.asarray(np.tile(sin, (1, reps)).astype(np.float32)))


def _stream_specs(n_blocks, d):
    return [pl.BlockSpec((PREFIX, d), lambda i, j=j: (jnp.maximum(n_blocks * i - 1 + j, 0), 0))
            for j in range(n_blocks)]


def _stream_rows(pre_ref, x_refs):
    first = jnp.where(pl.program_id(0) == 0, pre_ref[...], x_refs[0][...])
    return jnp.concatenate([first] + [r[...] for r in x_refs[1:]], axis=0)


def _stream_blocks(lp):
    n_blocks = lp // PREFIX
    return max(k for k in (5, 3, 1) if n_blocks % k == 0)


def _in_proj_kernel(*refs, tm, n_x):
    pre_ref, x_refs = refs[0], refs[1:1 + n_x]
    (g_ref, w_ref, cr_ref, sr_ref, ca_ref, sa_ref, qg_ref, kg_ref,
     oqk_ref, orest_ref, a_scr) = refs[1 + n_x:]
    i = pl.program_id(0)
    x = _stream_rows(pre_ref, x_refs)
    ms = jnp.mean(x * x, axis=-1, keepdims=True)
    a_scr[...] = ((x * lax.rsqrt(ms + NORM_EPS)) * g_ref[...]).astype(BF16)
    n_lane_groups = PROJ_TILE // LANES
    ret_tiles = 2
    plain_tiles = (2 * RET_HEADS * RET_V_DIM) // PROJ_TILE
    q_tiles = (ATTN_HEADS * ATTN_HEAD_DIM) // PROJ_TILE

    def tiled(ref):
        return jnp.concatenate([ref[...]] * n_lane_groups, axis=1)

    ts = max(t for t in PROJ_SUB_ROWS if tm % t == 0)
    lo = lax.broadcasted_iota(jnp.int32, (ts, LANES), 1) < RET_QK_DIM

    for r0 in range(0, tm, ts):
        rows = slice(r0, r0 + ts)
        row = i * tm + r0 + lax.broadcasted_iota(jnp.int32, (ts, 1), 0)
        k_scale = jnp.where(row >= PAD, RET_QK_DIM ** -0.5, 0.0).astype(F32)
        cos_r, sin_r = tiled(cr_ref.at[rows, :]), tiled(sr_ref.at[rows, :])
        cos_a, sin_a = ca_ref[rows, :], sa_ref[rows, :]

        def qk_norm_rope(xg, gain_ref):
            ms_h = jnp.mean(xg * xg, axis=-1, keepdims=True)
            xn = (xg * lax.rsqrt(ms_h + NORM_EPS)) * gain_ref[...]
            return xn * cos_a + _swap_halves(xn, ATTN_HEAD_DIM // 4) * sin_a

        for j in range(w_ref.shape[1] // PROJ_TILE):
            acc = jnp.dot(a_scr[rows, :], w_ref[:, j * PROJ_TILE:(j + 1) * PROJ_TILE],
                          preferred_element_type=F32)
            if j < ret_tiles:
                y = acc * cos_r + _swap_halves(acc, RET_QK_DIM // 4) * sin_r
                if j == 1:
                    y = y * k_scale
                pieces = []
                for c in range(n_lane_groups):
                    xg = y[:, c * LANES:(c + 1) * LANES]
                    xr = pltpu.roll(xg, RET_QK_DIM, 1)
                    pieces.append(jnp.where(lo, xg, xr))
                    pieces.append(jnp.where(lo, xr, xg))
                oqk_ref[rows, 2 * j * PROJ_TILE:2 * (j + 1) * PROJ_TILE] = (
                    jnp.concatenate(pieces, axis=1).astype(BF16))
                continue
            jr = j - ret_tiles
            if jr < plain_tiles:
                out = acc
            elif jr < plain_tiles + q_tiles:
                out = jnp.concatenate(
                    [qk_norm_rope(acc[:, c * LANES:(c + 1) * LANES], qg_ref)
                     * (LOG2_E * ATTN_HEAD_DIM ** -0.5) for c in range(n_lane_groups)], axis=1)
            else:
                pieces = [qk_norm_rope(acc[:, c * LANES:(c + 1) * LANES], kg_ref)
                          for c in range(ATTN_KV_HEADS)]
                pieces.append(acc[:, ATTN_KV_HEADS * LANES:])
                out = jnp.concatenate(pieces, axis=1)
            orest_ref[rows, jr * PROJ_TILE:(jr + 1) * PROJ_TILE] = out.astype(BF16)


def _in_proj(pre, x, g, w_bf, cr, sr, ca, sa, qg, kg):
    d = x.shape[1]
    lp = x.shape[0] + PREFIX
    n_x = _stream_blocks(lp)
    tm = n_x * PREFIX
    n_out = w_bf.shape[1]
    assert lp % tm == 0 and n_out == 9 * PROJ_TILE
    qk_w = 2 * 2 * RET_HEADS * RET_QK_DIM
    rest_w = n_out - 2 * RET_HEADS * RET_QK_DIM
    row_tab = pl.BlockSpec((tm, LANES), lambda i: (i, 0))
    vec = lambda n: pl.BlockSpec((1, n), lambda i: (0, 0))
    return pl.pallas_call(
        functools.partial(_in_proj_kernel, tm=tm, n_x=n_x),
        grid=(lp // tm,),
        in_specs=[
            pl.BlockSpec((PREFIX, d), lambda i: (0, 0)),
            *_stream_specs(n_x, d),
            vec(d),
            pl.BlockSpec((d, n_out), lambda i: (0, 0), pipeline_mode=pl.Buffered(1)),
            row_tab, row_tab, row_tab, row_tab,
            vec(LANES), vec(LANES),
        ],
        out_specs=[
            pl.BlockSpec((tm, qk_w), lambda i: (i, 0)),
            pl.BlockSpec((tm, rest_w), lambda i: (i, 0)),
        ],
        out_shape=[jax.ShapeDtypeStruct((lp, qk_w), BF16), jax.ShapeDtypeStruct((lp, rest_w), BF16)],
        scratch_shapes=[pltpu.VMEM((tm, d), BF16)],
        compiler_params=_params(("arbitrary",)),
        name="in_proj",
    )(pre, *([x] * n_x), g, w_bf, cr, sr, ca, sa, qg, kg)


def _retention_kernel(lg_ref, q_ref, k_ref, v_ref, g_ref, gn_ref, o_ref, st_ref, dk_ref, dq_ref, dm_ref,
                      *, n_chunks):
    h = pl.program_id(0)
    lgf = lg_ref[0, h]
    lgb = lg_ref[1, h]
    half = CHUNK // 2
    row = lax.broadcasted_iota(jnp.int32, (CHUNK, CHUNK), 0).astype(F32)
    lane = lax.broadcasted_iota(jnp.int32, (CHUNK, CHUNK), 1).astype(F32)
    lane_lo = lane < half
    dk_ref[...] = jnp.exp(jnp.where(lane_lo, lgf * (CHUNK - 1.0 - row), lgb * row))
    dq_ref[...] = jnp.exp(jnp.where(lane_lo, lgf * (row + 1.0), lgb * (CHUNK - row)))
    diff = row - lane
    dm_ref[...] = 0.5 * jnp.where(diff >= 0, jnp.exp(lgf * jnp.maximum(diff, 0.0)),
                                  jnp.exp(lgb * jnp.maximum(-diff, 0.0)))
    cf = jnp.exp(jnp.full((half, CHUNK), lgf * CHUNK, F32))
    cb = jnp.exp(jnp.full((half, CHUNK), lgb * CHUNK, F32))

    def rows(n):
        return pl.ds(pl.multiple_of(n * CHUNK, CHUNK), CHUNK)

    def updates(n, c):
        kd = k_ref[rows(n), :].astype(F32) * dk_ref[...]
        st_ref[n] = jnp.dot(kd.T.astype(BF16), v_ref[rows(n), :], preferred_element_type=F32)
        return c

    unroll = max(u for u in range(1, RET_UNROLL + 1) if n_chunks % u == 0)

    def unrolled(fn):
        def body(t, c):
            for u in range(unroll):
                fn(t * unroll + u, c)
            return c
        return body

    lax.fori_loop(0, n_chunks // unroll, unrolled(updates), 0)

    def scan_fwd(n, s):
        u = st_ref[n, 0:half, :]
        st_ref[n, 0:half, :] = s
        return cf * s + u

    lax.fori_loop(0, n_chunks, scan_fwd, jnp.zeros((half, CHUNK), F32))

    def scan_bwd(t, s):
        n = n_chunks - 1 - t
        u = st_ref[n, half:CHUNK, :]
        st_ref[n, half:CHUNK, :] = s
        return cb * s + u

    lax.fori_loop(0, n_chunks, scan_bwd, jnp.zeros((half, CHUNK), F32))

    def outputs(n, c):
        q = q_ref[rows(n), :]
        k = k_ref[rows(n), :]
        v = v_ref[rows(n), :]
        s2 = lax.dot_general(q, k, (((1,), (1,)), ((), ())), preferred_element_type=F32)
        intra = jnp.dot((s2 * dm_ref[...]).astype(BF16), v, preferred_element_type=F32)
        qd = (q.astype(F32) * dq_ref[...]).astype(BF16)
        cross = jnp.dot(qd, st_ref[n].astype(BF16), preferred_element_type=F32)
        o = intra + cross
        mu = jnp.mean(o, axis=-1, keepdims=True)
        dlt = o - mu
        var = jnp.mean(dlt * dlt, axis=-1, keepdims=True)
        on = dlt * lax.rsqrt(var + GROUPNORM_EPS)
        gate = g_ref[rows(n), :].astype(F32)
        o_ref[rows(n), :] = ((gate * jax.nn.sigmoid(gate)) * (on * gn_ref[...])).astype(BF16)
        return c

    lax.fori_loop(0, n_chunks // unroll, unrolled(outputs), 0)


def _retention(log_gamma, qk, rest, gn_g):
    lp = qk.shape[0]
    n_chunks = lp // CHUNK
    col = lambda off: pl.BlockSpec((lp, LANES), lambda h, lg: (0, h + off))
    return pl.pallas_call(
        functools.partial(_retention_kernel, n_chunks=n_chunks),
        grid_spec=pltpu.PrefetchScalarGridSpec(
            num_scalar_prefetch=1,
            grid=(RET_HEADS,),
            in_specs=[col(0), col(RET_HEADS), col(0), col(RET_HEADS),
                      pl.BlockSpec((1, LANES), lambda h, lg: (0, h))],
            out_specs=pl.BlockSpec((lp, LANES), lambda h, lg: (0, h)),
            scratch_shapes=[pltpu.VMEM((n_chunks, CHUNK, CHUNK), F32),
                            pltpu.VMEM((CHUNK, CHUNK), F32),
                            pltpu.VMEM((CHUNK, CHUNK), F32),
                            pltpu.VMEM((CHUNK, CHUNK), F32)],
        ),
        out_shape=jax.ShapeDtypeStruct((lp, RET_HEADS * RET_V_DIM), BF16),
        compiler_params=_params(("arbitrary",)),
        name="retention",
    )(log_gamma, qk, qk, rest, rest, gn_g)


def _attention_kernel(q_ref, k_ref, vt_ref, km_ref, vtm_ref, o_ref, s_scr, mx_scr, m_scr, acc_scr,
                      *, tq, tk, n_kc):
    q_all = jnp.concatenate([q_ref[:, g * LANES:(g + 1) * LANES] for g in range(ATTN_GROUP)], axis=0)
    n_strips = (ATTN_GROUP * tq) // ATTN_STRIP
    q_strips = [q_all[s * ATTN_STRIP:(s + 1) * ATTN_STRIP, :] for s in range(n_strips)]

    def scores(k, s):
        return lax.dot_general(k, q_strips[s], (((1,), (1,)), ((), ())), preferred_element_type=F32)

    def key_rows(c):
        return pl.ds(pl.multiple_of(PREFIX + c * tk, LANES), tk)

    def score_stage(c, slot):
        k = k_ref[key_rows(c), :]
        for s in range(n_strips):
            st = scores(k, s)
            s_scr[slot, s] = st
            mx_scr[slot, s] = jnp.max(st, axis=0, keepdims=True)

    def value_stage(c, slot):
        vt = vt_ref[0, :, key_rows(c)]
        for s in range(n_strips):
            m_old = m_scr[s]
            m_new = jnp.maximum(m_old, mx_scr[slot, s])
            p = jnp.exp2(s_scr[slot, s] - m_new).astype(BF16)
            acc_scr[s] = jnp.exp2(m_old - m_new) * acc_scr[s] + jnp.dot(vt, p, preferred_element_type=F32)
            m_scr[s] = m_new

    for s in range(n_strips):
        st = scores(km_ref[0], s)
        m0 = jnp.max(st, axis=0, keepdims=True)
        m_scr[s] = m0
        acc_scr[s] = jnp.dot(vtm_ref[0], jnp.exp2(st - m0).astype(BF16), preferred_element_type=F32)

    score_stage(0, 0)

    def body(i, carry):
        c = 2 * i
        score_stage(c + 1, 1)
        value_stage(c, 0)
        score_stage(c + 2, 0)
        value_stage(c + 1, 1)
        return carry

    lax.fori_loop(0, n_kc // 2 - 1, body, 0)
    score_stage(n_kc - 1, 1)
    value_stage(n_kc - 2, 0)
    value_stage(n_kc - 1, 1)
    outs = []
    for s in range(n_strips):
        acc = acc_scr[s]
        outs.append((acc[0:ATTN_HEAD_DIM, :] / acc[ATTN_HEAD_DIM:ATTN_HEAD_DIM + 1, :]).T)
    o = jnp.concatenate(outs, axis=0)
    for g in range(ATTN_GROUP):
        o_ref[:, g * LANES:(g + 1) * LANES] = o[g * tq:(g + 1) * tq, :]


def _attention(rest, *, tq, tk):
    lp = rest.shape[0]
    n_tok = lp - PREFIX
    assert lp % tq == 0 and n_tok % (2 * tk) == 0 and (ATTN_GROUP * tq) % ATTN_STRIP == 0
    gw = ATTN_GROUP * ATTN_HEAD_DIM
    q_off = (2 * RET_HEADS * RET_V_DIM) // gw
    k_col = 2 * RET_HEADS * RET_V_DIM + ATTN_HEADS * ATTN_HEAD_DIM
    k_off = k_col // LANES
    kv_w = ATTN_KV_HEADS * ATTN_HEAD_DIM
    vt = rest[:, k_col + kv_w:k_col + 2 * kv_w].T.reshape(ATTN_KV_HEADS, ATTN_HEAD_DIM, lp)
    vt = jnp.concatenate([vt, jnp.ones((ATTN_KV_HEADS, ATTN_ONES_ROWS, lp), BF16)], axis=1)
    vt_meta = vt[:, :, PAD:PREFIX]
    k_meta = rest[PAD:PREFIX, k_col:k_col + kv_w].reshape(N_META, ATTN_KV_HEADS, ATTN_HEAD_DIM)
    k_meta = jnp.swapaxes(k_meta, 0, 1)
    n_strips = (ATTN_GROUP * tq) // ATTN_STRIP
    vt_rows = ATTN_HEAD_DIM + ATTN_ONES_ROWS
    return pl.pallas_call(
        functools.partial(_attention_kernel, tq=tq, tk=tk, n_kc=n_tok // tk),
        grid=(ATTN_KV_HEADS, lp // tq),
        in_specs=[pl.BlockSpec((tq, gw), lambda kv, i: (i, q_off + kv)),
                  pl.BlockSpec((lp, LANES), lambda kv, i: (0, k_off + kv)),
                  pl.BlockSpec((1, vt_rows, lp), lambda kv, i: (kv, 0, 0)),
                  pl.BlockSpec((1, N_META, ATTN_HEAD_DIM), lambda kv, i: (kv, 0, 0)),
                  pl.BlockSpec((1, vt_rows, N_META), lambda kv, i: (kv, 0, 0))],
        out_specs=pl.BlockSpec((tq, gw), lambda kv, i: (i, kv)),
        out_shape=jax.ShapeDtypeStruct((lp, ATTN_HEADS * ATTN_HEAD_DIM), F32),
        scratch_shapes=[pltpu.VMEM((2, n_strips, tk, ATTN_STRIP), F32),
                        pltpu.VMEM((2, n_strips, 1, ATTN_STRIP), F32),
                        pltpu.VMEM((n_strips, 1, ATTN_STRIP), F32),
                        pltpu.VMEM((n_strips, vt_rows, ATTN_STRIP), F32)],
        compiler_params=_params(("arbitrary", "arbitrary")),
        name="attention",
    )(rest, rest, vt, k_meta, vt_meta)


def _mix_router_kernel(*refs, tm, n_x):
    pre_ref, x_refs = refs[0], refs[1:1 + n_x]
    (ret_ref, att_ref, ang_ref, wo_ref, fg_ref, wr_ref, br_ref,
     h1_ref, f_ref, route_ref, route_t_ref, cnt_ref, carry_scr) = refs[1 + n_x:]
    i = pl.program_id(0)

    @pl.when(i == 0)
    def _():
        carry_scr[...] = jnp.zeros(carry_scr.shape, F32)

    blocks = [jnp.where(i == 0, pre_ref[...], x_refs[0][...])] + [r[...] for r in x_refs[1:]]
    for b0 in range(0, n_x, MIX_SUB_BLOCKS):
        b1 = min(b0 + MIX_SUB_BLOCKS, n_x)
        h_rows = blocks[b0] if b1 - b0 == 1 else jnp.concatenate(blocks[b0:b1], axis=0)
        _mix_router_rows(slice(b0 * PREFIX, b1 * PREFIX), (b1 - b0) * PREFIX, h_rows, ret_ref, att_ref,
                         ang_ref, wo_ref, fg_ref, wr_ref, br_ref, h1_ref, f_ref, route_ref, route_t_ref,
                         carry_scr)
    cnt_ref[...] = carry_scr[...]


def _mix_router_rows(rows, tm, h_rows, ret_ref, att_ref, ang_ref, wo_ref, fg_ref, wr_ref, br_ref,
                     h1_ref, f_ref, route_ref, route_t_ref, carry_scr):
    att = att_ref[rows, :]
    ms = jnp.mean(att * att, axis=-1, keepdims=True)
    att_n = ((att * lax.rsqrt(ms + NORM_EPS)) * ang_ref[...]).astype(BF16)
    mix_in = jnp.concatenate([ret_ref[rows, :], att_n], axis=1)
    h1 = h_rows + jnp.dot(mix_in, wo_ref[...], preferred_element_type=F32)
    h1_ref[rows, :] = h1
    ms1 = jnp.mean(h1 * h1, axis=-1, keepdims=True)
    f = (h1 * lax.rsqrt(ms1 + NORM_EPS)) * fg_ref[...]
    f_ref[rows, :] = f

    f_hi = f.astype(BF16)
    f_lo = (f - f_hi.astype(F32)).astype(BF16)
    d = f.shape[1]
    k_half = 3 * d // 2
    logits = (jnp.dot(jnp.concatenate([f_hi, f_lo[:, :d // 2]], axis=1), wr_ref[0:k_half, :],
                      preferred_element_type=F32)
              + jnp.dot(jnp.concatenate([f_lo[:, d // 2:], f_hi], axis=1), wr_ref[k_half:, :],
                        preferred_element_type=F32)
              + br_ref[...])
    lane = lax.broadcasted_iota(jnp.int32, (tm, LANES), 1).astype(F32)
    far = float(LANES)
    g_logit = jnp.where(lane < N_GROUPS, logits, NEG_BIG)
    g_max = jnp.max(g_logit, axis=-1, keepdims=True)
    g_w = 1.0 / jnp.sum(jnp.exp(g_logit - g_max), axis=-1, keepdims=True)
    g_idx = jnp.min(jnp.where(g_logit == g_max, lane, far), axis=-1, keepdims=True)
    first = N_GROUPS + g_idx * EXPERTS_PER_GROUP
    e_logit = jnp.where((lane >= first) & (lane < first + EXPERTS_PER_GROUP), logits, NEG_BIG)
    v1 = jnp.max(e_logit, axis=-1, keepdims=True)
    i1 = jnp.min(jnp.where(e_logit == v1, lane, far), axis=-1, keepdims=True)
    e_rest = jnp.where(lane == i1, NEG_BIG, e_logit)
    v2 = jnp.max(e_rest, axis=-1, keepdims=True)
    i2 = jnp.min(jnp.where(e_rest == v2, lane, far), axis=-1, keepdims=True)
    e2 = jnp.exp(v2 - v1)
    w1 = g_w / (1.0 + e2)
    w2 = g_w * e2 / (1.0 + e2)

    oh1 = (lane == i1).astype(F32)
    oh2 = (lane == i2).astype(F32)
    oh = oh1 + oh2
    r_i = lax.broadcasted_iota(jnp.int32, (tm, tm), 0)
    c_i = lax.broadcasted_iota(jnp.int32, (tm, tm), 1)
    lower = (c_i < r_i).astype(BF16)
    before = jnp.dot(lower, oh.astype(BF16), preferred_element_type=F32) + carry_scr[...]
    rank1 = jnp.sum(before * oh1, axis=-1, keepdims=True)
    rank2 = jnp.sum(before * oh2, axis=-1, keepdims=True)
    carry_scr[...] = carry_scr[...] + jnp.sum(oh, axis=0, keepdims=True)

    route = jnp.where(lane == 0, i1 - N_GROUPS, 0.0)
    route = jnp.where(lane == 1, i2 - N_GROUPS, route)
    route = jnp.where(lane == 2, w1, route)
    route = jnp.where(lane == 3, w2, route)
    route = jnp.where(lane == 4, rank1, route)
    route = jnp.where(lane == 5, rank2, route)
    route_ref[rows, :] = route
    route_t_ref[:, rows] = route.T[0:ROUTE_T_ROWS, :]


def _mix_router(pre, x, ret, att, ang, wo_bf, fg, wr, br):
    d = x.shape[1]
    lp = x.shape[0] + PREFIX
    n_x = _stream_blocks(lp)
    tm = n_x * PREFIX
    assert lp % tm == 0
    rowblk = lambda n: pl.BlockSpec((tm, n), lambda i: (i, 0))
    vec = lambda n: pl.BlockSpec((1, n), lambda i: (0, 0))
    full = lambda a: pl.BlockSpec(a.shape, lambda i: (0, 0))
    return pl.pallas_call(
        functools.partial(_mix_router_kernel, tm=tm, n_x=n_x),
        grid=(lp // tm,),
        in_specs=[pl.BlockSpec((PREFIX, d), lambda i: (0, 0)), *_stream_specs(n_x, d),
                  rowblk(ret.shape[1]), rowblk(att.shape[1]), vec(att.shape[1]),
                  pl.BlockSpec(wo_bf.shape, lambda i: (0, 0), pipeline_mode=pl.Buffered(1)),
                  vec(d), full(wr), vec(LANES)],
        out_specs=[rowblk(d), rowblk(d), rowblk(LANES), pl.BlockSpec((ROUTE_T_ROWS, tm), lambda i: (0, i)),
                   vec(LANES)],
        out_shape=[jax.ShapeDtypeStruct((lp, d), F32), jax.ShapeDtypeStruct((lp, d), F32),
                   jax.ShapeDtypeStruct((lp, LANES), F32), jax.ShapeDtypeStruct((ROUTE_T_ROWS, lp), F32),
                   jax.ShapeDtypeStruct((1, LANES), F32)],
        scratch_shapes=[pltpu.VMEM((1, LANES), F32)],
        compiler_params=_params(("arbitrary",)),
        name="mix_router",
    )(pre, *([x] * n_x), ret, att, ang, wo_bf, fg, wr, br)


def _moe_kernel(ie_ref, irow_ref, irows_ref, sidx_ref, tok_ref, f_ref, wg_hbm, wu_hbm, wd_hbm, o_ref,
                x_scr, y_scr, wg_buf, wu_buf, wd_buf, gsem, ssem, wsem, *, n_items, n_chunks, chunk):
    i = pl.program_id(0)
    c = pl.program_id(1)
    n_rows = irows_ref[i]
    slot = i % 2
    w_slot = (i * n_chunks + c) % 2

    def weights(item, cc, ws, start):
        e = ie_ref[item]
        half = chunk // 2
        col0 = pl.multiple_of(cc * chunk, chunk)
        copies = (
            (wg_hbm.at[e, :, pl.ds(col0, chunk)], wg_buf.at[ws]),
            (wu_hbm.at[e, :, pl.ds(col0, chunk)], wu_buf.at[ws]),
            (wd_hbm.at[e, pl.ds(col0, half), :], wd_buf.at[ws, 0:half, :]),
            (wd_hbm.at[e, pl.ds(col0 + half, half), :], wd_buf.at[ws, half:chunk, :]),
        )

        @pl.when(irows_ref[item] > 0)
        def _():
            for n, (src, dst) in enumerate(copies):
                cp = pltpu.make_async_copy(src, dst, wsem.at[ws, n])
                cp.start(priority=n % 2) if start else cp.wait()

    @pl.when((i == 0) & (c == 0))
    def _():
        weights(0, 0, 0, True)

    @pl.when(c + 1 < n_chunks)
    def _():
        weights(i, c + 1, 1 - w_slot, True)

    @pl.when((c + 1 == n_chunks) & (i + 1 < n_items))
    def _():
        weights(i + 1, 0, 1 - w_slot, True)

    weights(i, c, w_slot, False)

    def tile_row(ref, g, u):
        return ref.at[g, pl.ds(u, 1), :]

    def group_wait(src_rows, dst_rows, sem):
        pltpu.make_async_copy(src_rows, dst_rows, sem).wait()

    def gather(item, dst_slot, start):
        row0 = irow_ref[item]
        dst = x_scr.at[dst_slot]
        sem = gsem.at[dst_slot]

        def group(g, carry):
            if start:
                for u in range(DMA_GROUP):
                    tok = tok_ref[row0 + g * DMA_GROUP + u]
                    pltpu.make_async_copy(f_ref.at[pl.ds(tok, 1), :], tile_row(dst, g, u),
                                          sem).start(priority=u % 2)
            else:
                group_wait(f_ref.at[pl.ds(0, DMA_GROUP), :], dst.at[0], sem)
            return carry

        lax.fori_loop(0, (irows_ref[item] + DMA_GROUP - 1) // DMA_GROUP, group, 0)

    def scatter(item, start):
        row0 = irow_ref[item]
        n = irows_ref[item]
        n_groups = n // DMA_GROUP

        def group(g, carry):
            if start:
                for u in range(DMA_GROUP):
                    a = sidx_ref[row0 + g * DMA_GROUP + u]
                    pltpu.make_async_copy(tile_row(y_scr, g, u), o_ref.at[pl.ds(a, 1), :],
                                          ssem).start(priority=u % 2)
            else:
                group_wait(y_scr.at[0], o_ref.at[pl.ds(0, DMA_GROUP), :], ssem)
            return carry

        lax.fori_loop(0, n_groups, group, 0)

        def single(r, carry):
            cp = pltpu.make_async_copy(y_scr.at[r // DMA_GROUP, pl.ds(r % DMA_GROUP, 1), :],
                                       o_ref.at[pl.ds(sidx_ref[row0 + r], 1), :], ssem)
            cp.start() if start else cp.wait()
            return carry

        lax.fori_loop(n_groups * DMA_GROUP, n, single, 0)

    @pl.when(c == 0)
    def _():
        @pl.when(i == 0)
        def _():
            x_scr[...] = jnp.zeros(x_scr.shape, F32)
            y_scr[...] = jnp.zeros(y_scr.shape, F32)
            gather(0, 0, True)

        @pl.when(i + 1 < n_items)
        def _():
            gather(i + 1, 1 - slot, True)

        gather(i, slot, False)

        @pl.when(i > 0)
        def _():
            scatter(i - 1, False)

    n_blocks = (n_rows + MOE_ROW_STEP - 1) // MOE_ROW_STEP
    for v in range(1, MOE_ITEM_ROWS // MOE_ROW_STEP + 1):
        @pl.when(n_blocks == v)
        def _(m=v * MOE_ROW_STEP):
            mg = m // DMA_GROUP
            d = x_scr.shape[-1]
            x = x_scr[slot, 0:mg].reshape(m, d).astype(BF16)
            gate = jnp.dot(x, wg_buf[w_slot].astype(BF16), preferred_element_type=F32)
            up = jnp.dot(x, wu_buf[w_slot].astype(BF16), preferred_element_type=F32)
            hid = ((gate * jax.nn.sigmoid(gate)) * up).astype(BF16)
            y = jnp.dot(hid, wd_buf[w_slot].astype(BF16), preferred_element_type=F32)
            y_scr[0:mg] = jnp.where(c > 0, y_scr[0:mg], 0.0) + y.reshape(mg, DMA_GROUP, d)

    @pl.when(c == n_chunks - 1)
    def _():
        scatter(i, True)

        @pl.when(i == n_items - 1)
        def _():
            scatter(i, False)


def _moe_experts(item_e, item_row, item_rows, sidx, tok_row, f, w_gate, w_up, w_down, *, chunk):
    lp, d = f.shape
    d_e = w_gate.shape[2]
    n_items = item_e.shape[0]
    n_chunks = d_e // chunk
    return pl.pallas_call(
        functools.partial(_moe_kernel, n_items=n_items, n_chunks=n_chunks, chunk=chunk),
        grid_spec=pltpu.PrefetchScalarGridSpec(
            num_scalar_prefetch=5,
            grid=(n_items, n_chunks),
            in_specs=[pl.BlockSpec(memory_space=pl.ANY)] * 4,
            out_specs=pl.BlockSpec(memory_space=pl.ANY),
            scratch_shapes=[pltpu.VMEM((2, MOE_ITEM_ROWS // DMA_GROUP, DMA_GROUP, d), F32),
                            pltpu.VMEM((MOE_ITEM_ROWS // DMA_GROUP, DMA_GROUP, d), F32),
                            pltpu.VMEM((2, d, chunk), F32),
                            pltpu.VMEM((2, d, chunk), F32),
                            pltpu.VMEM((2, chunk, d), F32),
                            pltpu.SemaphoreType.DMA((2,)),
                            pltpu.SemaphoreType.DMA(()),
                            pltpu.SemaphoreType.DMA((2, 4))],
        ),
        out_shape=jax.ShapeDtypeStruct((TOP_K * lp, d), F32),
        compiler_params=_params(("arbitrary", "arbitrary")),
        name="moe_experts",
    )(item_e, item_row, item_rows, sidx, tok_row, f, w_gate, w_up, w_down)


def _combine_kernel(*refs, n_b):
    h1_refs, route_refs = refs[0:n_b], refs[n_b:2 * n_b]
    y0_refs, y1_refs = refs[2 * n_b:3 * n_b], refs[3 * n_b:4 * n_b]
    g_ref, o_ref = refs[4 * n_b:]
    for b in range(n_b):
        route = route_refs[b][...]
        h2 = h1_refs[b][...] + (route[:, 2:3] * y0_refs[b][0] + route[:, 3:4] * y1_refs[b][0])
        ms = jnp.mean(h2 * h2, axis=-1, keepdims=True)
        o_ref[b * PREFIX:(b + 1) * PREFIX, :] = (h2 * lax.rsqrt(ms + NORM_EPS)) * g_ref[...]


def _combine(h1, route, y, g):
    lp, d = h1.shape
    n_tok = lp - PREFIX
    n_b = max(k for k in (4, 2, 1) if (n_tok // PREFIX) % k == 0)
    y3 = y.reshape(TOP_K, lp, d)
    blk = lambda width: [pl.BlockSpec((PREFIX, width), lambda i, b=b: (n_b * i + 1 + b, 0))
                         for b in range(n_b)]
    y_blk = lambda k: [pl.BlockSpec((1, PREFIX, d), lambda i, b=b: (k, n_b * i + 1 + b, 0))
                       for b in range(n_b)]
    return pl.pallas_call(
        functools.partial(_combine_kernel, n_b=n_b),
        grid=(n_tok // (n_b * PREFIX),),
        in_specs=[*blk(d), *blk(LANES), *y_blk(0), *y_blk(1), pl.BlockSpec((1, d), lambda i: (0, 0))],
        out_specs=pl.BlockSpec((n_b * PREFIX, d), lambda i: (i, 0)),
        out_shape=jax.ShapeDtypeStruct((n_tok, d), F32),
        compiler_params=_params(("arbitrary",)),
        name="combine",
    )(*([h1] * n_b), *([route] * n_b), *([y3] * (2 * n_b)), g)


def _row_tile(lp, candidates):
    for t in candidates:
        if lp % t == 0:
            return t
    raise ValueError(f"no row tile for {lp}")


def kernel(x, meta_tokens, norm_mix_g, w_in, ret_decay_logit_fwd, ret_decay_logit_bwd, ret_norm_g,
           attn_q_norm_g, attn_k_norm_g, attn_out_norm_g, w_out, norm_ffn_g, router_group_w,
           router_group_b, router_expert_w, router_expert_b, expert_w_gate, expert_w_up, expert_w_down,
           norm_final_g):
    batch, n_tok, d = x.shape
    assert batch == 1 and norm_mix_g.shape[0] == 1
    lp = n_tok + PREFIX
    pre = jnp.concatenate([jnp.zeros((PAD, d), x.dtype), meta_tokens.astype(x.dtype)], axis=0)

    cr, sr = _rope_tables(n_tok, RET_QK_DIM)
    ca, sa = _rope_tables(n_tok, ATTN_HEAD_DIM)
    qk, rest = _in_proj(pre, x[0], norm_mix_g, w_in[0].astype(BF16), cr, sr, ca, sa,
                        attn_q_norm_g, attn_k_norm_g)

    log_gamma = jnp.stack([jax.nn.log_sigmoid(ret_decay_logit_fwd[0].astype(F32)),
                           jax.nn.log_sigmoid(ret_decay_logit_bwd[0].astype(F32))])
    ret = _retention(log_gamma, qk, rest, ret_norm_g)
    att = _attention(rest, tq=_row_tile(lp, (320, 128, 64)), tk=_row_tile(n_tok // 2, (1024, 512, 256, 128)))

    n_route = N_GROUPS + N_EXPERTS
    wr = jnp.pad(jnp.concatenate([router_group_w[0], router_expert_w[0]], axis=1).astype(F32),
                 ((0, 0), (0, LANES - n_route)))
    br = jnp.pad(jnp.concatenate([router_group_b[0], router_expert_b[0]]).astype(F32),
                 (0, LANES - n_route))[None]
    wr_hi = wr.astype(BF16)
    wr_lo = (wr - wr_hi.astype(F32)).astype(BF16)
    wr = jnp.concatenate([wr_hi, wr_hi, wr_lo], axis=0)
    h1, f, route, route_t, cnt = _mix_router(pre, x[0], ret, att, attn_out_norm_g, w_out[0].astype(BF16),
                                    norm_ffn_g, wr, br)

    expert = route_t[0:TOP_K].astype(jnp.int32)
    rank = route_t[4:4 + TOP_K].astype(jnp.int32)
    counts = cnt[0, N_GROUPS:n_route].astype(jnp.int32)
    pad_start = jnp.cumsum(counts) - counts
    is_expert = expert[:, :, None] == jnp.arange(N_EXPERTS, dtype=jnp.int32)
    seg_start = jnp.sum(jnp.where(is_expert, pad_start, 0), axis=-1)
    dest = (seg_start + rank).reshape(-1)
    sidx = jnp.concatenate([jnp.argsort(dest).astype(jnp.int32), jnp.zeros((DMA_GROUP,), jnp.int32)])
    groups = (counts + MOE_ITEM_ROWS - 1) // MOE_ITEM_ROWS
    g_end = jnp.cumsum(groups)
    g_start = g_end - groups
    n_items = N_EXPERTS + -(-(lp * TOP_K) // MOE_ITEM_ROWS)
    ii = jnp.arange(n_items, dtype=jnp.int32)
    valid = ii < g_end[-1]
    item_e = jnp.sum((g_end[None, :] <= jnp.minimum(ii, g_end[-1] - 1)[:, None]).astype(jnp.int32), axis=1)
    g_in = ii - g_start[item_e]
    item_row = jnp.where(valid, pad_start[item_e] + g_in * MOE_ITEM_ROWS, 0).astype(jnp.int32)
    item_rows = jnp.where(valid, jnp.minimum(MOE_ITEM_ROWS, counts[item_e] - g_in * MOE_ITEM_ROWS),
                          0).astype(jnp.int32)

    tok_row = jnp.where(sidx >= lp, sidx - lp, sidx)
    y = _moe_experts(item_e, item_row, item_rows, sidx, tok_row, f, expert_w_gate[0], expert_w_up[0],
                     expert_w_down[0], chunk=MOE_CHUNK)
    out = _combine(h1, route, y, norm_final_g[None])
    return out[None]
```

```python
import functools

import jax
import jax.numpy as jnp
import numpy as np
from jax import lax
from jax.experimental import pallas as pl
from jax.experimental.pallas import tpu as pltpu

F32 = jnp.float32
BF16 = jnp.bfloat16

N_META = 16
CHUNK = 128
PREFIX = CHUNK
PAD = PREFIX - N_META
GRID_W = 64
ROPE_THETA = 10000.0
NORM_EPS = 1e-6
GROUPNORM_EPS = 1e-5

RET_HEADS = 8
RET_QK_DIM = 64
RET_V_DIM = 128
RET_UNROLL = 65
ATTN_HEADS = 8
ATTN_KV_HEADS = 2
ATTN_HEAD_DIM = 128
ATTN_GROUP = ATTN_HEADS // ATTN_KV_HEADS
ATTN_STRIP = 256
ATTN_ONES_ROWS = 16
LOG2_E = 1.4426950408889634

N_GROUPS = 8
EXPERTS_PER_GROUP = 8
N_EXPERTS = N_GROUPS * EXPERTS_PER_GROUP
TOP_K = 2
MOE_ITEM_ROWS = 512
MOE_CHUNK = 512
MOE_ROW_STEP = 64
DMA_GROUP = 8
WEIGHT_DMA_PRIORITY = 1
ROW_DMA_PRIORITY = 0

LANES = 128
PROJ_TILE = 512
PROJ_SUB_ROWS = (320, 192, 128)
MIX_SUB_BLOCKS = 5
ROUTE_T_ROWS = 8
VMEM_LIMIT = 56 * 1024 * 1024
NEG_BIG = -1e30


def _params(sem, vmem=VMEM_LIMIT):
    return pltpu.CompilerParams(dimension_semantics=sem, vmem_limit_bytes=vmem)


def _swap_halves(x, hw):
    w = x.shape[-1]
    lane = lax.broadcasted_iota(jnp.int32, x.shape, 1)
    first = (lane & hw) == 0
    return jnp.where(first, pltpu.roll(x, w - hw, 1), pltpu.roll(x, hw, 1))


def _rope_tables(n_tok, head_dim):
    half = head_dim // 2
    freqs = ROPE_THETA ** (-np.arange(0, half, 2, dtype=np.float64) / half)
    rows = n_tok // GRID_W
    pos_r = np.concatenate([np.zeros(PREFIX), np.repeat(np.arange(rows), GRID_W)]).astype(np.float64)
    pos_c = np.concatenate([np.zeros(PREFIX), np.tile(np.arange(GRID_W), rows)]).astype(np.float64)
    ang_r = pos_r[:, None] * freqs[None, :]
    ang_c = pos_c[:, None] * freqs[None, :]
    cos = np.concatenate([np.cos(ang_r), np.cos(ang_r), np.cos(ang_c), np.cos(ang_c)], axis=-1)
    sin = np.concatenate([-np.sin(ang_r), np.sin(ang_r), -np.sin(ang_c), np.sin(ang_c)], axis=-1)
    reps = LANES // head_dim
    return (jnp.asarray(np.tile(cos, (1, reps)).astype(np.float32)),
            jnp.asarray(np.tile(sin, (1, reps)).astype(np.float32)))


def _stream_specs(n_blocks, d):
    return [pl.BlockSpec((PREFIX, d), lambda i, j=j: (jnp.maximum(n_blocks * i - 1 + j, 0), 0))
            for j in range(n_blocks)]


def _stream_rows(pre_ref, x_refs):
    first = jnp.where(pl.program_id(0) == 0, pre_ref[...], x_refs[0][...])
    return jnp.concatenate([first] + [r[...] for r in x_refs[1:]], axis=0)


def _stream_blocks(lp):
    n_blocks = lp // PREFIX
    return max(k for k in (5, 3, 1) if n_blocks % k == 0)


def _in_proj_kernel(*refs, tm, n_x):
    pre_ref, x_refs = refs[0], refs[1:1 + n_x]
    (g_ref, w_ref, cr_ref, sr_ref, ca_ref, sa_ref, qg_ref, kg_ref,
     oqk_ref, orest_ref, a_scr) = refs[1 + n_x:]
    i = pl.program_id(0)
    x = _stream_rows(pre_ref, x_refs)
    ms = jnp.mean(x * x, axis=-1, keepdims=True)
    a_scr[...] = ((x * lax.rsqrt(ms + NORM_EPS)) * g_ref[...]).astype(BF16)
    n_lane_groups = PROJ_TILE // LANES
    ret_tiles = 2
    plain_tiles = (2 * RET_HEADS * RET_V_DIM) // PROJ_TILE
    q_tiles = (ATTN_HEADS * ATTN_HEAD_DIM) // PROJ_TILE

    def tiled(ref):
        return jnp.concatenate([ref[...]] * n_lane_groups, axis=1)

    ts = max(t for t in PROJ_SUB_ROWS if tm % t == 0)
    lo = lax.broadcasted_iota(jnp.int32, (ts, LANES), 1) < RET_QK_DIM

    for r0 in range(0, tm, ts):
        rows = slice(r0, r0 + ts)
        row = i * tm + r0 + lax.broadcasted_iota(jnp.int32, (ts, 1), 0)
        k_scale = jnp.where(row >= PAD, RET_QK_DIM ** -0.5, 0.0).astype(F32)
        cos_r, sin_r = tiled(cr_ref.at[rows, :]), tiled(sr_ref.at[rows, :])
        cos_a, sin_a = ca_ref[rows, :], sa_ref[rows, :]

        def qk_norm_rope(xg, gain_ref):
            ms_h = jnp.mean(xg * xg, axis=-1, keepdims=True)
            xn = (xg * lax.rsqrt(ms_h + NORM_EPS)) * gain_ref[...]
            return xn * cos_a + _swap_halves(xn, ATTN_HEAD_DIM // 4) * sin_a

        for j in range(w_ref.shape[1] // PROJ_TILE):
            acc = jnp.dot(a_scr[rows, :], w_ref[:, j * PROJ_TILE:(j + 1) * PROJ_TILE],
                          preferred_element_type=F32)
            if j < ret_tiles:
                y = acc * cos_r + _swap_halves(acc, RET_QK_DIM // 4) * sin_r
                if j == 1:
                    y = y * k_scale
                pieces = []
                for c in range(n_lane_groups):
                    xg = y[:, c * LANES:(c + 1) * LANES]
                    xr = pltpu.roll(xg, RET_QK_DIM, 1)
                    pieces.append(jnp.where(lo, xg, xr))
                    pieces.append(jnp.where(lo, xr, xg))
                oqk_ref[rows, 2 * j * PROJ_TILE:2 * (j + 1) * PROJ_TILE] = (
                    jnp.concatenate(pieces, axis=1).astype(BF16))
                continue
            jr = j - ret_tiles
            if jr < plain_tiles:
                out = acc
            elif jr < plain_tiles + q_tiles:
                out = jnp.concatenate(
                    [qk_norm_rope(acc[:, c * LANES:(c + 1) * LANES], qg_ref)
                     * (LOG2_E * ATTN_HEAD_DIM ** -0.5) for c in range(n_lane_groups)], axis=1)
            else:
                pieces = [qk_norm_rope(acc[:, c * LANES:(c + 1) * LANES], kg_ref)
                          for c in range(ATTN_KV_HEADS)]
                pieces.append(acc[:, ATTN_KV_HEADS * LANES:])
                out = jnp.concatenate(pieces, axis=1)
            orest_ref[rows, jr * PROJ_TILE:(jr + 1) * PROJ_TILE] = out.astype(BF16)


def _in_proj(pre, x, g, w_bf, cr, sr, ca, sa, qg, kg):
    d = x.shape[1]
    lp = x.shape[0] + PREFIX
    n_x = _stream_blocks(lp)
    tm = n_x * PREFIX
    n_out = w_bf.shape[1]
    assert lp % tm == 0 and n_out == 9 * PROJ_TILE
    qk_w = 2 * 2 * RET_HEADS * RET_QK_DIM
    rest_w = n_out - 2 * RET_HEADS * RET_QK_DIM
    row_tab = pl.BlockSpec((tm, LANES), lambda i: (i, 0))
    vec = lambda n: pl.BlockSpec((1, n), lambda i: (0, 0))
    return pl.pallas_call(
        functools.partial(_in_proj_kernel, tm=tm, n_x=n_x),
        grid=(lp // tm,),
        in_specs=[
            pl.BlockSpec((PREFIX, d), lambda i: (0, 0)),
            *_stream_specs(n_x, d),
            vec(d),
            pl.BlockSpec((d, n_out), lambda i: (0, 0), pipeline_mode=pl.Buffered(1)),
            row_tab, row_tab, row_tab, row_tab,
            vec(LANES), vec(LANES),
        ],
        out_specs=[
            pl.BlockSpec((tm, qk_w), lambda i: (i, 0)),
            pl.BlockSpec((tm, rest_w), lambda i: (i, 0)),
        ],
        out_shape=[jax.ShapeDtypeStruct((lp, qk_w), BF16), jax.ShapeDtypeStruct((lp, rest_w), BF16)],
        scratch_shapes=[pltpu.VMEM((tm, d), BF16)],
        compiler_params=_params(("arbitrary",)),
        name="in_proj",
    )(pre, *([x] * n_x), g, w_bf, cr, sr, ca, sa, qg, kg)


def _retention_kernel(lg_ref, q_ref, k_ref, v_ref, g_ref, gn_ref, o_ref, st_ref, dk_ref, dq_ref, dm_ref,
                      *, n_chunks):
    h = pl.program_id(0)
    lgf = lg_ref[0, h]
    lgb = lg_ref[1, h]
    half = CHUNK // 2
    row = lax.broadcasted_iota(jnp.int32, (CHUNK, CHUNK), 0).astype(F32)
    lane = lax.broadcasted_iota(jnp.int32, (CHUNK, CHUNK), 1).astype(F32)
    lane_lo = lane < half
    dk_ref[...] = jnp.exp(jnp.where(lane_lo, lgf * (CHUNK - 1.0 - row), lgb * row))
    dq_ref[...] = jnp.exp(jnp.where(lane_lo, lgf * (row + 1.0), lgb * (CHUNK - row)))
    diff = row - lane
    dm_ref[...] = 0.5 * jnp.where(diff >= 0, jnp.exp(lgf * jnp.maximum(diff, 0.0)),
                                  jnp.exp(lgb * jnp.maximum(-diff, 0.0)))
    cf = jnp.exp(jnp.full((half, CHUNK), lgf * CHUNK, F32))
    cb = jnp.exp(jnp.full((half, CHUNK), lgb * CHUNK, F32))

    def rows(n):
        return pl.ds(pl.multiple_of(n * CHUNK, CHUNK), CHUNK)

    def updates(n, c):
        kd = k_ref[rows(n), :].astype(F32) * dk_ref[...]
        st_ref[n] = jnp.dot(kd.T.astype(BF16), v_ref[rows(n), :], preferred_element_type=F32)
        return c

    unroll = max(u for u in range(1, RET_UNROLL + 1) if n_chunks % u == 0)

    def unrolled(fn):
        def body(t, c):
            for u in range(unroll):
                fn(t * unroll + u, c)
            return c
        return body

    lax.fori_loop(0, n_chunks // unroll, unrolled(updates), 0)

    def scan_fwd(n, s):
        u = st_ref[n, 0:half, :]
        st_ref[n, 0:half, :] = s
        return cf * s + u

    lax.fori_loop(0, n_chunks, scan_fwd, jnp.zeros((half, CHUNK), F32))

    def scan_bwd(t, s):
        n = n_chunks - 1 - t
        u = st_ref[n, half:CHUNK, :]
        st_ref[n, half:CHUNK, :] = s
        return cb * s + u

    lax.fori_loop(0, n_chunks, scan_bwd, jnp.zeros((half, CHUNK), F32))

    def outputs(n, c):
        q = q_ref[rows(n), :]
        k = k_ref[rows(n), :]
        v = v_ref[rows(n), :]
        s2 = lax.dot_general(q, k, (((1,), (1,)), ((), ())), preferred_element_type=F32)
        intra = jnp.dot((s2 * dm_ref[...]).astype(BF16), v, preferred_element_type=F32)
        qd = (q.astype(F32) * dq_ref[...]).astype(BF16)
        cross = jnp.dot(qd, st_ref[n].astype(BF16), preferred_element_type=F32)
        o = intra + cross
        mu = jnp.mean(o, axis=-1, keepdims=True)
        dlt = o - mu
        var = jnp.mean(dlt * dlt, axis=-1, keepdims=True)
        on = dlt * lax.rsqrt(var + GROUPNORM_EPS)
        gate = g_ref[rows(n), :].astype(F32)
        o_ref[rows(n), :] = ((gate * jax.nn.sigmoid(gate)) * (on * gn_ref[...])).astype(BF16)
        return c

    lax.fori_loop(0, n_chunks // unroll, unrolled(outputs), 0)


def _retention(log_gamma, qk, rest, gn_g):
    lp = qk.shape[0]
    n_chunks = lp // CHUNK
    col = lambda off: pl.BlockSpec((lp, LANES), lambda h, lg: (0, h + off))
    return pl.pallas_call(
        functools.partial(_retention_kernel, n_chunks=n_chunks),
        grid_spec=pltpu.PrefetchScalarGridSpec(
            num_scalar_prefetch=1,
            grid=(RET_HEADS,),
            in_specs=[col(0), col(RET_HEADS), col(0), col(RET_HEADS),
                      pl.BlockSpec((1, LANES), lambda h, lg: (0, h))],
            out_specs=pl.BlockSpec((lp, LANES), lambda h, lg: (0, h)),
            scratch_shapes=[pltpu.VMEM((n_chunks, CHUNK, CHUNK), F32),
                            pltpu.VMEM((CHUNK, CHUNK), F32),
                            pltpu.VMEM((CHUNK, CHUNK), F32),
                            pltpu.VMEM((CHUNK, CHUNK), F32)],
        ),
        out_shape=jax.ShapeDtypeStruct((lp, RET_HEADS * RET_V_DIM), BF16),
        compiler_params=_params(("arbitrary",)),
        name="retention",
    )(log_gamma, qk, qk, rest, rest, gn_g)


def _attention_kernel(q_ref, k_ref, vt_ref, km_ref, vtm_ref, o_ref, s_scr, mx_scr, m_scr, acc_scr,
                      *, tq, tk, n_kc):
    q_all = jnp.concatenate([q_ref[:, g * LANES:(g + 1) * LANES] for g in range(ATTN_GROUP)], axis=0)
    n_strips = (ATTN_GROUP * tq) // ATTN_STRIP
    q_strips = [q_all[s * ATTN_STRIP:(s + 1) * ATTN_STRIP, :] for s in range(n_strips)]

    def scores(k, s):
        return lax.dot_general(k, q_strips[s], (((1,), (1,)), ((), ())), preferred_element_type=F32)

    def key_rows(c):
        return pl.ds(pl.multiple_of(PREFIX + c * tk, LANES), tk)

    def score_stage(c, slot):
        k = k_ref[key_rows(c), :]
        for s in range(n_strips):
            st = scores(k, s)
            s_scr[slot, s] = st
            mx_scr[slot, s] = jnp.max(st, axis=0, keepdims=True)

    def value_stage(c, slot):
        vt = vt_ref[0, :, key_rows(c)]
        for s in range(n_strips):
            m_old = m_scr[s]
            m_new = jnp.maximum(m_old, mx_scr[slot, s])
            p = jnp.exp2(s_scr[slot, s] - m_new).astype(BF16)
            acc_scr[s] = jnp.exp2(m_old - m_new) * acc_scr[s] + jnp.dot(vt, p, preferred_element_type=F32)
            m_scr[s] = m_new

    for s in range(n_strips):
        st = scores(km_ref[0], s)
        m0 = jnp.max(st, axis=0, keepdims=True)
        m_scr[s] = m0
        acc_scr[s] = jnp.dot(vtm_ref[0], jnp.exp2(st - m0).astype(BF16), preferred_element_type=F32)

    score_stage(0, 0)

    def body(i, carry):
        c = 2 * i
        score_stage(c + 1, 1)
        value_stage(c, 0)
        score_stage(c + 2, 0)
        value_stage(c + 1, 1)
        return carry

    lax.fori_loop(0, n_kc // 2 - 1, body, 0)
    score_stage(n_kc - 1, 1)
    value_stage(n_kc - 2, 0)
    value_stage(n_kc - 1, 1)
    outs = []
    for s in range(n_strips):
        acc = acc_scr[s]
        outs.append((acc[0:ATTN_HEAD_DIM, :] / acc[ATTN_HEAD_DIM:ATTN_HEAD_DIM + 1, :]).T)
    o = jnp.concatenate(outs, axis=0)
    for g in range(ATTN_GROUP):
        o_ref[:, g * LANES:(g + 1) * LANES] = o[g * tq:(g + 1) * tq, :]


def _attention(rest, *, tq, tk):
    lp = rest.shape[0]
    n_tok = lp - PREFIX
    assert lp % tq == 0 and n_tok % (2 * tk) == 0 and (ATTN_GROUP * tq) % ATTN_STRIP == 0
    gw = ATTN_GROUP * ATTN_HEAD_DIM
    q_off = (2 * RET_HEADS * RET_V_DIM) // gw
    k_col = 2 * RET_HEADS * RET_V_DIM + ATTN_HEADS * ATTN_HEAD_DIM
    k_off = k_col // LANES
    kv_w = ATTN_KV_HEADS * ATTN_HEAD_DIM
    vt = rest[:, k_col + kv_w:k_col + 2 * kv_w].T.reshape(ATTN_KV_HEADS, ATTN_HEAD_DIM, lp)
    vt = jnp.concatenate([vt, jnp.ones((ATTN_KV_HEADS, ATTN_ONES_ROWS, lp), BF16)], axis=1)
    vt_meta = vt[:, :, PAD:PREFIX]
    k_meta = rest[PAD:PREFIX, k_col:k_col + kv_w].reshape(N_META, ATTN_KV_HEADS, ATTN_HEAD_DIM)
    k_meta = jnp.swapaxes(k_meta, 0, 1)
    n_strips = (ATTN_GROUP * tq) // ATTN_STRIP
    vt_rows = ATTN_HEAD_DIM + ATTN_ONES_ROWS
    return pl.pallas_call(
        functools.partial(_attention_kernel, tq=tq, tk=tk, n_kc=n_tok // tk),
        grid=(ATTN_KV_HEADS, lp // tq),
        in_specs=[pl.BlockSpec((tq, gw), lambda kv, i: (i, q_off + kv)),
                  pl.BlockSpec((lp, LANES), lambda kv, i: (0, k_off + kv)),
                  pl.BlockSpec((1, vt_rows, lp), lambda kv, i: (kv, 0, 0)),
                  pl.BlockSpec((1, N_META, ATTN_HEAD_DIM), lambda kv, i: (kv, 0, 0)),
                  pl.BlockSpec((1, vt_rows, N_META), lambda kv, i: (kv, 0, 0))],
        out_specs=pl.BlockSpec((tq, gw), lambda kv, i: (i, kv)),
        out_shape=jax.ShapeDtypeStruct((lp, ATTN_HEADS * ATTN_HEAD_DIM), F32),
        scratch_shapes=[pltpu.VMEM((2, n_strips, tk, ATTN_STRIP), F32),
                        pltpu.VMEM((2, n_strips, 1, ATTN_STRIP), F32),
                        pltpu.VMEM((n_strips, 1, ATTN_STRIP), F32),
                        pltpu.VMEM((n_strips, vt_rows, ATTN_STRIP), F32)],
        compiler_params=_params(("arbitrary", "arbitrary")),
        name="attention",
    )(rest, rest, vt, k_meta, vt_meta)


def _mix_router_kernel(*refs, tm, n_x):
    pre_ref, x_refs = refs[0], refs[1:1 + n_x]
    (ret_ref, att_ref, ang_ref, wo_ref, fg_ref, wr_ref, br_ref,
     h1_ref, f_ref, route_ref, route_t_ref, cnt_ref, carry_scr) = refs[1 + n_x:]
    i = pl.program_id(0)

    @pl.when(i == 0)
    def _():
        carry_scr[...] = jnp.zeros(carry_scr.shape, F32)

    blocks = [jnp.where(i == 0, pre_ref[...], x_refs[0][...])] + [r[...] for r in x_refs[1:]]
    for b0 in range(0, n_x, MIX_SUB_BLOCKS):
        b1 = min(b0 + MIX_SUB_BLOCKS, n_x)
        h_rows = blocks[b0] if b1 - b0 == 1 else jnp.concatenate(blocks[b0:b1], axis=0)
        _mix_router_rows(slice(b0 * PREFIX, b1 * PREFIX), (b1 - b0) * PREFIX, h_rows, ret_ref, att_ref,
                         ang_ref, wo_ref, fg_ref, wr_ref, br_ref, h1_ref, f_ref, route_ref, route_t_ref,
                         carry_scr)
    cnt_ref[...] = carry_scr[...]


def _mix_router_rows(rows, tm, h_rows, ret_ref, att_ref, ang_ref, wo_ref, fg_ref, wr_ref, br_ref,
                     h1_ref, f_ref, route_ref, route_t_ref, carry_scr):
    att = att_ref[rows, :]
    ms = jnp.mean(att * att, axis=-1, keepdims=True)
    att_n = ((att * lax.rsqrt(ms + NORM_EPS)) * ang_ref[...]).astype(BF16)
    mix_in = jnp.concatenate([ret_ref[rows, :], att_n], axis=1)
    h1 = h_rows + jnp.dot(mix_in, wo_ref[...], preferred_element_type=F32)
    h1_ref[rows, :] = h1
    ms1 = jnp.mean(h1 * h1, axis=-1, keepdims=True)
    f = (h1 * lax.rsqrt(ms1 + NORM_EPS)) * fg_ref[...]
    f_ref[rows, :] = f

    f_hi = f.astype(BF16)
    f_lo = (f - f_hi.astype(F32)).astype(BF16)
    d = f.shape[1]
    k_half = 3 * d // 2
    logits = (jnp.dot(jnp.concatenate([f_hi, f_lo[:, :d // 2]], axis=1), wr_ref[0:k_half, :],
                      preferred_element_type=F32)
              + jnp.dot(jnp.concatenate([f_lo[:, d // 2:], f_hi], axis=1), wr_ref[k_half:, :],
                        preferred_element_type=F32)
              + br_ref[...])
    lane = lax.broadcasted_iota(jnp.int32, (tm, LANES), 1).astype(F32)
    far = float(LANES)
    g_logit = jnp.where(lane < N_GROUPS, logits, NEG_BIG)
    g_max = jnp.max(g_logit, axis=-1, keepdims=True)
    g_w = 1.0 / jnp.sum(jnp.exp(g_logit - g_max), axis=-1, keepdims=True)
    g_idx = jnp.min(jnp.where(g_logit == g_max, lane, far), axis=-1, keepdims=True)
    first = N_GROUPS + g_idx * EXPERTS_PER_GROUP
    e_logit = jnp.where((lane >= first) & (lane < first + EXPERTS_PER_GROUP), logits, NEG_BIG)
    v1 = jnp.max(e_logit, axis=-1, keepdims=True)
    i1 = jnp.min(jnp.where(e_logit == v1, lane, far), axis=-1, keepdims=True)
    e_rest = jnp.where(lane == i1, NEG_BIG, e_logit)
    v2 = jnp.max(e_rest, axis=-1, keepdims=True)
    i2 = jnp.min(jnp.where(e_rest == v2, lane, far), axis=-1, keepdims=True)
    e2 = jnp.exp(v2 - v1)
    w1 = g_w / (1.0 + e2)
    w2 = g_w * e2 / (1.0 + e2)

    oh1 = (lane == i1).astype(F32)
    oh2 = (lane == i2).astype(F32)
    oh = oh1 + oh2
    r_i = lax.broadcasted_iota(jnp.int32, (tm, tm), 0)
    c_i = lax.broadcasted_iota(jnp.int32, (tm, tm), 1)
    lower = (c_i < r_i).astype(BF16)
    before = jnp.dot(lower, oh.astype(BF16), preferred_element_type=F32) + carry_scr[...]
    rank1 = jnp.sum(before * oh1, axis=-1, keepdims=True)
    rank2 = jnp.sum(before * oh2, axis=-1, keepdims=True)
    carry_scr[...] = carry_scr[...] + jnp.sum(oh, axis=0, keepdims=True)

    route = jnp.where(lane == 0, i1 - N_GROUPS, 0.0)
    route = jnp.where(lane == 1, i2 - N_GROUPS, route)
    route = jnp.where(lane == 2, w1, route)
    route = jnp.where(lane == 3, w2, route)
    route = jnp.where(lane == 4, rank1, route)
    route = jnp.where(lane == 5, rank2, route)
    route_ref[rows, :] = route
    route_t_ref[:, rows] = route.T[0:ROUTE_T_ROWS, :]


def _mix_router(pre, x, ret, att, ang, wo_bf, fg, wr, br):
    d = x.shape[1]
    lp = x.shape[0] + PREFIX
    n_x = _stream_blocks(lp)
    tm = n_x * PREFIX
    assert lp % tm == 0
    rowblk = lambda n: pl.BlockSpec((tm, n), lambda i: (i, 0))
    vec = lambda n: pl.BlockSpec((1, n), lambda i: (0, 0))
    full = lambda a: pl.BlockSpec(a.shape, lambda i: (0, 0))
    return pl.pallas_call(
        functools.partial(_mix_router_kernel, tm=tm, n_x=n_x),
        grid=(lp // tm,),
        in_specs=[pl.BlockSpec((PREFIX, d), lambda i: (0, 0)), *_stream_specs(n_x, d),
                  rowblk(ret.shape[1]), rowblk(att.shape[1]), vec(att.shape[1]),
                  pl.BlockSpec(wo_bf.shape, lambda i: (0, 0), pipeline_mode=pl.Buffered(1)),
                  vec(d), full(wr), vec(LANES)],
        out_specs=[rowblk(d), rowblk(d), rowblk(LANES), pl.BlockSpec((ROUTE_T_ROWS, tm), lambda i: (0, i)),
                   vec(LANES)],
        out_shape=[jax.ShapeDtypeStruct((lp, d), F32), jax.ShapeDtypeStruct((lp, d), F32),
                   jax.ShapeDtypeStruct((lp, LANES), F32), jax.ShapeDtypeStruct((ROUTE_T_ROWS, lp), F32),
                   jax.ShapeDtypeStruct((1, LANES), F32)],
        scratch_shapes=[pltpu.VMEM((1, LANES), F32)],
        compiler_params=_params(("arbitrary",)),
        name="mix_router",
    )(pre, *([x] * n_x), ret, att, ang, wo_bf, fg, wr, br)


def _moe_kernel(ie_ref, irow_ref, irows_ref, sidx_ref, tok_ref, f_ref, wg_hbm, wu_hbm, wd_hbm, o_ref,
                x_scr, y_scr, wg_buf, wu_buf, wd_buf, gsem, ssem, wsem, *, n_items, n_chunks, chunk):
    i = pl.program_id(0)
    c = pl.program_id(1)
    n_rows = irows_ref[i]
    slot = i % 2
    w_slot = (i * n_chunks + c) % 2

    def weights(item, cc, ws, start):
        e = ie_ref[item]
        half = chunk // 2
        col0 = pl.multiple_of(cc * chunk, chunk)
        copies = (
            (wg_hbm.at[e, :, pl.ds(col0, chunk)], wg_buf.at[ws]),
            (wu_hbm.at[e, :, pl.ds(col0, chunk)], wu_buf.at[ws]),
            (wd_hbm.at[e, pl.ds(col0, half), :], wd_buf.at[ws, 0:half, :]),
            (wd_hbm.at[e, pl.ds(col0 + half, half), :], wd_buf.at[ws, half:chunk, :]),
        )

        @pl.when(irows_ref[item] > 0)
        def _():
            for n, (src, dst) in enumerate(copies):
                cp = pltpu.make_async_copy(src, dst, wsem.at[ws, n])
                cp.start(priority=WEIGHT_DMA_PRIORITY) if start else cp.wait()

    @pl.when((i == 0) & (c == 0))
    def _():
        weights(0, 0, 0, True)

    @pl.when(c + 1 < n_chunks)
    def _():
        weights(i, c + 1, 1 - w_slot, True)

    @pl.when((c + 1 == n_chunks) & (i + 1 < n_items))
    def _():
        weights(i + 1, 0, 1 - w_slot, True)

    weights(i, c, w_slot, False)

    def tile_row(ref, g, u):
        return ref.at[g, pl.ds(u, 1), :]

    def group_wait(src_rows, dst_rows, sem):
        pltpu.make_async_copy(src_rows, dst_rows, sem).wait()

    def gather(item, dst_slot, start):
        row0 = irow_ref[item]
        dst = x_scr.at[dst_slot]
        sem = gsem.at[dst_slot]

        def group(g, carry):
            if start:
                for u in range(DMA_GROUP):
                    tok = tok_ref[row0 + g * DMA_GROUP + u]
                    pltpu.make_async_copy(f_ref.at[pl.ds(tok, 1), :], tile_row(dst, g, u),
                                          sem).start(priority=ROW_DMA_PRIORITY)
            else:
                group_wait(f_ref.at[pl.ds(0, DMA_GROUP), :], dst.at[0], sem)
            return carry

        lax.fori_loop(0, (irows_ref[item] + DMA_GROUP - 1) // DMA_GROUP, group, 0)

    def scatter(item, start):
        row0 = irow_ref[item]
        n = irows_ref[item]
        n_groups = n // DMA_GROUP

        def group(g, carry):
            if start:
                for u in range(DMA_GROUP):
                    a = sidx_ref[row0 + g * DMA_GROUP + u]
                    pltpu.make_async_copy(tile_row(y_scr, g, u), o_ref.at[pl.ds(a, 1), :],
                                          ssem).start(priority=ROW_DMA_PRIORITY)
            else:
                group_wait(y_scr.at[0], o_ref.at[pl.ds(0, DMA_GROUP), :], ssem)
            return carry

        lax.fori_loop(0, n_groups, group, 0)

        def single(r, carry):
            cp = pltpu.make_async_copy(y_scr.at[r // DMA_GROUP, pl.ds(r % DMA_GROUP, 1), :],
                                       o_ref.at[pl.ds(sidx_ref[row0 + r], 1), :], ssem)
            cp.start() if start else cp.wait()
            return carry

        lax.fori_loop(n_groups * DMA_GROUP, n, single, 0)

    @pl.when(c == 0)
    def _():
        @pl.when(i == 0)
        def _():
            x_scr[...] = jnp.zeros(x_scr.shape, F32)
            y_scr[...] = jnp.zeros(y_scr.shape, F32)
            gather(0, 0, True)

        @pl.when(i + 1 < n_items)
        def _():
            gather(i + 1, 1 - slot, True)

        gather(i, slot, False)

        @pl.when(i > 0)
        def _():
            scatter(i - 1, False)

    n_blocks = (n_rows + MOE_ROW_STEP - 1) // MOE_ROW_STEP
    for v in range(1, MOE_ITEM_ROWS // MOE_ROW_STEP + 1):
        @pl.when(n_blocks == v)
        def _(m=v * MOE_ROW_STEP):
            mg = m // DMA_GROUP
            d = x_scr.shape[-1]
            x = x_scr[slot, 0:mg].reshape(m, d).astype(BF16)
            gate = jnp.dot(x, wg_buf[w_slot].astype(BF16), preferred_element_type=F32)
            up = jnp.dot(x, wu_buf[w_slot].astype(BF16), preferred_element_type=F32)
            hid = ((gate * jax.nn.sigmoid(gate)) * up).astype(BF16)
            y = jnp.dot(hid, wd_buf[w_slot].astype(BF16), preferred_element_type=F32)
            y_scr[0:mg] = jnp.where(c > 0, y_scr[0:mg], 0.0) + y.reshape(mg, DMA_GROUP, d)

    @pl.when(c == n_chunks - 1)
    def _():
        scatter(i, True)

        @pl.when(i == n_items - 1)
        def _():
            scatter(i, False)


def _moe_experts(item_e, item_row, item_rows, sidx, tok_row, f, w_gate, w_up, w_down, *, chunk):
    lp, d = f.shape
    d_e = w_gate.shape[2]
    n_items = item_e.shape[0]
    n_chunks = d_e // chunk
    return pl.pallas_call(
        functools.partial(_moe_kernel, n_items=n_items, n_chunks=n_chunks, chunk=chunk),
        grid_spec=pltpu.PrefetchScalarGridSpec(
            num_scalar_prefetch=5,
            grid=(n_items, n_chunks),
            in_specs=[pl.BlockSpec(memory_space=pl.ANY)] * 4,
            out_specs=pl.BlockSpec(memory_space=pl.ANY),
            scratch_shapes=[pltpu.VMEM((2, MOE_ITEM_ROWS // DMA_GROUP, DMA_GROUP, d), F32),
                            pltpu.VMEM((MOE_ITEM_ROWS // DMA_GROUP, DMA_GROUP, d), F32),
                            pltpu.VMEM((2, d, chunk), F32),
                            pltpu.VMEM((2, d, chunk), F32),
                            pltpu.VMEM((2, chunk, d), F32),
                            pltpu.SemaphoreType.DMA((2,)),
                            pltpu.SemaphoreType.DMA(()),
                            pltpu.SemaphoreType.DMA((2, 4))],
        ),
        out_shape=jax.ShapeDtypeStruct((TOP_K * lp, d), F32),
        compiler_params=_params(("arbitrary", "arbitrary")),
        name="moe_experts",
    )(item_e, item_row, item_rows, sidx, tok_row, f, w_gate, w_up, w_down)


def _combine_kernel(*refs, n_b):
    h1_refs, route_refs = refs[0:n_b], refs[n_b:2 * n_b]
    y0_refs, y1_refs = refs[2 * n_b:3 * n_b], refs[3 * n_b:4 * n_b]
    g_ref, o_ref = refs[4 * n_b:]
    for b in range(n_b):
        route = route_refs[b][...]
        h2 = h1_refs[b][...] + (route[:, 2:3] * y0_refs[b][0] + route[:, 3:4] * y1_refs[b][0])
        ms = jnp.mean(h2 * h2, axis=-1, keepdims=True)
        o_ref[b * PREFIX:(b + 1) * PREFIX, :] = (h2 * lax.rsqrt(ms + NORM_EPS)) * g_ref[...]


def _combine(h1, route, y, g):
    lp, d = h1.shape
    n_tok = lp - PREFIX
    n_b = max(k for k in (4, 2, 1) if (n_tok // PREFIX) % k == 0)
    y3 = y.reshape(TOP_K, lp, d)
    blk = lambda width: [pl.BlockSpec((PREFIX, width), lambda i, b=b: (n_b * i + 1 + b, 0))
                         for b in range(n_b)]
    y_blk = lambda k: [pl.BlockSpec((1, PREFIX, d), lambda i, b=b: (k, n_b * i + 1 + b, 0))
                       for b in range(n_b)]
    return pl.pallas_call(
        functools.partial(_combine_kernel, n_b=n_b),
        grid=(n_tok // (n_b * PREFIX),),
        in_specs=[*blk(d), *blk(LANES), *y_blk(0), *y_blk(1), pl.BlockSpec((1, d), lambda i: (0, 0))],
        out_specs=pl.BlockSpec((n_b * PREFIX, d), lambda i: (i, 0)),
        out_shape=jax.ShapeDtypeStruct((n_tok, d), F32),
        compiler_params=_params(("arbitrary",)),
        name="combine",
    )(*([h1] * n_b), *([route] * n_b), *([y3] * (2 * n_b)), g)


def _row_tile(lp, candidates):
    for t in candidates:
        if lp % t == 0:
            return t
    raise ValueError(f"no row tile for {lp}")


def kernel(x, meta_tokens, norm_mix_g, w_in, ret_decay_logit_fwd, ret_decay_logit_bwd, ret_norm_g,
           attn_q_norm_g, attn_k_norm_g, attn_out_norm_g, w_out, norm_ffn_g, router_group_w,
           router_group_b, router_expert_w, router_expert_b, expert_w_gate, expert_w_up, expert_w_down,
           norm_final_g):
    batch, n_tok, d = x.shape
    assert batch == 1 and norm_mix_g.shape[0] == 1
    lp = n_tok + PREFIX
    pre = jnp.concatenate([jnp.zeros((PAD, d), x.dtype), meta_tokens.astype(x.dtype)], axis=0)

    cr, sr = _rope_tables(n_tok, RET_QK_DIM)
    ca, sa = _rope_tables(n_tok, ATTN_HEAD_DIM)
    qk, rest = _in_proj(pre, x[0], norm_mix_g, w_in[0].astype(BF16), cr, sr, ca, sa,
                        attn_q_norm_g, attn_k_norm_g)

    log_gamma = jnp.stack([jax.nn.log_sigmoid(ret_decay_logit_fwd[0].astype(F32)),
                           jax.nn.log_sigmoid(ret_decay_logit_bwd[0].astype(F32))])
    ret = _retention(log_gamma, qk, rest, ret_norm_g)
    att = _attention(rest, tq=_row_tile(lp, (320, 128, 64)), tk=_row_tile(n_tok // 2, (1024, 512, 256, 128)))

    n_route = N_GROUPS + N_EXPERTS
    wr = jnp.pad(jnp.concatenate([router_group_w[0], router_expert_w[0]], axis=1).astype(F32),
                 ((0, 0), (0, LANES - n_route)))
    br = jnp.pad(jnp.concatenate([router_group_b[0], router_expert_b[0]]).astype(F32),
                 (0, LANES - n_route))[None]
    wr_hi = wr.astype(BF16)
    wr_lo = (wr - wr_hi.astype(F32)).astype(BF16)
    wr = jnp.concatenate([wr_hi, wr_hi, wr_lo], axis=0)
    h1, f, route, route_t, cnt = _mix_router(pre, x[0], ret, att, attn_out_norm_g, w_out[0].astype(BF16),
                                    norm_ffn_g, wr, br)

    expert = route_t[0:TOP_K].astype(jnp.int32)
    rank = route_t[4:4 + TOP_K].astype(jnp.int32)
    counts = cnt[0, N_GROUPS:n_route].astype(jnp.int32)
    pad_start = jnp.cumsum(counts) - counts
    is_expert = expert[:, :, None] == jnp.arange(N_EXPERTS, dtype=jnp.int32)
    seg_start = jnp.sum(jnp.where(is_expert, pad_start, 0), axis=-1)
    dest = (seg_start + rank).reshape(-1)
    sidx = jnp.concatenate([jnp.argsort(dest).astype(jnp.int32), jnp.zeros((DMA_GROUP,), jnp.int32)])
    groups = (counts + MOE_ITEM_ROWS - 1) // MOE_ITEM_ROWS
    g_end = jnp.cumsum(groups)
    g_start = g_end - groups
    n_items = N_EXPERTS + -(-(lp * TOP_K) // MOE_ITEM_ROWS)
    ii = jnp.arange(n_items, dtype=jnp.int32)
    valid = ii < g_end[-1]
    item_e = jnp.sum((g_end[None, :] <= jnp.minimum(ii, g_end[-1] - 1)[:, None]).astype(jnp.int32), axis=1)
    g_in = ii - g_start[item_e]
    item_row = jnp.where(valid, pad_start[item_e] + g_in * MOE_ITEM_ROWS, 0).astype(jnp.int32)
    item_rows = jnp.where(valid, jnp.minimum(MOE_ITEM_ROWS, counts[item_e] - g_in * MOE_ITEM_ROWS),
                          0).astype(jnp.int32)

    tok_row = jnp.where(sidx >= lp, sidx - lp, sidx)
    y = _moe_experts(item_e, item_row, item_rows, sidx, tok_row, f, expert_w_gate[0], expert_w_up[0],
                     expert_w_down[0], chunk=MOE_CHUNK)
    out = _combine(h1, route, y, norm_final_g[None])
    return out[None]
```
